```python
import jax, jax.numpy as jnp
from jax import lax
import numpy as np

D_MODEL = 2048
BATCH = 2
SEQ = 4096
DEPTH = 1

CHUNK = 64

GLA_HEADS = 4
GLA_DK = 128
GLA_DV = 256
GLA_QK = GLA_HEADS * GLA_DK
GLA_V = GLA_HEADS * GLA_DV
GLA_LORA = 16
GLA_TAU = 16.0

RWKV_HEADS = 16
RWKV_HD = 64
RWKV_W = RWKV_HEADS * RWKV_HD
DECAY_LORA = 96
AAA_LORA = 96
GATE_LORA = 256
GN_EPS = 64e-5

N_BRANCH = 2
D_FF = 5632
NORM_EPS = 1e-6

GLA_SPLITS = (GLA_QK, GLA_QK, GLA_V, GLA_V, GLA_LORA)
GLA_IN = 2 * GLA_QK + 2 * GLA_V + GLA_LORA
RWKV_SPLITS = (RWKV_W, RWKV_W, RWKV_W, DECAY_LORA, AAA_LORA, GATE_LORA)
RWKV_IN = 3 * RWKV_W + DECAY_LORA + AAA_LORA + GATE_LORA
D_IN = GLA_IN + RWKV_IN + N_BRANCH * D_MODEL
BRANCH_IN = GLA_V + RWKV_W

kernel_name = "hybrid_gla_rwkv7_macaron_block"


def _split(p, sizes):
    out, off = [], 0
    for s in sizes:
        out.append(p[..., off:off + s])
        off += s
    return out


def rmsnorm(x, g):
    xf = x.astype(jnp.float32)
    y = xf * lax.rsqrt(jnp.mean(xf * xf, axis=-1, keepdims=True) + NORM_EPS)
    return (y * g.astype(jnp.float32)).astype(x.dtype)


def swiglu(h, wg, wu, wd):
    return (jax.nn.silu(h @ wg) * (h @ wu)) @ wd


def token_shift(p, mu):
    prev = jnp.pad(p, ((0, 0), (1, 0), (0, 0)))[:, :-1]
    return p + mu * (prev - p)


def gla_branch(q, k, v, r, a_down, w_a2, b_a, gn_w):
    B, S, _ = q.shape
    nc = S // CHUNK
    f32 = jnp.float32
    log_alpha = jax.nn.log_sigmoid(a_down.astype(f32) @ w_a2.astype(f32) + b_a.astype(f32)) / GLA_TAU
    shp = (B, nc, CHUNK, GLA_HEADS, GLA_DK)
    qf = q.astype(f32).reshape(shp) * (GLA_DK ** -0.5)
    kf = k.astype(f32).reshape(shp)
    vf = v.astype(f32).reshape(B, nc, CHUNK, GLA_HEADS, GLA_DV)
    cum = jnp.cumsum(log_alpha.reshape(shp), axis=2)
    total = cum[:, :, -1]
    kdec = kf * jnp.exp(total[:, :, None] - cum)
    u = jnp.einsum('bnchk,bnchv->nbhkv', kdec, vf)

    def step(state, inp):
        lt, uc = inp
        state = jnp.exp(lt)[..., None] * state + uc
        return state, state

    s0 = jnp.zeros((B, GLA_HEADS, GLA_DK, GLA_DV), f32)
    _, states = lax.scan(step, s0, (jnp.moveaxis(total, 1, 0), u))
    o = jnp.einsum('bnchk,nbhkv->bnchv', qf, states)
    o = o * lax.rsqrt(jnp.mean(o * o, axis=-1, keepdims=True) + NORM_EPS) * gn_w.astype(f32)
    o = o.reshape(B, S, GLA_V) * jax.nn.silu(r.astype(f32))
    return o.astype(q.dtype)


def rwkv7_branch(r, k, v, wd, ad, gd, w0, w_w2, a0, w_a2, w_g2, k_k, k_a, r_k, lnx_w, lnx_b):
    B, S, _ = r.shape
    f32 = jnp.float32
    r, k, v = r.astype(f32), k.astype(f32), v.astype(f32)
    w_raw = w0.astype(f32) + jnp.tanh(wd.astype(f32)) @ w_w2.astype(f32)
    log_w = -jnp.exp(-jax.nn.softplus(-w_raw) - 0.5)
    a = jax.nn.sigmoid(a0.astype(f32) + ad.astype(f32) @ w_a2.astype(f32))
    g = jax.nn.sigmoid(gd.astype(f32)) @ w_g2.astype(f32)
    hs = (B, S, RWKV_HEADS, RWKV_HD)
    kk = (k * k_k.astype(f32)).reshape(hs)
    kk = kk / jnp.maximum(jnp.linalg.norm(kk, axis=-1, keepdims=True), 1e-12)
    k = k * (1.0 + (a - 1.0) * k_a.astype(f32))
    rh, kh, vh, ah = r.reshape(hs), k.reshape(hs), v.reshape(hs), a.reshape(hs)
    decay = jnp.exp(log_w).reshape(hs)
    b = kk * ah

    def step(state, inp):
        r_t, w_t, k_t, v_t, kk_t, b_t = inp
        sa = jnp.einsum('bhvk,bhk->bhv', state, kk_t)
        state = (state * w_t[:, :, None, :] - sa[..., None] * b_t[:, :, None, :]
                 + v_t[..., None] * k_t[:, :, None, :])
        return state, jnp.einsum('bhvk,bhk->bhv', state, r_t)

    xs = tuple(jnp.moveaxis(t, 1, 0) for t in (rh, decay, kh, vh, kk, b))
    s0 = jnp.zeros((B, RWKV_HEADS, RWKV_HD, RWKV_HD), f32)
    _, y = lax.scan(step, s0, xs)
    y = jnp.moveaxis(y, 0, 1)
    mu = jnp.mean(y, axis=-1, keepdims=True)
    var = jnp.mean(jnp.square(y - mu), axis=-1, keepdims=True)
    yn = ((y - mu) * lax.rsqrt(var + GN_EPS)).reshape(B, S, RWKV_W)
    yn = yn * lnx_w.astype(f32) + lnx_b.astype(f32)
    bonus = (jnp.sum(rh * kh * r_k.astype(f32), axis=-1, keepdims=True) * vh).reshape(B, S, RWKV_W)
    return ((yn + bonus) * g).astype(r.dtype)


def hybrid_mixer(h, w_in, gla_w_a2, gla_b_a, gla_gn_w, rwkv_mu, rwkv_w0, rwkv_w_w2,
                 rwkv_a0, rwkv_w_a2, rwkv_w_g2, rwkv_k_k, rwkv_k_a, rwkv_r_k,
                 rwkv_lnx_w, rwkv_lnx_b, gate_b, w_branch, w_out):
    p = h @ w_in
    gla_p = p[..., :GLA_IN]
    rw_p = token_shift(p[..., GLA_IN:GLA_IN + RWKV_IN], rwkv_mu)
    gate_p = p[..., GLA_IN + RWKV_IN:]
    gq, gk, gv, gr, gad = _split(gla_p, GLA_SPLITS)
    rr, rk, rv, rwd, rad, rgd = _split(rw_p, RWKV_SPLITS)
    o_gla = gla_branch(gq, gk, gv, gr, gad, gla_w_a2, gla_b_a, gla_gn_w)
    o_rw = rwkv7_branch(rr, rk, rv, rwd, rad, rgd, rwkv_w0, rwkv_w_w2, rwkv_a0, rwkv_w_a2,
                        rwkv_w_g2, rwkv_k_k, rwkv_k_a, rwkv_r_k, rwkv_lnx_w, rwkv_lnx_b)
    gates = jax.nn.sigmoid((gate_p + gate_b).astype(jnp.float32))
    y_gla = (o_gla @ w_branch[:GLA_V]).astype(jnp.float32)
    y_rw = (o_rw @ w_branch[GLA_V:]).astype(jnp.float32)
    merged = gates[..., :D_MODEL] * y_gla + gates[..., D_MODEL:] * y_rw
    return merged.astype(h.dtype) @ w_out


def setup_inputs(seed: int = 0) -> dict:
    key = jax.random.key(seed)
    ks = iter(jax.random.split(key, 40))
    L, D = DEPTH, D_MODEL

    def nrm(shape, scale):
        return scale * jax.random.normal(next(ks), shape, jnp.float32)

    def gain(shape):
        return 1.0 + nrm(shape, 0.02)

    return {
        "x": nrm((BATCH, SEQ, D), 1.0),
        "ffn1_norm": gain((L, D)),
        "ffn1_wg": nrm((L, D, D_FF), D ** -0.5),
        "ffn1_wu": nrm((L, D, D_FF), D ** -0.5),
        "ffn1_wd": nrm((L, D_FF, D), D_FF ** -0.5),
        "mix_norm": gain((L, D)),
        "w_in": nrm((L, D, D_IN), D ** -0.5),
        "gla_w_a2": nrm((L, GLA_LORA, GLA_QK), GLA_LORA ** -0.5),
        "gla_b_a": nrm((L, GLA_QK), 0.1),
        "gla_gn_w": gain((L, GLA_DV)),
        "rwkv_mu": jax.random.uniform(next(ks), (L, RWKV_IN), jnp.float32, 0.0, 1.0),
        "rwkv_w0": -2.0 + nrm((L, RWKV_W), 0.5),
        "rwkv_w_w2": nrm((L, DECAY_LORA, RWKV_W), 0.3 * DECAY_LORA ** -0.5),
        "rwkv_a0": nrm((L, RWKV_W), 0.1),
        "rwkv_w_a2": nrm((L, AAA_LORA, RWKV_W), 0.3 * AAA_LORA ** -0.5),
        "rwkv_w_g2": nrm((L, GATE_LORA, RWKV_W), GATE_LORA ** -0.5),
        "rwkv_k_k": 0.85 + nrm((L, RWKV_W), 0.05),
        "rwkv_k_a": gain((L, RWKV_W)),
        "rwkv_r_k": nrm((L, RWKV_HEADS, RWKV_HD), 0.1),
        "rwkv_lnx_w": gain((L, RWKV_W)),
        "rwkv_lnx_b": nrm((L, RWKV_W), 0.01),
        "gate_b": nrm((L, N_BRANCH * D), 0.1),
        "w_branch": nrm((L, BRANCH_IN, D), GLA_V ** -0.5),
        "w_out": nrm((L, D, D), D ** -0.5),
        "ffn2_norm": gain((L, D)),
        "ffn2_wg": nrm((L, D, D_FF), D ** -0.5),
        "ffn2_wu": nrm((L, D, D_FF), D ** -0.5),
        "ffn2_wd": nrm((L, D_FF, D), D_FF ** -0.5),
        "final_norm": gain((D,)),
    }


def reference(x, ffn1_norm, ffn1_wg, ffn1_wu, ffn1_wd, mix_norm, w_in, gla_w_a2, gla_b_a,
              gla_gn_w, rwkv_mu, rwkv_w0, rwkv_w_w2, rwkv_a0, rwkv_w_a2, rwkv_w_g2, rwkv_k_k,
              rwkv_k_a, rwkv_r_k, rwkv_lnx_w, rwkv_lnx_b, gate_b, w_branch, w_out,
              ffn2_norm, ffn2_wg, ffn2_wu, ffn2_wd, final_norm):
    for l in range(DEPTH):
        h = rmsnorm(x, ffn1_norm[l])
        x = x + 0.5 * swiglu(h, ffn1_wg[l], ffn1_wu[l], ffn1_wd[l])
        h = rmsnorm(x, mix_norm[l])
        x = x + hybrid_mixer(h, w_in[l], gla_w_a2[l], gla_b_a[l], gla_gn_w[l], rwkv_mu[l],
                             rwkv_w0[l], rwkv_w_w2[l], rwkv_a0[l], rwkv_w_a2[l], rwkv_w_g2[l],
                             rwkv_k_k[l], rwkv_k_a[l], rwkv_r_k[l], rwkv_lnx_w[l], rwkv_lnx_b[l],
                             gate_b[l], w_branch[l], w_out[l])
        h = rmsnorm(x, ffn2_norm[l])
        x = x + 0.5 * swiglu(h, ffn2_wg[l], ffn2_wu[l], ffn2_wd[l])
    return rmsnorm(x, final_norm)
```

```python
import functools

import jax
import jax.numpy as jnp
from jax import lax
from jax.experimental import pallas as pl
from jax.experimental.pallas import tpu as pltpu

F32 = jnp.float32
BF16 = jnp.bfloat16

NORM_EPS = 1e-6
GN_EPS = 64e-5
GLA_TAU = 16.0
CHUNK = 64

GLA_HEADS = 4
GLA_DK = 128
GLA_DV = 256
GLA_QK = GLA_HEADS * GLA_DK
GLA_V = GLA_HEADS * GLA_DV
GLA_LORA = 16

RWKV_HD = 64
RWKV_W = 1024
DECAY_LORA = 96
AAA_LORA = 96
GATE_LORA = 256

LANE = 128
PAIR = 2 * RWKV_HD
N_PAIR = RWKV_W // PAIR

RW_GROUP = 3 * RWKV_W + LANE + LANE + GATE_LORA
GLA_GROUP = RW_GROUP
RW_WD = 3 * RWKV_W
GLA_AD = 2 * GLA_QK + 2 * GLA_V
RW_AD = RW_WD + LANE
RW_GD = RW_AD + LANE

VMEM_LIMIT = 48 * 1024 * 1024


def _cparams(sem):
    return pltpu.CompilerParams(dimension_semantics=sem, vmem_limit_bytes=VMEM_LIMIT)


def _dot(a, b):
    return jnp.dot(a, b, preferred_element_type=F32)


def _dot_nt(a, b):
    return lax.dot_general(a, b, (((1,), (1,)), ((), ())), preferred_element_type=F32)


def _dot_tn(a, b):
    return lax.dot_general(a, b, (((0,), (0,)), ((), ())), preferred_element_type=F32)


def _dot_split(a, b_bf16):
    hi = a.astype(BF16)
    lo = (a - hi.astype(F32)).astype(BF16)
    return _dot(hi, b_bf16) + _dot(lo, b_bf16)


def _rmsnorm(x, g):
    return x * lax.rsqrt(jnp.mean(x * x, axis=-1, keepdims=True) + NORM_EPS) * g


def _sigmoid(z):
    return 1.0 / (1.0 + jnp.exp(-z))


def _softplus(z):
    return jnp.maximum(z, 0.0) + jnp.log(1.0 + jnp.exp(-jnp.abs(z)))


def _ffn_kernel(x_ref, g_ref, wg_ref, wu_ref, wd_ref, fn_ref, o_ref, h_scr, acc_scr, *, final_norm):
    j = pl.program_id(1)

    @pl.when(j == 0)
    def _():
        h_scr[...] = _rmsnorm(x_ref[...], g_ref[...]).astype(BF16)
        acc_scr[...] = jnp.zeros_like(acc_scr)

    h = h_scr[...]
    a = _dot(h, wg_ref[...])
    u = _dot(h, wu_ref[...])
    act = (a * _sigmoid(a) * u).astype(BF16)
    acc_scr[...] += _dot(act, wd_ref[...])

    @pl.when(j == pl.num_programs(1) - 1)
    def _():
        y = x_ref[...] + 0.5 * acc_scr[...]
        if final_norm:
            y = _rmsnorm(y, fn_ref[...])
        o_ref[...] = y


def _ffn(x, g, wg, wu, wd, fn, *, final_norm, tm=512, tf=512):
    T, D = x.shape
    FF = wg.shape[1]
    return pl.pallas_call(
        functools.partial(_ffn_kernel, final_norm=final_norm),
        out_shape=jax.ShapeDtypeStruct((T, D), F32),
        grid=(T // tm, FF // tf),
        in_specs=[
            pl.BlockSpec((tm, D), lambda i, j: (i, 0)),
            pl.BlockSpec((1, D), lambda i, j: (0, 0)),
            pl.BlockSpec((D, tf), lambda i, j: (0, j)),
            pl.BlockSpec((D, tf), lambda i, j: (0, j)),
            pl.BlockSpec((tf, D), lambda i, j: (j, 0)),
            pl.BlockSpec((1, D), lambda i, j: (0, 0)),
        ],
        out_specs=pl.BlockSpec((tm, D), lambda i, j: (i, 0)),
        scratch_shapes=[pltpu.VMEM((tm, D), BF16), pltpu.VMEM((tm, D), F32)],
        compiler_params=_cparams(("parallel", "arbitrary")),
        name="ffn_final" if final_norm else "ffn",
    )(x, g, wg, wu, wd, fn)


def _proj_kernel(x_ref, g_ref, w_ref, b_ref, o_ref, h_scr, *, gate):
    @pl.when(pl.program_id(1) == 0)
    def _():
        h_scr[...] = _rmsnorm(x_ref[...], g_ref[...]).astype(BF16)

    y = _dot(h_scr[...], w_ref[...])
    if gate:
        y = _sigmoid(y + b_ref[...])
    o_ref[...] = y


def _norm_proj(x, g, w, b, *, gate, tm=512, tn=1024):
    T, D = x.shape
    N = w.shape[1]
    return pl.pallas_call(
        functools.partial(_proj_kernel, gate=gate),
        out_shape=jax.ShapeDtypeStruct((T, N), F32),
        grid=(T // tm, N // tn),
        in_specs=[
            pl.BlockSpec((tm, D), lambda i, j: (i, 0)),
            pl.BlockSpec((1, D), lambda i, j: (0, 0)),
            pl.BlockSpec((D, tn), lambda i, j: (0, j)),
            pl.BlockSpec((1, tn), lambda i, j: (0, j)),
        ],
        out_specs=pl.BlockSpec((tm, tn), lambda i, j: (i, j)),
        scratch_shapes=[pltpu.VMEM((tm, D), BF16)],
        compiler_params=_cparams(("parallel", "arbitrary")),
        name="gate_proj" if gate else "in_proj",
    )(x, g, w, b)


def _chunk_tri(n):
    r = lax.broadcasted_iota(jnp.int32, (n, n), 0)
    c = lax.broadcasted_iota(jnp.int32, (n, n), 1)
    return jnp.where((r >= c) & (r // CHUNK == c // CHUNK), 1.0, 0.0).astype(BF16)


def _gla_kernel(p_ref, wa2_ref, ba_ref, gn_ref, o_ref, st_scr, *, tg):
    @pl.when(pl.program_id(1) == 0)
    def _():
        st_scr[...] = jnp.zeros_like(st_scr)

    gad = p_ref[:, GLA_AD:GLA_AD + LANE].astype(BF16)
    z = _dot(gad, wa2_ref[...]) + ba_ref[...]
    log_alpha = -_softplus(-z) * (1.0 / GLA_TAU)
    tri = _chunk_tri(tg)
    hi = log_alpha.astype(BF16)
    lo = (log_alpha - hi.astype(F32)).astype(BF16)
    cum = _dot(tri, hi) + _dot(tri, lo)
    gn = gn_ref[...]
    scale = GLA_DK ** -0.5

    for ci in range(tg // CHUNK):
        rows = slice(ci * CHUNK, (ci + 1) * CHUNK)
        cum_c = cum[rows, :]
        tot = cum_c[CHUNK - 1:CHUNK, :]
        kdec = (p_ref[rows, GLA_QK:2 * GLA_QK] * jnp.exp(tot - cum_c)).astype(BF16)
        etot = jnp.exp(tot)
        q = (p_ref[rows, 0:GLA_QK] * scale).astype(BF16)
        for h in range(GLA_HEADS):
            ks = slice(h * GLA_DK, (h + 1) * GLA_DK)
            v_h = p_ref[rows, 2 * GLA_QK + h * GLA_DV:2 * GLA_QK + (h + 1) * GLA_DV].astype(BF16)
            r_h = p_ref[rows, 2 * GLA_QK + GLA_V + h * GLA_DV:2 * GLA_QK + GLA_V + (h + 1) * GLA_DV]
            st = st_scr[h] * etot[:, ks] + _dot_tn(v_h, kdec[:, ks])
            st_scr[h] = st
            o = _dot_nt(q[:, ks], st.astype(BF16))
            o = o * lax.rsqrt(jnp.mean(o * o, axis=-1, keepdims=True) + NORM_EPS) * gn
            o = o * (r_h * _sigmoid(r_h))
            o_ref[rows, h * GLA_DV:(h + 1) * GLA_DV] = o.astype(BF16)


def _gla(p, wa2, ba, gn, B, S, *, tg=256):
    T = B * S
    nb = S // tg
    return pl.pallas_call(
        functools.partial(_gla_kernel, tg=tg),
        out_shape=jax.ShapeDtypeStruct((T, GLA_V), BF16),
        grid=(B, nb),
        in_specs=[
            pl.BlockSpec((tg, GLA_GROUP), lambda b, i: (b * nb + i, 1)),
            pl.BlockSpec((LANE, GLA_QK), lambda b, i: (0, 0)),
            pl.BlockSpec((1, GLA_QK), lambda b, i: (0, 0)),
            pl.BlockSpec((1, GLA_DV), lambda b, i: (0, 0)),
        ],
        out_specs=pl.BlockSpec((tg, GLA_V), lambda b, i: (b * nb + i, 0)),
        scratch_shapes=[pltpu.VMEM((GLA_HEADS, GLA_DV, GLA_DK), F32)],
        compiler_params=_cparams(("parallel", "arbitrary")),
        name="gla",
    )(p, wa2, ba, gn)


def _head_ones(n):
    r = lax.broadcasted_iota(jnp.int32, (n, n), 0)
    c = lax.broadcasted_iota(jnp.int32, (n, n), 1)
    return jnp.where(r // RWKV_HD == c // RWKV_HD, 1.0, 0.0).astype(BF16)


def _head_sum(x, ones_pair):
    parts = [_dot_split(x[:, j * PAIR:(j + 1) * PAIR], ones_pair) for j in range(x.shape[1] // PAIR)]
    return jnp.concatenate(parts, axis=1)


def _rwkv_prep_kernel(p_ref, mu_ref, w0_ref, ww2_ref, a0_ref, wa2_ref, wg2_ref, kk_ref, ka_ref, rk_ref,
                      rt_ref, bt_ref, at_ref, kt_ref, at2_ref, kt2_ref, v_ref, bonus_ref, g_ref, ptot_ref,
                      carry_scr, *, tm):
    @pl.when(pl.program_id(1) == 0)
    def _():
        carry_scr[...] = jnp.zeros_like(carry_scr)

    p = p_ref[...]
    last = carry_scr[...]
    carry_scr[...] = p[tm - 1:tm, :]
    row = lax.broadcasted_iota(jnp.int32, p.shape, 0)
    prev = jnp.where(row == 0, last, pltpu.roll(p, 1, axis=0))
    p = p + mu_ref[...] * (prev - p)

    r = p[:, 0:RWKV_W]
    k = p[:, RWKV_W:2 * RWKV_W]
    v = p[:, 2 * RWKV_W:3 * RWKV_W]
    wd = p[:, RW_WD:RW_WD + LANE]
    ad = p[:, RW_AD:RW_AD + LANE]
    gd = p[:, RW_GD:RW_GD + GATE_LORA]

    w_raw = w0_ref[...] + _dot(jnp.tanh(wd).astype(BF16), ww2_ref[...])
    log_w = -jnp.exp(-_softplus(-w_raw) - 0.5)
    a = _sigmoid(a0_ref[...] + _dot(ad.astype(BF16), wa2_ref[...]))
    g_ref[...] = _dot(_sigmoid(gd).astype(BF16), wg2_ref[...])

    ones_pair = _head_ones(PAIR)
    kk = k * kk_ref[...]
    kk = kk / jnp.maximum(jnp.sqrt(_head_sum(kk * kk, ones_pair)), 1e-12)
    kp = k * (1.0 + (a - 1.0) * ka_ref[...])
    bonus_ref[...] = _head_sum(r * kp * rk_ref[...], ones_pair) * v
    v_ref[...] = v.astype(BF16)

    tri = _chunk_tri(tm)
    hi = log_w.astype(BF16)
    lo = (log_w - hi.astype(F32)).astype(BF16)
    cum = _dot(tri, hi) + _dot(tri, lo)
    nalpha = -(kk * a)
    for ci in range(tm // CHUNK):
        rows = slice(ci * CHUNK, (ci + 1) * CHUNK)
        cum_c = cum[rows, :]
        tot = cum_c[CHUNK - 1:CHUNK, :]
        e_fwd = jnp.exp(cum_c)
        e_inv = jnp.exp(-cum_c)
        e_tail = jnp.exp(tot - cum_c)
        rt_ref[rows, :] = (r[rows, :] * e_fwd).astype(BF16)
        bt_ref[rows, :] = (kk[rows, :] * jnp.exp(cum_c - log_w[rows, :])).astype(BF16)
        at_ref[rows, :] = (nalpha[rows, :] * e_inv).astype(BF16)
        kt_ref[rows, :] = (kp[rows, :] * e_inv).astype(BF16)
        at2_ref[rows, :] = (nalpha[rows, :] * e_tail).astype(BF16)
        kt2_ref[rows, :] = (kp[rows, :] * e_tail).astype(BF16)
        ptot_ref[ci] = jnp.exp(tot)


def _rwkv_prep(p, mu, w0, ww2, a0, wa2, wg2, k_k, k_a, r_k, B, S, *, tm=256):
    T = B * S
    nb = S // tm
    cpt = tm // CHUNK
    vec = lambda n: pl.BlockSpec((1, n), lambda b, i: (0, 0))
    tok = lambda: pl.BlockSpec((tm, RWKV_W), lambda b, i: (b * nb + i, 0))
    bf = jax.ShapeDtypeStruct((T, RWKV_W), BF16)
    f32 = jax.ShapeDtypeStruct((T, RWKV_W), F32)
    return pl.pallas_call(
        functools.partial(_rwkv_prep_kernel, tm=tm),
        out_shape=[bf, bf, bf, bf, bf, bf, bf, f32, f32,
                   jax.ShapeDtypeStruct((T // CHUNK, 1, RWKV_W), F32)],
        grid=(B, nb),
        in_specs=[
            pl.BlockSpec((tm, RW_GROUP), lambda b, i: (b * nb + i, 0)),
            vec(RW_GROUP), vec(RWKV_W),
            pl.BlockSpec((LANE, RWKV_W), lambda b, i: (0, 0)),
            vec(RWKV_W),
            pl.BlockSpec((LANE, RWKV_W), lambda b, i: (0, 0)),
            pl.BlockSpec((GATE_LORA, RWKV_W), lambda b, i: (0, 0)),
            vec(RWKV_W), vec(RWKV_W), vec(RWKV_W),
        ],
        out_specs=[tok() for _ in range(9)]
        + [pl.BlockSpec((cpt, 1, RWKV_W), lambda b, i: (b * nb + i, 0, 0))],
        scratch_shapes=[pltpu.VMEM((1, RW_GROUP), F32)],
        compiler_params=_cparams(("parallel", "arbitrary")),
        name="rwkv_prep",
    )(p, mu, w0, ww2, a0, wa2, wg2, k_k, k_a, r_k)


def _rwkv_core_kernel(rt_ref, bt_ref, at_ref, kt_ref, at2_ref, kt2_ref, v_ref, bonus_ref, g_ref, ptot_ref,
                      lw_ref, lb_ref, o_ref, h_scr):
    @pl.when(pl.program_id(1) == 0)
    def _():
        h_scr[...] = jnp.zeros_like(h_scr)

    ri = lax.broadcasted_iota(jnp.int32, (PAIR, PAIR), 0)
    ci = lax.broadcasted_iota(jnp.int32, (PAIR, PAIR), 1)
    head_blk = (ri // RWKV_HD) == (ci // RWKV_HD)
    strict = ri > ci
    lower = ri >= ci
    eye = ri == ci
    blk8 = (ri // 8) == (ci // 8)
    eye_f = jnp.where(eye, 1.0, 0.0)
    ones_pair = jnp.where(head_blk, 1.0, 0.0).astype(BF16)
    zeros_b = jnp.zeros((PAIR, PAIR), BF16)

    def stack(ref, lanes):
        x = ref[:, lanes]
        return jnp.where(head_blk, jnp.concatenate([x, x], axis=0), jnp.zeros((), x.dtype))

    for j in range(N_PAIR):
        lanes = slice(j * PAIR, (j + 1) * PAIR)
        bx = stack(bt_ref, lanes)
        rx = stack(rt_ref, lanes)
        ak = jnp.concatenate([stack(at_ref, lanes), stack(kt_ref, lanes)], axis=0)
        ak2 = jnp.concatenate([stack(at2_ref, lanes), stack(kt2_ref, lanes)], axis=0)
        vx = stack(v_ref, lanes)

        gb = _dot_nt(bx, ak)
        gr = _dot_nt(rx, ak)
        a_ab = jnp.where(strict, gb[:, :PAIR], 0.0)
        a_kb = jnp.where(strict, gb[:, PAIR:], 0.0)
        a_r = jnp.concatenate([jnp.where(lower, gr[:, :PAIR], 0.0),
                               jnp.where(lower, gr[:, PAIR:], 0.0)], axis=1).astype(BF16)

        a_d = jnp.where(blk8, a_ab, 0.0)
        a_db = a_d.astype(BF16)
        pw = _dot(a_db, a_db).astype(BF16)
        s = eye_f + a_d
        sp = _dot(pw, jnp.concatenate([s.astype(BF16), pw], axis=1))
        s = s + sp[:, :PAIR]
        pw = sp[:, PAIR:].astype(BF16)
        s = s + _dot(pw, s.astype(BF16))
        for width in (8, 16, 32):
            off = ((ri // (2 * width)) == (ci // (2 * width))) & ((ri // width) != (ci // width))
            e = jnp.where(off, a_ab, 0.0).astype(BF16)
            sb = s.astype(BF16)
            s = s + _dot(sb, _dot(e, sb).astype(BF16))
        t_inv = s.astype(BF16)

        rhs = jnp.concatenate([bx, _dot(a_kb.astype(BF16), vx).astype(BF16)], axis=1)
        wu = _dot(t_inv, rhs).astype(BF16)
        z = jnp.concatenate([wu, jnp.concatenate([zeros_b, vx], axis=1)], axis=0)
        mc = _dot_tn(ak2, z)
        qy = _dot(a_r, z)

        ptot = ptot_ref[0, :, lanes]
        m = (mc[:, :PAIR] + jnp.where(eye, ptot, 0.0)).astype(BF16)
        q = (qy[:, :PAIR] + rx.astype(F32)).astype(BF16)
        h = h_scr[j]
        hb = h.astype(BF16)
        y = _dot(q, hb) + qy[:, PAIR:]
        h_scr[j] = _dot(m, hb) + mc[:, PAIR:]
        y = y[:RWKV_HD, :] + y[RWKV_HD:, :]

        mean = _dot_split(y, ones_pair) * (1.0 / RWKV_HD)
        yc = y - mean
        var = _dot_split(yc * yc, ones_pair) * (1.0 / RWKV_HD)
        yn = yc * lax.rsqrt(var + GN_EPS) * lw_ref[:, lanes] + lb_ref[:, lanes]
        o_ref[:, lanes] = ((yn + bonus_ref[:, lanes]) * g_ref[:, lanes]).astype(BF16)


def _rwkv_core(rt, bt, at, kt, at2, kt2, v, bonus, g, ptot, lnx_w, lnx_b, B, S):
    T = B * S
    nc = S // CHUNK
    tok = lambda: pl.BlockSpec((CHUNK, RWKV_W), lambda b, c: (b * nc + c, 0))
    vec = lambda: pl.BlockSpec((1, RWKV_W), lambda b, c: (0, 0))
    return pl.pallas_call(
        _rwkv_core_kernel,
        out_shape=jax.ShapeDtypeStruct((T, RWKV_W), BF16),
        grid=(B, nc),
        in_specs=[tok() for _ in range(9)]
        + [pl.BlockSpec((1, 1, RWKV_W), lambda b, c: (b * nc + c, 0, 0)), vec(), vec()],
        out_specs=tok(),
        scratch_shapes=[pltpu.VMEM((N_PAIR, PAIR, PAIR), F32)],
        compiler_params=_cparams(("parallel", "arbitrary")),
        name="rwkv_core",
    )(rt, bt, at, kt, at2, kt2, v, bonus, g, ptot, lnx_w, lnx_b)


def _merge_kernel(x_ref, og_ref, or_ref, gt_ref, wb1_ref, wb2_ref, wo_ref, o_ref):
    D = x_ref.shape[1]
    y_gla = _dot(og_ref[...], wb1_ref[...])
    y_rw = _dot(or_ref[...], wb2_ref[...])
    merged = gt_ref[:, :D] * y_gla + gt_ref[:, D:] * y_rw
    o_ref[...] = x_ref[...] + _dot(merged.astype(BF16), wo_ref[...])


def _merge(x, o_gla, o_rw, gates, wb1, wb2, wo, *, tm=256):
    T, D = x.shape
    const = lambda shape: pl.BlockSpec(shape, lambda i: (0, 0), pipeline_mode=pl.Buffered(1))
    return pl.pallas_call(
        _merge_kernel,
        out_shape=jax.ShapeDtypeStruct((T, D), F32),
        grid=(T // tm,),
        in_specs=[
            pl.BlockSpec((tm, D), lambda i: (i, 0)),
            pl.BlockSpec((tm, GLA_V), lambda i: (i, 0)),
            pl.BlockSpec((tm, RWKV_W), lambda i: (i, 0)),
            pl.BlockSpec((tm, 2 * D), lambda i: (i, 0)),
            const((GLA_V, D)), const((RWKV_W, D)), const((D, D)),
        ],
        out_specs=pl.BlockSpec((tm, D), lambda i: (i, 0)),
        compiler_params=_cparams(("parallel",)),
        name="merge_out",
    )(x, o_gla, o_rw, gates, wb1, wb2, wo)


def _pad_cols(w, n):
    return jnp.pad(w, ((0, 0), (0, n - w.shape[1])))


def _pad_rows(w, n):
    return jnp.pad(w, ((0, n - w.shape[0]), (0, 0)))


def _pack_in_proj(w_in, rwkv_mu):
    gla_in = 2 * GLA_QK + 2 * GLA_V + GLA_LORA
    rw_in = 3 * RWKV_W + DECAY_LORA + AAA_LORA + GATE_LORA
    w_gla = w_in[:, :gla_in]
    w_rw = w_in[:, gla_in:gla_in + rw_in]
    w_gate = w_in[:, gla_in + rw_in:]

    def rw_pack(m):
        main = m[:, :3 * RWKV_W]
        wd = _pad_cols(m[:, 3 * RWKV_W:3 * RWKV_W + DECAY_LORA], LANE)
        ad = _pad_cols(m[:, 3 * RWKV_W + DECAY_LORA:3 * RWKV_W + DECAY_LORA + AAA_LORA], LANE)
        gd = m[:, 3 * RWKV_W + DECAY_LORA + AAA_LORA:]
        return jnp.concatenate([main, wd, ad, gd], axis=1)

    w_a = jnp.concatenate([rw_pack(w_rw), _pad_cols(w_gla, GLA_GROUP)], axis=1)
    return w_a.astype(BF16), w_gate.astype(BF16), rw_pack(rwkv_mu[None, :])


def kernel(x, ffn1_norm, ffn1_wg, ffn1_wu, ffn1_wd, mix_norm, w_in, gla_w_a2, gla_b_a, gla_gn_w, rwkv_mu,
           rwkv_w0, rwkv_w_w2, rwkv_a0, rwkv_w_a2, rwkv_w_g2, rwkv_k_k, rwkv_k_a, rwkv_r_k, rwkv_lnx_w,
           rwkv_lnx_b, gate_b, w_branch, w_out, ffn2_norm, ffn2_wg, ffn2_wu, ffn2_wd, final_norm):
    B, S, D = x.shape
    T = B * S
    depth = ffn1_norm.shape[0]
    bf = lambda w: w.astype(BF16)
    row = lambda v: v.reshape(1, -1)
    xt = x.reshape(T, D)
    for l in range(depth):
        last = l == depth - 1
        xt = _ffn(xt, row(ffn1_norm[l]), bf(ffn1_wg[l]), bf(ffn1_wu[l]), bf(ffn1_wd[l]), row(final_norm),
                  final_norm=False)

        w_a, w_gate, mu = _pack_in_proj(w_in[l], rwkv_mu[l])
        p = _norm_proj(xt, row(mix_norm[l]), w_a, jnp.zeros((1, w_a.shape[1]), F32), gate=False)
        gates = _norm_proj(xt, row(mix_norm[l]), w_gate, row(gate_b[l]), gate=True)

        o_gla = _gla(p, bf(_pad_rows(gla_w_a2[l], LANE)), row(gla_b_a[l]), row(gla_gn_w[l]), B, S)

        prep = _rwkv_prep(p, mu, row(rwkv_w0[l]), bf(_pad_rows(rwkv_w_w2[l], LANE)), row(rwkv_a0[l]),
                          bf(_pad_rows(rwkv_w_a2[l], LANE)), bf(rwkv_w_g2[l]), row(rwkv_k_k[l]),
                          row(rwkv_k_a[l]), row(rwkv_r_k[l]), B, S)
        o_rw = _rwkv_core(*prep, row(rwkv_lnx_w[l]), row(rwkv_lnx_b[l]), B, S)

        xt = _merge(xt, o_gla, o_rw, gates, bf(w_branch[l, :GLA_V]), bf(w_branch[l, GLA_V:]), bf(w_out[l]))

        xt = _ffn(xt, row(ffn2_norm[l]), bf(ffn2_wg[l]), bf(ffn2_wu[l]), bf(ffn2_wd[l]), row(final_norm),
                  final_norm=last)
    if depth == 0:
        raise ValueError("depth must be >= 1")
    return xt.reshape(B, S, D)
```

```python
import functools

import jax
import jax.numpy as jnp
from jax import lax
from jax.experimental import pallas as pl
from jax.experimental.pallas import tpu as pltpu

F32 = jnp.float32
BF16 = jnp.bfloat16

NORM_EPS = 1e-6
GN_EPS = 64e-5
GLA_TAU = 16.0
CHUNK = 64

GLA_HEADS = 4
GLA_DK = 128
GLA_DV = 256
GLA_QK = GLA_HEADS * GLA_DK
GLA_V = GLA_HEADS * GLA_DV
GLA_LORA = 16

RWKV_HD = 64
RWKV_W = 1024
DECAY_LORA = 96
AAA_LORA = 96
GATE_LORA = 256

LANE = 128
PAIR = 2 * RWKV_HD
N_PAIR = RWKV_W // PAIR

RW_GROUP = 3 * RWKV_W + LANE + LANE + GATE_LORA
GLA_GROUP = RW_GROUP
RW_WD = 3 * RWKV_W
GLA_AD = 2 * GLA_QK + 2 * GLA_V
RW_AD = RW_WD + LANE
RW_GD = RW_AD + LANE

VMEM_LIMIT = 48 * 1024 * 1024


def _cparams(sem):
    return pltpu.CompilerParams(dimension_semantics=sem, vmem_limit_bytes=VMEM_LIMIT)


def _dot(a, b):
    return jnp.dot(a, b, preferred_element_type=F32)


def _dot_nt(a, b):
    return lax.dot_general(a, b, (((1,), (1,)), ((), ())), preferred_element_type=F32)


def _dot_tn(a, b):
    return lax.dot_general(a, b, (((0,), (0,)), ((), ())), preferred_element_type=F32)


def _dot_split(a, b_bf16):
    hi = a.astype(BF16)
    lo = (a - hi.astype(F32)).astype(BF16)
    return _dot(hi, b_bf16) + _dot(lo, b_bf16)


def _rmsnorm(x, g):
    return x * lax.rsqrt(jnp.mean(x * x, axis=-1, keepdims=True) + NORM_EPS) * g


def _sigmoid(z):
    return 1.0 / (1.0 + jnp.exp(-z))


def _softplus(z):
    return jnp.maximum(z, 0.0) + jnp.log(1.0 + jnp.exp(-jnp.abs(z)))


def _ffn_kernel(x_ref, g_ref, wg_ref, wu_ref, wd_ref, fn_ref, o_ref, h_scr, acc_scr, *, final_norm):
    j = pl.program_id(1)

    @pl.when(j == 0)
    def _():
        h_scr[...] = _rmsnorm(x_ref[...], g_ref[...]).astype(BF16)
        acc_scr[...] = jnp.zeros_like(acc_scr)

    h = h_scr[...]
    a = _dot(h, wg_ref[...])
    u = _dot(h, wu_ref[...])
    act = (a * _sigmoid(a) * u).astype(BF16)
    acc_scr[...] += _dot(act, wd_ref[...])

    @pl.when(j == pl.num_programs(1) - 1)
    def _():
        y = x_ref[...] + 0.5 * acc_scr[...]
        if final_norm:
            y = _rmsnorm(y, fn_ref[...])
        o_ref[...] = y


def _ffn(x, g, wg, wu, wd, fn, *, final_norm, tm=512, tf=512):
    T, D = x.shape
    FF = wg.shape[1]
    return pl.pallas_call(
        functools.partial(_ffn_kernel, final_norm=final_norm),
        out_shape=jax.ShapeDtypeStruct((T, D), F32),
        grid=(T // tm, FF // tf),
        in_specs=[
            pl.BlockSpec((tm, D), lambda i, j: (i, 0)),
            pl.BlockSpec((1, D), lambda i, j: (0, 0)),
            pl.BlockSpec((D, tf), lambda i, j: (0, j)),
            pl.BlockSpec((D, tf), lambda i, j: (0, j)),
            pl.BlockSpec((tf, D), lambda i, j: (j, 0)),
            pl.BlockSpec((1, D), lambda i, j: (0, 0)),
        ],
        out_specs=pl.BlockSpec((tm, D), lambda i, j: (i, 0)),
        scratch_shapes=[pltpu.VMEM((tm, D), BF16), pltpu.VMEM((tm, D), F32)],
        compiler_params=_cparams(("parallel", "arbitrary")),
        name="ffn_final" if final_norm else "ffn",
    )(x, g, wg, wu, wd, fn)


def _proj_kernel(x_ref, g_ref, w_ref, b_ref, o_ref, h_scr, *, gate):
    @pl.when(pl.program_id(1) == 0)
    def _():
        h_scr[...] = _rmsnorm(x_ref[...], g_ref[...]).astype(BF16)

    y = _dot(h_scr[...], w_ref[...])
    if gate:
        y = _sigmoid(y + b_ref[...])
    o_ref[...] = y


def _norm_proj(x, g, w, b, *, gate, tm=512, tn=1024):
    T, D = x.shape
    N = w.shape[1]
    return pl.pallas_call(
        functools.partial(_proj_kernel, gate=gate),
        out_shape=jax.ShapeDtypeStruct((T, N), F32),
        grid=(T // tm, N // tn),
        in_specs=[
            pl.BlockSpec((tm, D), lambda i, j: (i, 0)),
            pl.BlockSpec((1, D), lambda i, j: (0, 0)),
            pl.BlockSpec((D, tn), lambda i, j: (0, j)),
            pl.BlockSpec((1, tn), lambda i, j: (0, j)),
        ],
        out_specs=pl.BlockSpec((tm, tn), lambda i, j: (i, j)),
        scratch_shapes=[pltpu.VMEM((tm, D), BF16)],
        compiler_params=_cparams(("parallel", "arbitrary")),
        name="gate_proj" if gate else "in_proj",
    )(x, g, w, b)


def _chunk_tri(n):
    r = lax.broadcasted_iota(jnp.int32, (n, n), 0)
    c = lax.broadcasted_iota(jnp.int32, (n, n), 1)
    return jnp.where((r >= c) & (r // CHUNK == c // CHUNK), 1.0, 0.0).astype(BF16)


def _gla_kernel(p_ref, wa2_ref, ba_ref, gn_ref, o_ref, st_scr, *, tg):
    @pl.when(pl.program_id(1) == 0)
    def _():
        st_scr[...] = jnp.zeros_like(st_scr)

    gad = p_ref[:, GLA_AD:GLA_AD + LANE].astype(BF16)
    z = _dot(gad, wa2_ref[...]) + ba_ref[...]
    log_alpha = -_softplus(-z) * (1.0 / GLA_TAU)
    tri = _chunk_tri(tg)
    hi = log_alpha.astype(BF16)
    lo = (log_alpha - hi.astype(F32)).astype(BF16)
    cum = _dot(tri, hi) + _dot(tri, lo)
    gn = gn_ref[...]
    scale = GLA_DK ** -0.5

    for ci in range(tg // CHUNK):
        rows = slice(ci * CHUNK, (ci + 1) * CHUNK)
        cum_c = cum[rows, :]
        tot = cum_c[CHUNK - 1:CHUNK, :]
        kdec = (p_ref[rows, GLA_QK:2 * GLA_QK] * jnp.exp(tot - cum_c)).astype(BF16)
        etot = jnp.exp(tot)
        q = (p_ref[rows, 0:GLA_QK] * scale).astype(BF16)
        for h in range(GLA_HEADS):
            ks = slice(h * GLA_DK, (h + 1) * GLA_DK)
            v_h = p_ref[rows, 2 * GLA_QK + h * GLA_DV:2 * GLA_QK + (h + 1) * GLA_DV].astype(BF16)
            r_h = p_ref[rows, 2 * GLA_QK + GLA_V + h * GLA_DV:2 * GLA_QK + GLA_V + (h + 1) * GLA_DV]
            st = st_scr[h] * etot[:, ks] + _dot_tn(v_h, kdec[:, ks])
            st_scr[h] = st
            o = _dot_nt(q[:, ks], st.astype(BF16))
            o = o * lax.rsqrt(jnp.mean(o * o, axis=-1, keepdims=True) + NORM_EPS) * gn
            o = o * (r_h * _sigmoid(r_h))
            o_ref[rows, h * GLA_DV:(h + 1) * GLA_DV] = o.astype(BF16)


def _gla(p, wa2, ba, gn, B, S, *, tg=256):
    T = B * S
    nb = S // tg
    return pl.pallas_call(
        functools.partial(_gla_kernel, tg=tg),
        out_shape=jax.ShapeDtypeStruct((T, GLA_V), BF16),
        grid=(B, nb),
        in_specs=[
            pl.BlockSpec((tg, GLA_GROUP), lambda b, i: (b * nb + i, 1)),
            pl.BlockSpec((LANE, GLA_QK), lambda b, i: (0, 0)),
            pl.BlockSpec((1, GLA_QK), lambda b, i: (0, 0)),
            pl.BlockSpec((1, GLA_DV), lambda b, i: (0, 0)),
        ],
        out_specs=pl.BlockSpec((tg, GLA_V), lambda b, i: (b * nb + i, 0)),
        scratch_shapes=[pltpu.VMEM((GLA_HEADS, GLA_DV, GLA_DK), F32)],
        compiler_params=_cparams(("parallel", "arbitrary")),
        name="gla",
    )(p, wa2, ba, gn)


def _head_ones(n):
    r = lax.broadcasted_iota(jnp.int32, (n, n), 0)
    c = lax.broadcasted_iota(jnp.int32, (n, n), 1)
    return jnp.where(r // RWKV_HD == c // RWKV_HD, 1.0, 0.0).astype(BF16)


def _head_sum(x, ones_pair):
    parts = [_dot_split(x[:, j * PAIR:(j + 1) * PAIR], ones_pair) for j in range(x.shape[1] // PAIR)]
    return jnp.concatenate(parts, axis=1)


def _rwkv_prep_kernel(p_ref, mu_ref, w0_ref, ww2_ref, a0_ref, wa2_ref, wg2_ref, kk_ref, ka_ref, rk_ref,
                      rt_ref, bt_ref, at_ref, kt_ref, at2_ref, kt2_ref, v_ref, bonus_ref, g_ref, ptot_ref,
                      carry_scr, *, tm):
    @pl.when(pl.program_id(1) == 0)
    def _():
        carry_scr[...] = jnp.zeros_like(carry_scr)

    p = p_ref[...]
    last = carry_scr[...]
    carry_scr[...] = p[tm - 1:tm, :]
    row = lax.broadcasted_iota(jnp.int32, p.shape, 0)
    prev = jnp.where(row == 0, last, pltpu.roll(p, 1, axis=0))
    p = p + mu_ref[...] * (prev - p)

    r = p[:, 0:RWKV_W]
    k = p[:, RWKV_W:2 * RWKV_W]
    v = p[:, 2 * RWKV_W:3 * RWKV_W]
    wd = p[:, RW_WD:RW_WD + LANE]
    ad = p[:, RW_AD:RW_AD + LANE]
    gd = p[:, RW_GD:RW_GD + GATE_LORA]

    w_raw = w0_ref[...] + _dot(jnp.tanh(wd).astype(BF16), ww2_ref[...])
    log_w = -jnp.exp(-_softplus(-w_raw) - 0.5)
    a = _sigmoid(a0_ref[...] + _dot(ad.astype(BF16), wa2_ref[...]))
    g_ref[...] = _dot(_sigmoid(gd).astype(BF16), wg2_ref[...])

    ones_pair = _head_ones(PAIR)
    kk = k * kk_ref[...]
    kk = kk / jnp.maximum(jnp.sqrt(_head_sum(kk * kk, ones_pair)), 1e-12)
    kp = k * (1.0 + (a - 1.0) * ka_ref[...])
    bonus_ref[...] = _head_sum(r * kp * rk_ref[...], ones_pair) * v
    v_ref[...] = v.astype(BF16)

    tri = _chunk_tri(tm)
    hi = log_w.astype(BF16)
    lo = (log_w - hi.astype(F32)).astype(BF16)
    cum = _dot(tri, hi) + _dot(tri, lo)
    nalpha = -(kk * a)
    for ci in range(tm // CHUNK):
        rows = slice(ci * CHUNK, (ci + 1) * CHUNK)
        cum_c = cum[rows, :]
        tot = cum_c[CHUNK - 1:CHUNK, :]
        e_fwd = jnp.exp(cum_c)
        e_inv = jnp.exp(-cum_c)
        e_tail = jnp.exp(tot - cum_c)
        rt_ref[rows, :] = (r[rows, :] * e_fwd).astype(BF16)
        bt_ref[rows, :] = (kk[rows, :] * jnp.exp(cum_c - log_w[rows, :])).astype(BF16)
        at_ref[rows, :] = (nalpha[rows, :] * e_inv).astype(BF16)
        kt_ref[rows, :] = (kp[rows, :] * e_inv).astype(BF16)
        at2_ref[rows, :] = (nalpha[rows, :] * e_tail).astype(BF16)
        kt2_ref[rows, :] = (kp[rows, :] * e_tail).astype(BF16)
        ptot_ref[ci] = jnp.exp(tot)


def _rwkv_prep(p, mu, w0, ww2, a0, wa2, wg2, k_k, k_a, r_k, B, S, *, tm=256):
    T = B * S
    nb = S // tm
    cpt = tm // CHUNK
    vec = lambda n: pl.BlockSpec((1, n), lambda b, i: (0, 0))
    tok = lambda: pl.BlockSpec((tm, RWKV_W), lambda b, i: (b * nb + i, 0))
    bf = jax.ShapeDtypeStruct((T, RWKV_W), BF16)
    f32 = jax.ShapeDtypeStruct((T, RWKV_W), F32)
    return pl.pallas_call(
        functools.partial(_rwkv_prep_kernel, tm=tm),
        out_shape=[bf, bf, bf, bf, bf, bf, bf, f32, f32,
                   jax.ShapeDtypeStruct((T // CHUNK, 1, RWKV_W), F32)],
        grid=(B, nb),
        in_specs=[
            pl.BlockSpec((tm, RW_GROUP), lambda b, i: (b * nb + i, 0)),
            vec(RW_GROUP), vec(RWKV_W),
            pl.BlockSpec((LANE, RWKV_W), lambda b, i: (0, 0)),
            vec(RWKV_W),
            pl.BlockSpec((LANE, RWKV_W), lambda b, i: (0, 0)),
            pl.BlockSpec((GATE_LORA, RWKV_W), lambda b, i: (0, 0)),
            vec(RWKV_W), vec(RWKV_W), vec(RWKV_W),
        ],
        out_specs=[tok() for _ in range(9)]
        + [pl.BlockSpec((cpt, 1, RWKV_W), lambda b, i: (b * nb + i, 0, 0))],
        scratch_shapes=[pltpu.VMEM((1, RW_GROUP), F32)],
        compiler_params=_cparams(("parallel", "arbitrary")),
        name="rwkv_prep",
    )(p, mu, w0, ww2, a0, wa2, wg2, k_k, k_a, r_k)


def _rwkv_core_kernel(rt_ref, bt_ref, at_ref, kt_ref, at2_ref, kt2_ref, v_ref, bonus_ref, g_ref, ptot_ref,
                      lw_ref, lb_ref, o_ref, h_scr):
    @pl.when(pl.program_id(1) == 0)
    def _():
        h_scr[...] = jnp.zeros_like(h_scr)

    ri = lax.broadcasted_iota(jnp.int32, (PAIR, PAIR), 0)
    ci = lax.broadcasted_iota(jnp.int32, (PAIR, PAIR), 1)
    head_blk = (ri // RWKV_HD) == (ci // RWKV_HD)
    strict = ri > ci
    lower = ri >= ci
    eye = ri == ci
    blk8 = (ri // 8) == (ci // 8)
    eye_f = jnp.where(eye, 1.0, 0.0)
    ones_pair = jnp.where(head_blk, 1.0, 0.0).astype(BF16)
    zeros_b = jnp.zeros((PAIR, PAIR), BF16)

    def stack(ref, lanes):
        x = ref[:, lanes]
        return jnp.where(head_blk, jnp.concatenate([x, x], axis=0), jnp.zeros((), x.dtype))

    pairs = range(N_PAIR)
    lanes = [slice(j * PAIR, (j + 1) * PAIR) for j in pairs]
    bx = [stack(bt_ref, l) for l in lanes]
    rx = [stack(rt_ref, l) for l in lanes]
    ak = [jnp.concatenate([stack(at_ref, l), stack(kt_ref, l)], axis=0) for l in lanes]
    ak2 = [jnp.concatenate([stack(at2_ref, l), stack(kt2_ref, l)], axis=0) for l in lanes]
    vx = [stack(v_ref, l) for l in lanes]

    gb = [_dot_nt(bx[j], ak[j]) for j in pairs]
    gr = [_dot_nt(rx[j], ak[j]) for j in pairs]
    a_ab = [jnp.where(strict, x[:, :PAIR], 0.0) for x in gb]
    a_kb = [jnp.where(strict, x[:, PAIR:], 0.0).astype(BF16) for x in gb]
    a_r = [jnp.concatenate([jnp.where(lower, x[:, :PAIR], 0.0),
                            jnp.where(lower, x[:, PAIR:], 0.0)], axis=1).astype(BF16) for x in gr]

    a_d = [jnp.where(blk8, x, 0.0) for x in a_ab]
    a_db = [x.astype(BF16) for x in a_d]
    pw = [_dot(x, x).astype(BF16) for x in a_db]
    s = [eye_f + x for x in a_d]
    sp = [_dot(pw[j], jnp.concatenate([s[j].astype(BF16), pw[j]], axis=1)) for j in pairs]
    s = [s[j] + sp[j][:, :PAIR] for j in pairs]
    pw = [x[:, PAIR:].astype(BF16) for x in sp]
    s = [s[j] + _dot(pw[j], s[j].astype(BF16)) for j in pairs]
    for width in (8, 16, 32):
        off = ((ri // (2 * width)) == (ci // (2 * width))) & ((ri // width) != (ci // width))
        e = [jnp.where(off, x, 0.0).astype(BF16) for x in a_ab]
        sb = [x.astype(BF16) for x in s]
        es = [_dot(e[j], sb[j]).astype(BF16) for j in pairs]
        s = [s[j] + _dot(sb[j], es[j]) for j in pairs]
    t_inv = [x.astype(BF16) for x in s]

    kv = [_dot(a_kb[j], vx[j]).astype(BF16) for j in pairs]
    wu = [_dot(t_inv[j], jnp.concatenate([bx[j], kv[j]], axis=1)).astype(BF16) for j in pairs]
    z = [jnp.concatenate([wu[j], jnp.concatenate([zeros_b, vx[j]], axis=1)], axis=0) for j in pairs]
    mc = [_dot_tn(ak2[j], z[j]) for j in pairs]
    qy = [_dot(a_r[j], z[j]) for j in pairs]

    m = [(mc[j][:, :PAIR] + jnp.where(eye, ptot_ref[0, :, lanes[j]], 0.0)).astype(BF16) for j in pairs]
    q = [(qy[j][:, :PAIR] + rx[j].astype(F32)).astype(BF16) for j in pairs]
    hb = [h_scr[j].astype(BF16) for j in pairs]
    y = [_dot(q[j], hb[j]) + qy[j][:, PAIR:] for j in pairs]
    hn = [_dot(m[j], hb[j]) + mc[j][:, PAIR:] for j in pairs]
    for j in pairs:
        h_scr[j] = hn[j]
    y = [x[:RWKV_HD, :] + x[RWKV_HD:, :] for x in y]

    mean = [_dot_split(x, ones_pair) * (1.0 / RWKV_HD) for x in y]
    yc = [y[j] - mean[j] for j in pairs]
    var = [_dot_split(x * x, ones_pair) * (1.0 / RWKV_HD) for x in yc]
    for j in pairs:
        l = lanes[j]
        yn = yc[j] * lax.rsqrt(var[j] + GN_EPS) * lw_ref[:, l] + lb_ref[:, l]
        o_ref[:, l] = ((yn + bonus_ref[:, l]) * g_ref[:, l]).astype(BF16)


def _rwkv_core(rt, bt, at, kt, at2, kt2, v, bonus, g, ptot, lnx_w, lnx_b, B, S):
    T = B * S
    nc = S // CHUNK
    tok = lambda: pl.BlockSpec((CHUNK, RWKV_W), lambda b, c: (b * nc + c, 0))
    vec = lambda: pl.BlockSpec((1, RWKV_W), lambda b, c: (0, 0))
    return pl.pallas_call(
        _rwkv_core_kernel,
        out_shape=jax.ShapeDtypeStruct((T, RWKV_W), BF16),
        grid=(B, nc),
        in_specs=[tok() for _ in range(9)]
        + [pl.BlockSpec((1, 1, RWKV_W), lambda b, c: (b * nc + c, 0, 0)), vec(), vec()],
        out_specs=tok(),
        scratch_shapes=[pltpu.VMEM((N_PAIR, PAIR, PAIR), F32)],
        compiler_params=_cparams(("parallel", "arbitrary")),
        name="rwkv_core",
    )(rt, bt, at, kt, at2, kt2, v, bonus, g, ptot, lnx_w, lnx_b)


def _merge_kernel(x_ref, og_ref, or_ref, gt_ref, wb1_ref, wb2_ref, wo_ref, o_ref):
    D = x_ref.shape[1]
    y_gla = _dot(og_ref[...], wb1_ref[...])
    y_rw = _dot(or_ref[...], wb2_ref[...])
    merged = gt_ref[:, :D] * y_gla + gt_ref[:, D:] * y_rw
    o_ref[...] = x_ref[...] + _dot(merged.astype(BF16), wo_ref[...])


def _merge(x, o_gla, o_rw, gates, wb1, wb2, wo, *, tm=256):
    T, D = x.shape
    const = lambda shape: pl.BlockSpec(shape, lambda i: (0, 0), pipeline_mode=pl.Buffered(1))
    return pl.pallas_call(
        _merge_kernel,
        out_shape=jax.ShapeDtypeStruct((T, D), F32),
        grid=(T // tm,),
        in_specs=[
            pl.BlockSpec((tm, D), lambda i: (i, 0)),
            pl.BlockSpec((tm, GLA_V), lambda i: (i, 0)),
            pl.BlockSpec((tm, RWKV_W), lambda i: (i, 0)),
            pl.BlockSpec((tm, 2 * D), lambda i: (i, 0)),
            const((GLA_V, D)), const((RWKV_W, D)), const((D, D)),
        ],
        out_specs=pl.BlockSpec((tm, D), lambda i: (i, 0)),
        compiler_params=_cparams(("parallel",)),
        name="merge_out",
    )(x, o_gla, o_rw, gates, wb1, wb2, wo)


def _pad_cols(w, n):
    return jnp.pad(w, ((0, 0), (0, n - w.shape[1])))


def _pad_rows(w, n):
    return jnp.pad(w, ((0, n - w.shape[0]), (0, 0)))


def _pack_in_proj(w_in, rwkv_mu):
    gla_in = 2 * GLA_QK + 2 * GLA_V + GLA_LORA
    rw_in = 3 * RWKV_W + DECAY_LORA + AAA_LORA + GATE_LORA
    w_gla = w_in[:, :gla_in]
    w_rw = w_in[:, gla_in:gla_in + rw_in]
    w_gate = w_in[:, gla_in + rw_in:]

    def rw_pack(m):
        main = m[:, :3 * RWKV_W]
        wd = _pad_cols(m[:, 3 * RWKV_W:3 * RWKV_W + DECAY_LORA], LANE)
        ad = _pad_cols(m[:, 3 * RWKV_W + DECAY_LORA:3 * RWKV_W + DECAY_LORA + AAA_LORA], LANE)
        gd = m[:, 3 * RWKV_W + DECAY_LORA + AAA_LORA:]
        return jnp.concatenate([main, wd, ad, gd], axis=1)

    w_a = jnp.concatenate([rw_pack(w_rw), _pad_cols(w_gla, GLA_GROUP)], axis=1)
    return w_a.astype(BF16), w_gate.astype(BF16), rw_pack(rwkv_mu[None, :])


def kernel(x, ffn1_norm, ffn1_wg, ffn1_wu, ffn1_wd, mix_norm, w_in, gla_w_a2, gla_b_a, gla_gn_w, rwkv_mu,
           rwkv_w0, rwkv_w_w2, rwkv_a0, rwkv_w_a2, rwkv_w_g2, rwkv_k_k, rwkv_k_a, rwkv_r_k, rwkv_lnx_w,
           rwkv_lnx_b, gate_b, w_branch, w_out, ffn2_norm, ffn2_wg, ffn2_wu, ffn2_wd, final_norm):
    B, S, D = x.shape
    T = B * S
    depth = ffn1_norm.shape[0]
    bf = lambda w: w.astype(BF16)
    row = lambda v: v.reshape(1, -1)
    xt = x.reshape(T, D)
    for l in range(depth):
        last = l == depth - 1
        xt = _ffn(xt, row(ffn1_norm[l]), bf(ffn1_wg[l]), bf(ffn1_wu[l]), bf(ffn1_wd[l]), row(final_norm),
                  final_norm=False)

        w_a, w_gate, mu = _pack_in_proj(w_in[l], rwkv_mu[l])
        p = _norm_proj(xt, row(mix_norm[l]), w_a, jnp.zeros((1, w_a.shape[1]), F32), gate=False)
        gates = _norm_proj(xt, row(mix_norm[l]), w_gate, row(gate_b[l]), gate=True)

        o_gla = _gla(p, bf(_pad_rows(gla_w_a2[l], LANE)), row(gla_b_a[l]), row(gla_gn_w[l]), B, S)

        prep = _rwkv_prep(p, mu, row(rwkv_w0[l]), bf(_pad_rows(rwkv_w_w2[l], LANE)), row(rwkv_a0[l]),
                          bf(_pad_rows(rwkv_w_a2[l], LANE)), bf(rwkv_w_g2[l]), row(rwkv_k_k[l]),
                          row(rwkv_k_a[l]), row(rwkv_r_k[l]), B, S)
        o_rw = _rwkv_core(*prep, row(rwkv_lnx_w[l]), row(rwkv_lnx_b[l]), B, S)

        xt = _merge(xt, o_gla, o_rw, gates, bf(w_branch[l, :GLA_V]), bf(w_branch[l, GLA_V:]), bf(w_out[l]))

        xt = _ffn(xt, row(ffn2_norm[l]), bf(ffn2_wg[l]), bf(ffn2_wu[l]), bf(ffn2_wd[l]), row(final_norm),
                  final_norm=last)
    if depth == 0:
        raise ValueError("depth must be >= 1")
    return xt.reshape(B, S, D)
```

```python
import functools

import jax
import jax.numpy as jnp
from jax import lax
from jax.experimental import pallas as pl
from jax.experimental.pallas import tpu as pltpu

F32 = jnp.float32
BF16 = jnp.bfloat16

NORM_EPS = 1e-6
GN_EPS = 64e-5
GLA_TAU = 16.0
CHUNK = 64

GLA_HEADS = 4
GLA_DK = 128
GLA_DV = 256
GLA_QK = GLA_HEADS * GLA_DK
GLA_V = GLA_HEADS * GLA_DV
GLA_LORA = 16

RWKV_HD = 64
RWKV_W = 1024
DECAY_LORA = 96
AAA_LORA = 96
GATE_LORA = 256

LANE = 128
PAIR = 2 * RWKV_HD
N_PAIR = RWKV_W // PAIR

RW_GROUP = 3 * RWKV_W + LANE + LANE + GATE_LORA
GLA_GROUP = RW_GROUP
RW_WD = 3 * RWKV_W
GLA_AD = 2 * GLA_QK + 2 * GLA_V
RW_AD = RW_WD + LANE
RW_GD = RW_AD + LANE

VMEM_LIMIT = 56 * 1024 * 1024


def _cparams(sem):
    return pltpu.CompilerParams(dimension_semantics=sem, vmem_limit_bytes=VMEM_LIMIT)


def _dot(a, b):
    return jnp.dot(a, b, preferred_element_type=F32)


def _dot_nt(a, b):
    return lax.dot_general(a, b, (((1,), (1,)), ((), ())), preferred_element_type=F32)


def _dot_tn(a, b):
    return lax.dot_general(a, b, (((0,), (0,)), ((), ())), preferred_element_type=F32)


def _dot_split(a, b_bf16):
    hi = a.astype(BF16)
    lo = (a - hi.astype(F32)).astype(BF16)
    return _dot(hi, b_bf16) + _dot(lo, b_bf16)


def _rmsnorm(x, g):
    return x * lax.rsqrt(jnp.mean(x * x, axis=-1, keepdims=True) + NORM_EPS) * g


def _sigmoid(z):
    return 1.0 / (1.0 + jnp.exp(-z))


def _softplus(z):
    return jnp.maximum(z, 0.0) + jnp.log(1.0 + jnp.exp(-jnp.abs(z)))


def _ffn_kernel(x_ref, g_ref, wg_ref, wu_ref, wd_ref, fn_ref, o_ref, h_scr, *, final_norm):
    j = pl.program_id(1)

    @pl.when(j == 0)
    def _():
        h_scr[...] = _rmsnorm(x_ref[...], g_ref[...]).astype(BF16)
        o_ref[...] = jnp.zeros_like(o_ref)

    h = h_scr[...]
    a = _dot(h, wg_ref[...])
    u = _dot(h, wu_ref[...])
    act = (a * _sigmoid(a) * u).astype(BF16)
    o_ref[...] += _dot(act, wd_ref[...])

    @pl.when(j == pl.num_programs(1) - 1)
    def _():
        y = x_ref[...] + 0.5 * o_ref[...]
        if final_norm:
            y = _rmsnorm(y, fn_ref[...])
        o_ref[...] = y


def _ffn(x, g, wg, wu, wd, fn, *, final_norm, tm=1024, tf=512):
    T, D = x.shape
    FF = wg.shape[1]
    return pl.pallas_call(
        functools.partial(_ffn_kernel, final_norm=final_norm),
        out_shape=jax.ShapeDtypeStruct((T, D), F32),
        grid=(T // tm, FF // tf),
        in_specs=[
            pl.BlockSpec((tm, D), lambda i, j: (i, 0), pipeline_mode=pl.Buffered(1)),
            pl.BlockSpec((1, D), lambda i, j: (0, 0)),
            pl.BlockSpec((D, tf), lambda i, j: (0, j)),
            pl.BlockSpec((D, tf), lambda i, j: (0, j)),
            pl.BlockSpec((tf, D), lambda i, j: (j, 0)),
            pl.BlockSpec((1, D), lambda i, j: (0, 0)),
        ],
        out_specs=pl.BlockSpec((tm, D), lambda i, j: (i, 0)),
        scratch_shapes=[pltpu.VMEM((tm, D), BF16)],
        compiler_params=_cparams(("parallel", "arbitrary")),
        name="ffn_final" if final_norm else "ffn",
    )(x, g, wg, wu, wd, fn)


def _proj_kernel(x_ref, g_ref, w_ref, b_ref, o_ref, h_scr, *, gate):
    @pl.when(pl.program_id(1) == 0)
    def _():
        h_scr[...] = _rmsnorm(x_ref[...], g_ref[...]).astype(BF16)

    y = _dot(h_scr[...], w_ref[...])
    if gate:
        y = _sigmoid(y + b_ref[...])
    o_ref[...] = y


def _norm_proj(x, g, w, b, *, gate, tm=1024, tn=1024):
    T, D = x.shape
    N = w.shape[1]
    return pl.pallas_call(
        functools.partial(_proj_kernel, gate=gate),
        out_shape=jax.ShapeDtypeStruct((T, N), F32),
        grid=(T // tm, N // tn),
        in_specs=[
            pl.BlockSpec((tm, D), lambda i, j: (i, 0)),
            pl.BlockSpec((1, D), lambda i, j: (0, 0)),
            pl.BlockSpec((D, tn), lambda i, j: (0, j)),
            pl.BlockSpec((1, tn), lambda i, j: (0, j)),
        ],
        out_specs=pl.BlockSpec((tm, tn), lambda i, j: (i, j)),
        scratch_shapes=[pltpu.VMEM((tm, D), BF16)],
        compiler_params=_cparams(("parallel", "arbitrary")),
        name="gate_proj" if gate else "in_proj",
    )(x, g, w, b)


def _chunk_tri(n):
    r = lax.broadcasted_iota(jnp.int32, (n, n), 0)
    c = lax.broadcasted_iota(jnp.int32, (n, n), 1)
    return jnp.where((r >= c) & (r // CHUNK == c // CHUNK), 1.0, 0.0).astype(BF16)


def _gla_kernel(p_ref, wa2_ref, ba_ref, gn_ref, o_ref, st_scr, *, tg):
    @pl.when(pl.program_id(1) == 0)
    def _():
        st_scr[...] = jnp.zeros_like(st_scr)

    gad = p_ref[:, GLA_AD:GLA_AD + LANE].astype(BF16)
    z = _dot(gad, wa2_ref[...]) + ba_ref[...]
    log_alpha = -_softplus(-z) * (1.0 / GLA_TAU)
    tri = _chunk_tri(tg)
    hi = log_alpha.astype(BF16)
    lo = (log_alpha - hi.astype(F32)).astype(BF16)
    cum = _dot(tri, hi) + _dot(tri, lo)
    gn = gn_ref[...]
    scale = GLA_DK ** -0.5

    n_chunks = tg // CHUNK
    rows = [slice(c * CHUNK, (c + 1) * CHUNK) for c in range(n_chunks)]
    keys = [slice(h * GLA_DK, (h + 1) * GLA_DK) for h in range(GLA_HEADS)]
    vals = [slice(h * GLA_DV, (h + 1) * GLA_DV) for h in range(GLA_HEADS)]
    units = [(c, h) for c in range(n_chunks) for h in range(GLA_HEADS)]
    tot = [cum[r, :][CHUNK - 1:CHUNK, :] for r in rows]
    etot = [jnp.exp(t) for t in tot]
    kdec = [(p_ref[rows[c], GLA_QK:2 * GLA_QK] * jnp.exp(tot[c] - cum[rows[c], :])).astype(BF16)
            for c in range(n_chunks)]
    q = [(p_ref[r, 0:GLA_QK] * scale).astype(BF16) for r in rows]
    inc = {}
    for c, h in units:
        v_h = p_ref[rows[c], 2 * GLA_QK + h * GLA_DV:2 * GLA_QK + (h + 1) * GLA_DV].astype(BF16)
        inc[c, h] = _dot_tn(v_h, kdec[c][:, keys[h]])
    st = {}
    for h in range(GLA_HEADS):
        prev = st_scr[h]
        for c in range(n_chunks):
            prev = prev * etot[c][:, keys[h]] + inc[c, h]
            st[c, h] = prev
        st_scr[h] = prev
    out = {u: _dot_nt(q[u[0]][:, keys[u[1]]], st[u].astype(BF16)) for u in units}
    for c, h in units:
        o = out[c, h]
        r_h = p_ref[rows[c], 2 * GLA_QK + GLA_V + h * GLA_DV:2 * GLA_QK + GLA_V + (h + 1) * GLA_DV]
        o = o * lax.rsqrt(jnp.mean(o * o, axis=-1, keepdims=True) + NORM_EPS) * gn
        o = o * (r_h * _sigmoid(r_h))
        o_ref[rows[c], vals[h]] = o.astype(BF16)


def _gla(p, wa2, ba, gn, B, S, *, tg=256):
    T = B * S
    nb = S // tg
    return pl.pallas_call(
        functools.partial(_gla_kernel, tg=tg),
        out_shape=jax.ShapeDtypeStruct((T, GLA_V), BF16),
        grid=(B, nb),
        in_specs=[
            pl.BlockSpec((tg, GLA_GROUP), lambda b, i: (b * nb + i, 1)),
            pl.BlockSpec((LANE, GLA_QK), lambda b, i: (0, 0)),
            pl.BlockSpec((1, GLA_QK), lambda b, i: (0, 0)),
            pl.BlockSpec((1, GLA_DV), lambda b, i: (0, 0)),
        ],
        out_specs=pl.BlockSpec((tg, GLA_V), lambda b, i: (b * nb + i, 0)),
        scratch_shapes=[pltpu.VMEM((GLA_HEADS, GLA_DV, GLA_DK), F32)],
        compiler_params=_cparams(("parallel", "arbitrary")),
        name="gla",
    )(p, wa2, ba, gn)


def _head_ones(n):
    r = lax.broadcasted_iota(jnp.int32, (n, n), 0)
    c = lax.broadcasted_iota(jnp.int32, (n, n), 1)
    return jnp.where(r // RWKV_HD == c // RWKV_HD, 1.0, 0.0).astype(BF16)


def _head_sum(x, ones_pair):
    parts = [_dot_split(x[:, j * PAIR:(j + 1) * PAIR], ones_pair) for j in range(x.shape[1] // PAIR)]
    return jnp.concatenate(parts, axis=1)


def _rwkv_prep_kernel(p_ref, mu_ref, w0_ref, ww2_ref, a0_ref, wa2_ref, wg2_ref, kk_ref, ka_ref, rk_ref,
                      rt_ref, bt_ref, at_ref, kt_ref, at2_ref, kt2_ref, v_ref, bonus_ref, g_ref, ptot_ref,
                      carry_scr, *, tm):
    @pl.when(pl.program_id(1) == 0)
    def _():
        carry_scr[...] = jnp.zeros_like(carry_scr)

    p = p_ref[...]
    last = carry_scr[...]
    carry_scr[...] = p[tm - 1:tm, :]
    row = lax.broadcasted_iota(jnp.int32, p.shape, 0)
    prev = jnp.where(row == 0, last, pltpu.roll(p, 1, axis=0))
    p = p + mu_ref[...] * (prev - p)

    r = p[:, 0:RWKV_W]
    k = p[:, RWKV_W:2 * RWKV_W]
    v = p[:, 2 * RWKV_W:3 * RWKV_W]
    wd = p[:, RW_WD:RW_WD + LANE]
    ad = p[:, RW_AD:RW_AD + LANE]
    gd = p[:, RW_GD:RW_GD + GATE_LORA]

    w_raw = w0_ref[...] + _dot(jnp.tanh(wd).astype(BF16), ww2_ref[...])
    log_w = -jnp.exp(-_softplus(-w_raw) - 0.5)
    a = _sigmoid(a0_ref[...] + _dot(ad.astype(BF16), wa2_ref[...]))
    g_ref[...] = _dot(_sigmoid(gd).astype(BF16), wg2_ref[...])

    ones_pair = _head_ones(PAIR)
    kk = k * kk_ref[...]
    kk = kk / jnp.maximum(jnp.sqrt(_head_sum(kk * kk, ones_pair)), 1e-12)
    kp = k * (1.0 + (a - 1.0) * ka_ref[...])
    bonus_ref[...] = _head_sum(r * kp * rk_ref[...], ones_pair) * v
    v_ref[...] = v.astype(BF16)

    tri = _chunk_tri(tm)
    hi = log_w.astype(BF16)
    lo = (log_w - hi.astype(F32)).astype(BF16)
    cum = _dot(tri, hi) + _dot(tri, lo)
    nalpha = -(kk * a)
    for ci in range(tm // CHUNK):
        rows = slice(ci * CHUNK, (ci + 1) * CHUNK)
        cum_c = cum[rows, :]
        tot = cum_c[CHUNK - 1:CHUNK, :]
        e_fwd = jnp.exp(cum_c)
        e_inv = jnp.exp(-cum_c)
        e_tail = jnp.exp(tot - cum_c)
        rt_ref[rows, :] = (r[rows, :] * e_fwd).astype(BF16)
        bt_ref[rows, :] = (kk[rows, :] * jnp.exp(cum_c - log_w[rows, :])).astype(BF16)
        at_ref[rows, :] = (nalpha[rows, :] * e_inv).astype(BF16)
        kt_ref[rows, :] = (kp[rows, :] * e_inv).astype(BF16)
        at2_ref[rows, :] = (nalpha[rows, :] * e_tail).astype(BF16)
        kt2_ref[rows, :] = (kp[rows, :] * e_tail).astype(BF16)
        ptot_ref[ci] = jnp.exp(tot)


def _rwkv_prep(p, mu, w0, ww2, a0, wa2, wg2, k_k, k_a, r_k, B, S, *, tm=256):
    T = B * S
    nb = S // tm
    cpt = tm // CHUNK
    vec = lambda n: pl.BlockSpec((1, n), lambda b, i: (0, 0))
    tok = lambda: pl.BlockSpec((tm, RWKV_W), lambda b, i: (b * nb + i, 0))
    bf = jax.ShapeDtypeStruct((T, RWKV_W), BF16)
    f32 = jax.ShapeDtypeStruct((T, RWKV_W), F32)
    return pl.pallas_call(
        functools.partial(_rwkv_prep_kernel, tm=tm),
        out_shape=[bf, bf, bf, bf, bf, bf, bf, f32, f32,
                   jax.ShapeDtypeStruct((T // CHUNK, 1, RWKV_W), F32)],
        grid=(B, nb),
        in_specs=[
            pl.BlockSpec((tm, RW_GROUP), lambda b, i: (b * nb + i, 0)),
            vec(RW_GROUP), vec(RWKV_W),
            pl.BlockSpec((LANE, RWKV_W), lambda b, i: (0, 0)),
            vec(RWKV_W),
            pl.BlockSpec((LANE, RWKV_W), lambda b, i: (0, 0)),
            pl.BlockSpec((GATE_LORA, RWKV_W), lambda b, i: (0, 0)),
            vec(RWKV_W), vec(RWKV_W), vec(RWKV_W),
        ],
        out_specs=[tok() for _ in range(9)]
        + [pl.BlockSpec((cpt, 1, RWKV_W), lambda b, i: (b * nb + i, 0, 0))],
        scratch_shapes=[pltpu.VMEM((1, RW_GROUP), F32)],
        compiler_params=_cparams(("parallel", "arbitrary")),
        name="rwkv_prep",
    )(p, mu, w0, ww2, a0, wa2, wg2, k_k, k_a, r_k)


def _rwkv_core_kernel(rt_ref, bt_ref, at_ref, kt_ref, at2_ref, kt2_ref, v_ref, bonus_ref, g_ref, ptot_ref,
                      lw_ref, lb_ref, o_ref, h_scr):
    @pl.when(pl.program_id(1) == 0)
    def _():
        h_scr[...] = jnp.zeros_like(h_scr)

    ri = lax.broadcasted_iota(jnp.int32, (PAIR, PAIR), 0)
    ci = lax.broadcasted_iota(jnp.int32, (PAIR, PAIR), 1)
    head_blk = (ri // RWKV_HD) == (ci // RWKV_HD)
    strict = ri > ci
    lower = ri >= ci
    eye = ri == ci
    blk8 = (ri // 8) == (ci // 8)
    eye_f = jnp.where(eye, 1.0, 0.0)
    ones_pair = jnp.where(head_blk, 1.0, 0.0).astype(BF16)
    zeros_b = jnp.zeros((PAIR, PAIR), BF16)

    def stack(ref, lanes):
        x = ref[:, lanes]
        return jnp.where(head_blk, jnp.concatenate([x, x], axis=0), jnp.zeros((), x.dtype))

    pairs = range(N_PAIR)
    lanes = [slice(j * PAIR, (j + 1) * PAIR) for j in pairs]
    bx = [stack(bt_ref, l) for l in lanes]
    rx = [stack(rt_ref, l) for l in lanes]
    ak = [jnp.concatenate([stack(at_ref, l), stack(kt_ref, l)], axis=0) for l in lanes]
    ak2 = [jnp.concatenate([stack(at2_ref, l), stack(kt2_ref, l)], axis=0) for l in lanes]
    vx = [stack(v_ref, l) for l in lanes]

    gb = [_dot_nt(bx[j], ak[j]) for j in pairs]
    gr = [_dot_nt(rx[j], ak[j]) for j in pairs]
    a_ab = [jnp.where(strict, x[:, :PAIR], 0.0) for x in gb]
    a_kb = [jnp.where(strict, x[:, PAIR:], 0.0).astype(BF16) for x in gb]
    a_r = [jnp.concatenate([jnp.where(lower, x[:, :PAIR], 0.0),
                            jnp.where(lower, x[:, PAIR:], 0.0)], axis=1).astype(BF16) for x in gr]

    a_d = [jnp.where(blk8, x, 0.0) for x in a_ab]
    a_db = [x.astype(BF16) for x in a_d]
    pw = [_dot(x, x).astype(BF16) for x in a_db]
    s = [eye_f + x for x in a_d]
    sp = [_dot(pw[j], jnp.concatenate([s[j].astype(BF16), pw[j]], axis=1)) for j in pairs]
    s = [s[j] + sp[j][:, :PAIR] for j in pairs]
    pw = [x[:, PAIR:].astype(BF16) for x in sp]
    s = [s[j] + _dot(pw[j], s[j].astype(BF16)) for j in pairs]
    for width in (8, 16, 32):
        off = ((ri // (2 * width)) == (ci // (2 * width))) & ((ri // width) != (ci // width))
        e = [jnp.where(off, x, 0.0).astype(BF16) for x in a_ab]
        sb = [x.astype(BF16) for x in s]
        es = [_dot(e[j], sb[j]).astype(BF16) for j in pairs]
        s = [s[j] + _dot(sb[j], es[j]) for j in pairs]
    t_inv = [x.astype(BF16) for x in s]

    kv = [_dot(a_kb[j], vx[j]).astype(BF16) for j in pairs]
    wu = [_dot(t_inv[j], jnp.concatenate([bx[j], kv[j]], axis=1)).astype(BF16) for j in pairs]
    z = [jnp.concatenate([wu[j], jnp.concatenate([zeros_b, vx[j]], axis=1)], axis=0) for j in pairs]
    mc = [_dot_tn(ak2[j], z[j]) for j in pairs]
    qy = [_dot(a_r[j], z[j]) for j in pairs]

    m = [(mc[j][:, :PAIR] + jnp.where(eye, ptot_ref[0, :, lanes[j]], 0.0)).astype(BF16) for j in pairs]
    q = [(qy[j][:, :PAIR] + rx[j].astype(F32)).astype(BF16) for j in pairs]
    hb = [h_scr[j].astype(BF16) for j in pairs]
    y = [_dot(q[j], hb[j]) + qy[j][:, PAIR:] for j in pairs]
    hn = [_dot(m[j], hb[j]) + mc[j][:, PAIR:] for j in pairs]
    for j in pairs:
        h_scr[j] = hn[j]
    y = [x[:RWKV_HD, :] + x[RWKV_HD:, :] for x in y]

    mean = [_dot_split(x, ones_pair) * (1.0 / RWKV_HD) for x in y]
    yc = [y[j] - mean[j] for j in pairs]
    var = [_dot_split(x * x, ones_pair) * (1.0 / RWKV_HD) for x in yc]
    for j in pairs:
        l = lanes[j]
        yn = yc[j] * lax.rsqrt(var[j] + GN_EPS) * lw_ref[:, l] + lb_ref[:, l]
        o_ref[:, l] = ((yn + bonus_ref[:, l]) * g_ref[:, l]).astype(BF16)


def _rwkv_core(rt, bt, at, kt, at2, kt2, v, bonus, g, ptot, lnx_w, lnx_b, B, S):
    T = B * S
    nc = S // CHUNK
    tok = lambda: pl.BlockSpec((CHUNK, RWKV_W), lambda b, c: (b * nc + c, 0))
    vec = lambda: pl.BlockSpec((1, RWKV_W), lambda b, c: (0, 0))
    return pl.pallas_call(
        _rwkv_core_kernel,
        out_shape=jax.ShapeDtypeStruct((T, RWKV_W), BF16),
        grid=(B, nc),
        in_specs=[tok() for _ in range(9)]
        + [pl.BlockSpec((1, 1, RWKV_W), lambda b, c: (b * nc + c, 0, 0)), vec(), vec()],
        out_specs=tok(),
        scratch_shapes=[pltpu.VMEM((N_PAIR, PAIR, PAIR), F32)],
        compiler_params=_cparams(("parallel", "arbitrary")),
        name="rwkv_core",
    )(rt, bt, at, kt, at2, kt2, v, bonus, g, ptot, lnx_w, lnx_b)


def _merge_kernel(x_ref, og_ref, or_ref, gt_ref, wb1_ref, wb2_ref, wo_ref, o_ref):
    D = x_ref.shape[1]
    y_gla = _dot(og_ref[...], wb1_ref[...])
    y_rw = _dot(or_ref[...], wb2_ref[...])
    merged = gt_ref[:, :D] * y_gla + gt_ref[:, D:] * y_rw
    o_ref[...] = x_ref[...] + _dot(merged.astype(BF16), wo_ref[...])


def _merge(x, o_gla, o_rw, gates, wb1, wb2, wo, *, tm=256):
    T, D = x.shape
    const = lambda shape: pl.BlockSpec(shape, lambda i: (0, 0), pipeline_mode=pl.Buffered(1))
    return pl.pallas_call(
        _merge_kernel,
        out_shape=jax.ShapeDtypeStruct((T, D), F32),
        grid=(T // tm,),
        in_specs=[
            pl.BlockSpec((tm, D), lambda i: (i, 0)),
            pl.BlockSpec((tm, GLA_V), lambda i: (i, 0)),
            pl.BlockSpec((tm, RWKV_W), lambda i: (i, 0)),
            pl.BlockSpec((tm, 2 * D), lambda i: (i, 0)),
            const((GLA_V, D)), const((RWKV_W, D)), const((D, D)),
        ],
        out_specs=pl.BlockSpec((tm, D), lambda i: (i, 0)),
        compiler_params=_cparams(("parallel",)),
        name="merge_out",
    )(x, o_gla, o_rw, gates, wb1, wb2, wo)


def _pad_cols(w, n):
    return jnp.pad(w, ((0, 0), (0, n - w.shape[1])))


def _pad_rows(w, n):
    return jnp.pad(w, ((0, n - w.shape[0]), (0, 0)))


def _pack_in_proj(w_in, rwkv_mu):
    gla_in = 2 * GLA_QK + 2 * GLA_V + GLA_LORA
    rw_in = 3 * RWKV_W + DECAY_LORA + AAA_LORA + GATE_LORA
    w_gla = w_in[:, :gla_in]
    w_rw = w_in[:, gla_in:gla_in + rw_in]
    w_gate = w_in[:, gla_in + rw_in:]

    def rw_pack(m):
        main = m[:, :3 * RWKV_W]
        wd = _pad_cols(m[:, 3 * RWKV_W:3 * RWKV_W + DECAY_LORA], LANE)
        ad = _pad_cols(m[:, 3 * RWKV_W + DECAY_LORA:3 * RWKV_W + DECAY_LORA + AAA_LORA], LANE)
        gd = m[:, 3 * RWKV_W + DECAY_LORA + AAA_LORA:]
        return jnp.concatenate([main, wd, ad, gd], axis=1)

    w_a = jnp.concatenate([rw_pack(w_rw), _pad_cols(w_gla, GLA_GROUP)], axis=1)
    return w_a.astype(BF16), w_gate.astype(BF16), rw_pack(rwkv_mu[None, :])


def kernel(x, ffn1_norm, ffn1_wg, ffn1_wu, ffn1_wd, mix_norm, w_in, gla_w_a2, gla_b_a, gla_gn_w, rwkv_mu,
           rwkv_w0, rwkv_w_w2, rwkv_a0, rwkv_w_a2, rwkv_w_g2, rwkv_k_k, rwkv_k_a, rwkv_r_k, rwkv_lnx_w,
           rwkv_lnx_b, gate_b, w_branch, w_out, ffn2_norm, ffn2_wg, ffn2_wu, ffn2_wd, final_norm):
    B, S, D = x.shape
    T = B * S
    depth = ffn1_norm.shape[0]
    bf = lambda w: w.astype(BF16)
    row = lambda v: v.reshape(1, -1)
    xt = x.reshape(T, D)
    for l in range(depth):
        last = l == depth - 1
        xt = _ffn(xt, row(ffn1_norm[l]), bf(ffn1_wg[l]), bf(ffn1_wu[l]), bf(ffn1_wd[l]), row(final_norm),
                  final_norm=False)

        w_a, w_gate, mu = _pack_in_proj(w_in[l], rwkv_mu[l])
        p = _norm_proj(xt, row(mix_norm[l]), w_a, jnp.zeros((1, w_a.shape[1]), F32), gate=False)
        gates = _norm_proj(xt, row(mix_norm[l]), w_gate, row(gate_b[l]), gate=True)

        o_gla = _gla(p, bf(_pad_rows(gla_w_a2[l], LANE)), row(gla_b_a[l]), row(gla_gn_w[l]), B, S)

        prep = _rwkv_prep(p, mu, row(rwkv_w0[l]), bf(_pad_rows(rwkv_w_w2[l], LANE)), row(rwkv_a0[l]),
                          bf(_pad_rows(rwkv_w_a2[l], LANE)), bf(rwkv_w_g2[l]), row(rwkv_k_k[l]),
                          row(rwkv_k_a[l]), row(rwkv_r_k[l]), B, S)
        o_rw = _rwkv_core(*prep, row(rwkv_lnx_w[l]), row(rwkv_lnx_b[l]), B, S)

        xt = _merge(xt, o_gla, o_rw, gates, bf(w_branch[l, :GLA_V]), bf(w_branch[l, GLA_V:]), bf(w_out[l]))

        xt = _ffn(xt, row(ffn2_norm[l]), bf(ffn2_wg[l]), bf(ffn2_wu[l]), bf(ffn2_wd[l]), row(final_norm),
                  final_norm=last)
    if depth == 0:
        raise ValueError("depth must be >= 1")
    return xt.reshape(B, S, D)
```

```python
import functools

import jax
import jax.numpy as jnp
from jax import lax
from jax.experimental import pallas as pl
from jax.experimental.pallas import tpu as pltpu

F32 = jnp.float32
BF16 = jnp.bfloat16

NORM_EPS = 1e-6
GN_EPS = 64e-5
GLA_TAU = 16.0
CHUNK = 64

GLA_HEADS = 4
GLA_DK = 128
GLA_DV = 256
GLA_QK = GLA_HEADS * GLA_DK
GLA_V = GLA_HEADS * GLA_DV
GLA_LORA = 16

RWKV_HD = 64
RWKV_W = 1024
DECAY_LORA = 96
AAA_LORA = 96
GATE_LORA = 256

LANE = 128
PAIR = 2 * RWKV_HD
N_PAIR = RWKV_W // PAIR

RW_GROUP = 3 * RWKV_W + LANE + LANE + GATE_LORA
GLA_GROUP = RW_GROUP
RW_WD = 3 * RWKV_W
GLA_AD = 2 * GLA_QK + 2 * GLA_V
RW_AD = RW_WD + LANE
RW_GD = RW_AD + LANE

VMEM_LIMIT = 56 * 1024 * 1024


def _cparams(sem):
    return pltpu.CompilerParams(dimension_semantics=sem, vmem_limit_bytes=VMEM_LIMIT)


def _dot(a, b):
    return jnp.dot(a, b, preferred_element_type=F32)


def _dot_nt(a, b):
    return lax.dot_general(a, b, (((1,), (1,)), ((), ())), preferred_element_type=F32)


def _dot_tn(a, b):
    return lax.dot_general(a, b, (((0,), (0,)), ((), ())), preferred_element_type=F32)


def _dot_split(a, b_bf16):
    hi = a.astype(BF16)
    lo = (a - hi.astype(F32)).astype(BF16)
    return _dot(hi, b_bf16) + _dot(lo, b_bf16)


def _rmsnorm(x, g):
    return x * lax.rsqrt(jnp.mean(x * x, axis=-1, keepdims=True) + NORM_EPS) * g


def _sigmoid(z):
    return 1.0 / (1.0 + jnp.exp(-z))


def _softplus(z):
    return jnp.maximum(z, 0.0) + jnp.log(1.0 + jnp.exp(-jnp.abs(z)))


def _ffn_kernel(x_ref, g_ref, wg_ref, wu_ref, wd_ref, fn_ref, o_ref, h_scr, *, final_norm):
    j = pl.program_id(1)

    @pl.when(j == 0)
    def _():
        h_scr[...] = _rmsnorm(x_ref[...], g_ref[...]).astype(BF16)
        o_ref[...] = jnp.zeros_like(o_ref)

    h = h_scr[...]
    a = _dot(h, wg_ref[...])
    u = _dot(h, wu_ref[...])
    act = (a * _sigmoid(a) * u).astype(BF16)
    o_ref[...] += _dot(act, wd_ref[...])

    @pl.when(j == pl.num_programs(1) - 1)
    def _():
        y = x_ref[...] + 0.5 * o_ref[...]
        if final_norm:
            y = _rmsnorm(y, fn_ref[...])
        o_ref[...] = y


def _ffn(x, g, wg, wu, wd, fn, *, final_norm, tm=1024, tf=512):
    T, D = x.shape
    FF = wg.shape[1]
    return pl.pallas_call(
        functools.partial(_ffn_kernel, final_norm=final_norm),
        out_shape=jax.ShapeDtypeStruct((T, D), F32),
        grid=(T // tm, FF // tf),
        in_specs=[
            pl.BlockSpec((tm, D), lambda i, j: (i, 0), pipeline_mode=pl.Buffered(1)),
            pl.BlockSpec((1, D), lambda i, j: (0, 0)),
            pl.BlockSpec((D, tf), lambda i, j: (0, j)),
            pl.BlockSpec((D, tf), lambda i, j: (0, j)),
            pl.BlockSpec((tf, D), lambda i, j: (j, 0)),
            pl.BlockSpec((1, D), lambda i, j: (0, 0)),
        ],
        out_specs=pl.BlockSpec((tm, D), lambda i, j: (i, 0)),
        scratch_shapes=[pltpu.VMEM((tm, D), BF16)],
        compiler_params=_cparams(("parallel", "arbitrary")),
        name="ffn_final" if final_norm else "ffn",
    )(x, g, wg, wu, wd, fn)


def _proj_kernel(x_ref, g_ref, w_ref, b_ref, o_ref, h_scr, *, gate):
    @pl.when(pl.program_id(1) == 0)
    def _():
        h_scr[...] = _rmsnorm(x_ref[...], g_ref[...]).astype(BF16)

    y = _dot(h_scr[...], w_ref[...])
    if gate:
        y = _sigmoid(y + b_ref[...])
    o_ref[...] = y


def _norm_proj(x, g, w, b, *, gate, tm=1024, tn=1024):
    T, D = x.shape
    N = w.shape[1]
    return pl.pallas_call(
        functools.partial(_proj_kernel, gate=gate),
        out_shape=jax.ShapeDtypeStruct((T, N), F32),
        grid=(T // tm, N // tn),
        in_specs=[
            pl.BlockSpec((tm, D), lambda i, j: (i, 0)),
            pl.BlockSpec((1, D), lambda i, j: (0, 0)),
            pl.BlockSpec((D, tn), lambda i, j: (0, j)),
            pl.BlockSpec((1, tn), lambda i, j: (0, j)),
        ],
        out_specs=pl.BlockSpec((tm, tn), lambda i, j: (i, j)),
        scratch_shapes=[pltpu.VMEM((tm, D), BF16)],
        compiler_params=_cparams(("parallel", "arbitrary")),
        name="gate_proj" if gate else "in_proj",
    )(x, g, w, b)


def _chunk_tri(n):
    r = lax.broadcasted_iota(jnp.int32, (n, n), 0)
    c = lax.broadcasted_iota(jnp.int32, (n, n), 1)
    return jnp.where((r >= c) & (r // CHUNK == c // CHUNK), 1.0, 0.0).astype(BF16)


def _gla_kernel(p_ref, wa2_ref, ba_ref, gn_ref, o_ref, st_scr, *, tg):
    @pl.when(pl.program_id(1) == 0)
    def _():
        st_scr[...] = jnp.zeros_like(st_scr)

    gad = p_ref[:, GLA_AD:GLA_AD + LANE].astype(BF16)
    z = _dot(gad, wa2_ref[...]) + ba_ref[...]
    log_alpha = -_softplus(-z) * (1.0 / GLA_TAU)
    tri = _chunk_tri(tg)
    hi = log_alpha.astype(BF16)
    lo = (log_alpha - hi.astype(F32)).astype(BF16)
    cum = _dot(tri, hi) + _dot(tri, lo)
    gn = gn_ref[...]
    scale = GLA_DK ** -0.5

    n_chunks = tg // CHUNK
    rows = [slice(c * CHUNK, (c + 1) * CHUNK) for c in range(n_chunks)]
    keys = [slice(h * GLA_DK, (h + 1) * GLA_DK) for h in range(GLA_HEADS)]
    vals = [slice(h * GLA_DV, (h + 1) * GLA_DV) for h in range(GLA_HEADS)]
    units = [(c, h) for c in range(n_chunks) for h in range(GLA_HEADS)]
    tot = [cum[r, :][CHUNK - 1:CHUNK, :] for r in rows]
    etot = [jnp.exp(t) for t in tot]
    kdec = [(p_ref[rows[c], GLA_QK:2 * GLA_QK] * jnp.exp(tot[c] - cum[rows[c], :])).astype(BF16)
            for c in range(n_chunks)]
    q = [(p_ref[r, 0:GLA_QK] * scale).astype(BF16) for r in rows]
    inc = {}
    for c, h in units:
        v_h = p_ref[rows[c], 2 * GLA_QK + h * GLA_DV:2 * GLA_QK + (h + 1) * GLA_DV].astype(BF16)
        inc[c, h] = _dot_tn(v_h, kdec[c][:, keys[h]])
    st = {}
    for h in range(GLA_HEADS):
        prev = st_scr[h]
        for c in range(n_chunks):
            prev = prev * etot[c][:, keys[h]] + inc[c, h]
            st[c, h] = prev
        st_scr[h] = prev
    out = {u: _dot_nt(q[u[0]][:, keys[u[1]]], st[u].astype(BF16)) for u in units}
    for c, h in units:
        o = out[c, h]
        r_h = p_ref[rows[c], 2 * GLA_QK + GLA_V + h * GLA_DV:2 * GLA_QK + GLA_V + (h + 1) * GLA_DV]
        o = o * lax.rsqrt(jnp.mean(o * o, axis=-1, keepdims=True) + NORM_EPS) * gn
        o = o * (r_h * _sigmoid(r_h))
        o_ref[rows[c], vals[h]] = o.astype(BF16)


def _gla(p, wa2, ba, gn, B, S, *, tg=256):
    T = B * S
    nb = S // tg
    return pl.pallas_call(
        functools.partial(_gla_kernel, tg=tg),
        out_shape=jax.ShapeDtypeStruct((T, GLA_V), BF16),
        grid=(B, nb),
        in_specs=[
            pl.BlockSpec((tg, GLA_GROUP), lambda b, i: (b * nb + i, 1)),
            pl.BlockSpec((LANE, GLA_QK), lambda b, i: (0, 0)),
            pl.BlockSpec((1, GLA_QK), lambda b, i: (0, 0)),
            pl.BlockSpec((1, GLA_DV), lambda b, i: (0, 0)),
        ],
        out_specs=pl.BlockSpec((tg, GLA_V), lambda b, i: (b * nb + i, 0)),
        scratch_shapes=[pltpu.VMEM((GLA_HEADS, GLA_DV, GLA_DK), F32)],
        compiler_params=_cparams(("parallel", "arbitrary")),
        name="gla",
    )(p, wa2, ba, gn)


def _head_ones(n):
    r = lax.broadcasted_iota(jnp.int32, (n, n), 0)
    c = lax.broadcasted_iota(jnp.int32, (n, n), 1)
    return jnp.where(r // RWKV_HD == c // RWKV_HD, 1.0, 0.0).astype(BF16)


def _head_sum(x, ones_pair):
    parts = [_dot_split(x[:, j * PAIR:(j + 1) * PAIR], ones_pair) for j in range(x.shape[1] // PAIR)]
    return jnp.concatenate(parts, axis=1)


def _rwkv_prep_kernel(p_ref, mu_ref, w0_ref, ww2_ref, a0_ref, wa2_ref, wg2_ref, kk_ref, ka_ref, rk_ref,
                      rt_ref, bt_ref, at_ref, kt_ref, v_ref, bonus_ref, g_ref, ptot_ref,
                      carry_scr, *, tm):
    @pl.when(pl.program_id(1) == 0)
    def _():
        carry_scr[...] = jnp.zeros_like(carry_scr)

    p = p_ref[...]
    last = carry_scr[...]
    carry_scr[...] = p[tm - 1:tm, :]
    row = lax.broadcasted_iota(jnp.int32, p.shape, 0)
    prev = jnp.where(row == 0, last, pltpu.roll(p, 1, axis=0))
    p = p + mu_ref[...] * (prev - p)

    r = p[:, 0:RWKV_W]
    k = p[:, RWKV_W:2 * RWKV_W]
    v = p[:, 2 * RWKV_W:3 * RWKV_W]
    wd = p[:, RW_WD:RW_WD + LANE]
    ad = p[:, RW_AD:RW_AD + LANE]
    gd = p[:, RW_GD:RW_GD + GATE_LORA]

    w_raw = w0_ref[...] + _dot(jnp.tanh(wd).astype(BF16), ww2_ref[...])
    log_w = -jnp.exp(-_softplus(-w_raw) - 0.5)
    a = _sigmoid(a0_ref[...] + _dot(ad.astype(BF16), wa2_ref[...]))
    g_ref[...] = _dot(_sigmoid(gd).astype(BF16), wg2_ref[...])

    ones_pair = _head_ones(PAIR)
    kk = k * kk_ref[...]
    kk = kk / jnp.maximum(jnp.sqrt(_head_sum(kk * kk, ones_pair)), 1e-12)
    kp = k * (1.0 + (a - 1.0) * ka_ref[...])
    bonus_ref[...] = _head_sum(r * kp * rk_ref[...], ones_pair) * v
    v_ref[...] = v.astype(BF16)

    tri = _chunk_tri(tm)
    hi = log_w.astype(BF16)
    lo = (log_w - hi.astype(F32)).astype(BF16)
    cum = _dot(tri, hi) + _dot(tri, lo)
    nalpha = -(kk * a)
    for ci in range(tm // CHUNK):
        rows = slice(ci * CHUNK, (ci + 1) * CHUNK)
        cum_c = cum[rows, :]
        tot = cum_c[CHUNK - 1:CHUNK, :]
        e_inv = jnp.exp(-cum_c)
        rt_ref[rows, :] = (r[rows, :] * jnp.exp(cum_c)).astype(BF16)
        bt_ref[rows, :] = (kk[rows, :] * jnp.exp(cum_c - log_w[rows, :])).astype(BF16)
        at_ref[rows, :] = (nalpha[rows, :] * e_inv).astype(BF16)
        kt_ref[rows, :] = (kp[rows, :] * e_inv).astype(BF16)
        ptot_ref[ci] = jnp.exp(tot)


def _rwkv_prep(p, mu, w0, ww2, a0, wa2, wg2, k_k, k_a, r_k, B, S, *, tm=256):
    T = B * S
    nb = S // tm
    cpt = tm // CHUNK
    vec = lambda n: pl.BlockSpec((1, n), lambda b, i: (0, 0))
    tok = lambda: pl.BlockSpec((tm, RWKV_W), lambda b, i: (b * nb + i, 0))
    bf = jax.ShapeDtypeStruct((T, RWKV_W), BF16)
    f32 = jax.ShapeDtypeStruct((T, RWKV_W), F32)
    return pl.pallas_call(
        functools.partial(_rwkv_prep_kernel, tm=tm),
        out_shape=[bf, bf, bf, bf, bf, f32, f32,
                   jax.ShapeDtypeStruct((T // CHUNK, 1, RWKV_W), F32)],
        grid=(B, nb),
        in_specs=[
            pl.BlockSpec((tm, RW_GROUP), lambda b, i: (b * nb + i, 0)),
            vec(RW_GROUP), vec(RWKV_W),
            pl.BlockSpec((LANE, RWKV_W), lambda b, i: (0, 0)),
            vec(RWKV_W),
            pl.BlockSpec((LANE, RWKV_W), lambda b, i: (0, 0)),
            pl.BlockSpec((GATE_LORA, RWKV_W), lambda b, i: (0, 0)),
            vec(RWKV_W), vec(RWKV_W), vec(RWKV_W),
        ],
        out_specs=[tok() for _ in range(7)]
        + [pl.BlockSpec((cpt, 1, RWKV_W), lambda b, i: (b * nb + i, 0, 0))],
        scratch_shapes=[pltpu.VMEM((1, RW_GROUP), F32)],
        compiler_params=_cparams(("parallel", "arbitrary")),
        name="rwkv_prep",
    )(p, mu, w0, ww2, a0, wa2, wg2, k_k, k_a, r_k)


def _rwkv_core_kernel(rt_ref, bt_ref, at_ref, kt_ref, v_ref, bonus_ref, g_ref, ptot_ref,
                      lw_ref, lb_ref, o_ref, h_scr):
    @pl.when(pl.program_id(0) == 0)
    def _():
        h_scr[...] = jnp.zeros_like(h_scr)

    n_batch = rt_ref.shape[0]
    ri = lax.broadcasted_iota(jnp.int32, (PAIR, PAIR), 0)
    ci = lax.broadcasted_iota(jnp.int32, (PAIR, PAIR), 1)
    head_blk = (ri // RWKV_HD) == (ci // RWKV_HD)
    strict = ri > ci
    lower = ri >= ci
    eye = ri == ci
    blk8 = (ri // 8) == (ci // 8)
    eye_f = jnp.where(eye, 1.0, 0.0)
    ones_pair = jnp.where(head_blk, 1.0, 0.0).astype(BF16)
    ones_2 = jnp.concatenate([ones_pair, ones_pair], axis=0)
    zeros_b = jnp.zeros((PAIR, PAIR), BF16)

    def stack(x):
        return jnp.where(head_blk, jnp.concatenate([x, x], axis=0), jnp.zeros((), x.dtype))

    def dot_packed(lhs, rhs):
        out = []
        for j in range(0, len(lhs), 2):
            l2 = jnp.concatenate([lhs[j], lhs[j + 1]], axis=1)
            r2 = jnp.concatenate([jnp.concatenate([rhs[j], zeros_b], axis=1),
                                  jnp.concatenate([zeros_b, rhs[j + 1]], axis=1)], axis=0)
            o = _dot(l2, r2)
            out += [o[:, :PAIR], o[:, PAIR:]]
        return out

    def head_sum(x):
        hi = x.astype(BF16)
        lo = (x - hi.astype(F32)).astype(BF16)
        return _dot(jnp.concatenate([hi, lo], axis=1), ones_2)

    pairs = range(n_batch * N_PAIR)
    where = [(b, slice(j * PAIR, (j + 1) * PAIR)) for b in range(n_batch) for j in range(N_PAIR)]
    ptot = [ptot_ref[b, 0, :, l] for b, l in where]
    bx = [stack(bt_ref[b, :, l]) for b, l in where]
    rx = [stack(rt_ref[b, :, l]) for b, l in where]
    at = [at_ref[b, :, l] for b, l in where]
    kt = [kt_ref[b, :, l] for b, l in where]
    ak = [jnp.concatenate([stack(at[j]), stack(kt[j])], axis=0) for j in pairs]
    ak2 = [jnp.concatenate([stack((at[j].astype(F32) * ptot[j]).astype(BF16)),
                            stack((kt[j].astype(F32) * ptot[j]).astype(BF16))], axis=0) for j in pairs]
    vx = [stack(v_ref[b, :, l]) for b, l in where]

    gb = [_dot_nt(bx[j], ak[j]) for j in pairs]
    gr = [_dot_nt(rx[j], ak[j]) for j in pairs]
    a_ab = [jnp.where(strict, x[:, :PAIR], 0.0) for x in gb]
    a_kb = [jnp.where(strict, x[:, PAIR:], 0.0).astype(BF16) for x in gb]
    a_r = [jnp.concatenate([jnp.where(lower, x[:, :PAIR], 0.0),
                            jnp.where(lower, x[:, PAIR:], 0.0)], axis=1).astype(BF16) for x in gr]

    a_d = [jnp.where(blk8, x, 0.0) for x in a_ab]
    a_db = [x.astype(BF16) for x in a_d]
    pw = [x.astype(BF16) for x in dot_packed(a_db, a_db)]
    s = [eye_f + x for x in a_d]
    sp = [_dot(pw[j], jnp.concatenate([s[j].astype(BF16), pw[j]], axis=1)) for j in pairs]
    s = [s[j] + sp[j][:, :PAIR] for j in pairs]
    pw = [x[:, PAIR:].astype(BF16) for x in sp]
    ps = dot_packed(pw, [x.astype(BF16) for x in s])
    s = [s[j] + ps[j] for j in pairs]
    for width in (8, 16, 32):
        off = ((ri // (2 * width)) == (ci // (2 * width))) & ((ri // width) != (ci // width))
        e = [jnp.where(off, x, 0.0).astype(BF16) for x in a_ab]
        sb = [x.astype(BF16) for x in s]
        es = [x.astype(BF16) for x in dot_packed(e, sb)]
        ses = dot_packed(sb, es)
        s = [s[j] + ses[j] for j in pairs]
    t_inv = [x.astype(BF16) for x in s]

    kv = [x.astype(BF16) for x in dot_packed(a_kb, vx)]
    wu = [_dot(t_inv[j], jnp.concatenate([bx[j], kv[j]], axis=1)).astype(BF16) for j in pairs]
    z = [jnp.concatenate([wu[j], jnp.concatenate([zeros_b, vx[j]], axis=1)], axis=0) for j in pairs]
    mc = [_dot_tn(ak2[j], z[j]) for j in pairs]
    qy = [_dot(a_r[j], z[j]) for j in pairs]

    m = [(mc[j][:, :PAIR] + jnp.where(eye, ptot[j], 0.0)).astype(BF16) for j in pairs]
    q = [(qy[j][:, :PAIR] + rx[j].astype(F32)).astype(BF16) for j in pairs]
    hb = [h_scr[j].astype(BF16) for j in pairs]
    qh = dot_packed(q, hb)
    mh = dot_packed(m, hb)
    for j in pairs:
        h_scr[j] = mh[j] + mc[j][:, PAIR:]
    y = [qh[j] + qy[j][:, PAIR:] for j in pairs]
    y = [x[:RWKV_HD, :] + x[RWKV_HD:, :] for x in y]

    mean = [head_sum(x) * (1.0 / RWKV_HD) for x in y]
    yc = [y[j] - mean[j] for j in pairs]
    var = [head_sum(x * x) * (1.0 / RWKV_HD) for x in yc]
    for j in pairs:
        b, l = where[j]
        yn = yc[j] * lax.rsqrt(var[j] + GN_EPS) * lw_ref[:, l] + lb_ref[:, l]
        o_ref[b, :, l] = ((yn + bonus_ref[b, :, l]) * g_ref[b, :, l]).astype(BF16)


def _rwkv_core(rt, bt, at, kt, v, bonus, g, ptot, lnx_w, lnx_b, B, S):
    nc = S // CHUNK
    seq = lambda a: a.reshape(B, S, RWKV_W)
    tok = lambda: pl.BlockSpec((B, CHUNK, RWKV_W), lambda c: (0, c, 0))
    vec = lambda: pl.BlockSpec((1, RWKV_W), lambda c: (0, 0))
    out = pl.pallas_call(
        _rwkv_core_kernel,
        out_shape=jax.ShapeDtypeStruct((B, S, RWKV_W), BF16),
        grid=(nc,),
        in_specs=[tok() for _ in range(7)]
        + [pl.BlockSpec((B, 1, 1, RWKV_W), lambda c: (0, c, 0, 0)), vec(), vec()],
        out_specs=tok(),
        scratch_shapes=[pltpu.VMEM((B * N_PAIR, PAIR, PAIR), F32)],
        compiler_params=_cparams(("arbitrary",)),
        name="rwkv_core",
    )(seq(rt), seq(bt), seq(at), seq(kt), seq(v), seq(bonus), seq(g),
      ptot.reshape(B, nc, 1, RWKV_W), lnx_w, lnx_b)
    return out.reshape(B * S, RWKV_W)


def _merge_kernel(x_ref, og_ref, or_ref, gt_ref, wb1_ref, wb2_ref, wo_ref, o_ref):
    D = x_ref.shape[1]
    y_gla = _dot(og_ref[...], wb1_ref[...])
    y_rw = _dot(or_ref[...], wb2_ref[...])
    merged = gt_ref[:, :D] * y_gla + gt_ref[:, D:] * y_rw
    o_ref[...] = x_ref[...] + _dot(merged.astype(BF16), wo_ref[...])


def _merge(x, o_gla, o_rw, gates, wb1, wb2, wo, *, tm=256):
    T, D = x.shape
    const = lambda shape: pl.BlockSpec(shape, lambda i: (0, 0), pipeline_mode=pl.Buffered(1))
    return pl.pallas_call(
        _merge_kernel,
        out_shape=jax.ShapeDtypeStruct((T, D), F32),
        grid=(T // tm,),
        in_specs=[
            pl.BlockSpec((tm, D), lambda i: (i, 0)),
            pl.BlockSpec((tm, GLA_V), lambda i: (i, 0)),
            pl.BlockSpec((tm, RWKV_W), lambda i: (i, 0)),
            pl.BlockSpec((tm, 2 * D), lambda i: (i, 0)),
            const((GLA_V, D)), const((RWKV_W, D)), const((D, D)),
        ],
        out_specs=pl.BlockSpec((tm, D), lambda i: (i, 0)),
        compiler_params=_cparams(("parallel",)),
        name="merge_out",
    )(x, o_gla, o_rw, gates, wb1, wb2, wo)


def _pad_cols(w, n):
    return jnp.pad(w, ((0, 0), (0, n - w.shape[1])))


def _pad_rows(w, n):
    return jnp.pad(w, ((0, n - w.shape[0]), (0, 0)))


def _pack_in_proj(w_in, rwkv_mu):
    gla_in = 2 * GLA_QK + 2 * GLA_V + GLA_LORA
    rw_in = 3 * RWKV_W + DECAY_LORA + AAA_LORA + GATE_LORA
    w_gla = w_in[:, :gla_in]
    w_rw = w_in[:, gla_in:gla_in + rw_in]
    w_gate = w_in[:, gla_in + rw_in:]

    def rw_pack(m):
        main = m[:, :3 * RWKV_W]
        wd = _pad_cols(m[:, 3 * RWKV_W:3 * RWKV_W + DECAY_LORA], LANE)
        ad = _pad_cols(m[:, 3 * RWKV_W + DECAY_LORA:3 * RWKV_W + DECAY_LORA + AAA_LORA], LANE)
        gd = m[:, 3 * RWKV_W + DECAY_LORA + AAA_LORA:]
        return jnp.concatenate([main, wd, ad, gd], axis=1)

    w_a = jnp.concatenate([rw_pack(w_rw), _pad_cols(w_gla, GLA_GROUP)], axis=1)
    return w_a.astype(BF16), w_gate.astype(BF16), rw_pack(rwkv_mu[None, :])


def kernel(x, ffn1_norm, ffn1_wg, ffn1_wu, ffn1_wd, mix_norm, w_in, gla_w_a2, gla_b_a, gla_gn_w, rwkv_mu,
           rwkv_w0, rwkv_w_w2, rwkv_a0, rwkv_w_a2, rwkv_w_g2, rwkv_k_k, rwkv_k_a, rwkv_r_k, rwkv_lnx_w,
           rwkv_lnx_b, gate_b, w_branch, w_out, ffn2_norm, ffn2_wg, ffn2_wu, ffn2_wd, final_norm):
    B, S, D = x.shape
    T = B * S
    depth = ffn1_norm.shape[0]
    bf = lambda w: w.astype(BF16)
    row = lambda v: v.reshape(1, -1)
    xt = x.reshape(T, D)
    for l in range(depth):
        last = l == depth - 1
        xt = _ffn(xt, row(ffn1_norm[l]), bf(ffn1_wg[l]), bf(ffn1_wu[l]), bf(ffn1_wd[l]), row(final_norm),
                  final_norm=False)

        w_a, w_gate, mu = _pack_in_proj(w_in[l], rwkv_mu[l])
        p = _norm_proj(xt, row(mix_norm[l]), w_a, jnp.zeros((1, w_a.shape[1]), F32), gate=False)
        gates = _norm_proj(xt, row(mix_norm[l]), w_gate, row(gate_b[l]), gate=True)

        o_gla = _gla(p, bf(_pad_rows(gla_w_a2[l], LANE)), row(gla_b_a[l]), row(gla_gn_w[l]), B, S)

        prep = _rwkv_prep(p, mu, row(rwkv_w0[l]), bf(_pad_rows(rwkv_w_w2[l], LANE)), row(rwkv_a0[l]),
                          bf(_pad_rows(rwkv_w_a2[l], LANE)), bf(rwkv_w_g2[l]), row(rwkv_k_k[l]),
                          row(rwkv_k_a[l]), row(rwkv_r_k[l]), B, S)
        o_rw = _rwkv_core(*prep, row(rwkv_lnx_w[l]), row(rwkv_lnx_b[l]), B, S)

        xt = _merge(xt, o_gla, o_rw, gates, bf(w_branch[l, :GLA_V]), bf(w_branch[l, GLA_V:]), bf(w_out[l]))

        xt = _ffn(xt, row(ffn2_norm[l]), bf(ffn2_wg[l]), bf(ffn2_wu[l]), bf(ffn2_wd[l]), row(final_norm),
                  final_norm=last)
    if depth == 0:
        raise ValueError("depth must be >= 1")
    return xt.reshape(B, S, D)
```

```python
import functools

import jax
import jax.numpy as jnp
from jax import lax
from jax.experimental import pallas as pl
from jax.experimental.pallas import tpu as pltpu

F32 = jnp.float32
BF16 = jnp.bfloat16

NORM_EPS = 1e-6
GN_EPS = 64e-5
GLA_TAU = 16.0
CHUNK = 64

GLA_HEADS = 4
GLA_DK = 128
GLA_DV = 256
GLA_QK = GLA_HEADS * GLA_DK
GLA_V = GLA_HEADS * GLA_DV
GLA_LORA = 16

RWKV_HD = 64
RWKV_W = 1024
DECAY_LORA = 96
AAA_LORA = 96
GATE_LORA = 256

LANE = 128
PAIR = 2 * RWKV_HD
N_PAIR = RWKV_W // PAIR

RW_GROUP = 3 * RWKV_W + LANE + LANE + GATE_LORA
GLA_GROUP = RW_GROUP
RW_WD = 3 * RWKV_W
GLA_AD = 2 * GLA_QK + 2 * GLA_V
RW_AD = RW_WD + LANE
RW_GD = RW_AD + LANE

VMEM_LIMIT = 56 * 1024 * 1024
BF16_SUBLANES = 16

FFN_TM, FFN_TF = 1024, 512
PROJ_TM, PROJ_TN = 1024, 1024


def _rows_per_block(rows, steps):
    for rb in range(BF16_SUBLANES, rows + 1, BF16_SUBLANES):
        if rows % rb == 0 and rows // rb <= steps:
            return rb
    raise ValueError(f"cannot split {rows} rows over {steps} steps")


def _cparams(sem):
    return pltpu.CompilerParams(dimension_semantics=sem, vmem_limit_bytes=VMEM_LIMIT)


def _dot(a, b):
    return jnp.dot(a, b, preferred_element_type=F32)


def _dot_nt(a, b):
    return lax.dot_general(a, b, (((1,), (1,)), ((), ())), preferred_element_type=F32)


def _dot_tn(a, b):
    return lax.dot_general(a, b, (((0,), (0,)), ((), ())), preferred_element_type=F32)


def _dot_split(a, b_bf16):
    hi = a.astype(BF16)
    lo = (a - hi.astype(F32)).astype(BF16)
    return _dot(hi, b_bf16) + _dot(lo, b_bf16)


def _rmsnorm(x, g):
    return x * lax.rsqrt(jnp.mean(x * x, axis=-1, keepdims=True) + NORM_EPS) * g


def _sigmoid(z):
    return 1.0 / (1.0 + jnp.exp(-z))


def _softplus(z):
    return jnp.maximum(z, 0.0) + jnp.log(1.0 + jnp.exp(-jnp.abs(z)))


CAST = "cast"
PACK_IN = "pack_in"

GLA_IN = 2 * GLA_QK + 2 * GLA_V + GLA_LORA
RWKV_IN = 3 * RWKV_W + DECAY_LORA + AAA_LORA + GATE_LORA


def _pack_in_cols(w):
    zeros = lambda n: jnp.zeros((w.shape[0], n), w.dtype)
    o = GLA_IN
    rw = [w[:, o:o + 3 * RWKV_W]]
    o += 3 * RWKV_W
    rw += [w[:, o:o + DECAY_LORA], zeros(LANE - DECAY_LORA)]
    o += DECAY_LORA
    rw += [w[:, o:o + AAA_LORA], zeros(LANE - AAA_LORA)]
    o += AAA_LORA
    rw += [w[:, o:o + GATE_LORA]]
    gla = [w[:, :GLA_IN], zeros(GLA_GROUP - GLA_IN)]
    return jnp.concatenate(rw + gla, axis=1), w[:, GLA_IN + RWKV_IN:]


def _aux_outputs(kind, w):
    rows, cols = w.shape
    if kind == CAST:
        return [cols]
    return [RW_GROUP + GLA_GROUP, cols - GLA_IN - RWKV_IN]


def _aux_plan(aux, nj):
    in_specs, out_specs, out_shapes, meta = [], [], [], []
    for kind, w, rb in aux:
        n_blocks = w.shape[0] // rb
        index = lambda i, j, n=n_blocks: (jnp.minimum(i * nj + j, n - 1), 0)
        in_specs.append(pl.BlockSpec((rb, w.shape[1]), index))
        widths = _aux_outputs(kind, w)
        for c in widths:
            out_specs.append(pl.BlockSpec((rb, c), index))
            out_shapes.append(jax.ShapeDtypeStruct((w.shape[0], c), BF16))
        meta.append((kind, n_blocks, len(widths)))
    return in_specs, out_specs, out_shapes, tuple(meta)


def _aux_run(meta, in_refs, out_refs):
    step = pl.program_id(0) * pl.num_programs(1) + pl.program_id(1)
    k = 0
    for (kind, n_blocks, n_out), w_ref in zip(meta, in_refs):
        outs = out_refs[k:k + n_out]
        k += n_out

        @pl.when(step < n_blocks)
        def _(kind=kind, w_ref=w_ref, outs=outs):
            if kind == CAST:
                outs[0][...] = w_ref[...].astype(BF16)
            else:
                packed, gates = _pack_in_cols(w_ref[...])
                outs[0][...] = packed.astype(BF16)
                outs[1][...] = gates.astype(BF16)


def _ffn_kernel(*refs, final_norm, aux_meta):
    x_ref, g_ref, wg_ref, wu_ref, wd_ref, fn_ref = refs[:6]
    n_aux = len(aux_meta)
    o_ref = refs[6 + n_aux]
    h_scr = refs[-1]
    _aux_run(aux_meta, refs[6:6 + n_aux], refs[7 + n_aux:-1])
    j = pl.program_id(1)

    @pl.when(j == 0)
    def _():
        h_scr[...] = _rmsnorm(x_ref[...], g_ref[...]).astype(BF16)
        o_ref[...] = jnp.zeros_like(o_ref)

    h = h_scr[...]
    a = _dot(h, wg_ref[...])
    u = _dot(h, wu_ref[...])
    act = (a * _sigmoid(a) * u).astype(BF16)
    o_ref[...] += _dot(act, wd_ref[...])

    @pl.when(j == pl.num_programs(1) - 1)
    def _():
        y = x_ref[...] + 0.5 * o_ref[...]
        if final_norm:
            y = _rmsnorm(y, fn_ref[...])
        o_ref[...] = y


def _ffn(x, g, wg, wu, wd, fn, *, final_norm, aux=(), tm=FFN_TM, tf=FFN_TF):
    T, D = x.shape
    FF = wg.shape[1]
    nj = FF // tf
    aux_in, aux_out, aux_shapes, aux_meta = _aux_plan(aux, nj)
    return pl.pallas_call(
        functools.partial(_ffn_kernel, final_norm=final_norm, aux_meta=aux_meta),
        out_shape=[jax.ShapeDtypeStruct((T, D), F32)] + aux_shapes,
        grid=(T // tm, nj),
        in_specs=[
            pl.BlockSpec((tm, D), lambda i, j: (i, 0), pipeline_mode=pl.Buffered(1)),
            pl.BlockSpec((1, D), lambda i, j: (0, 0)),
            pl.BlockSpec((D, tf), lambda i, j: (0, j)),
            pl.BlockSpec((D, tf), lambda i, j: (0, j)),
            pl.BlockSpec((tf, D), lambda i, j: (j, 0)),
            pl.BlockSpec((1, D), lambda i, j: (0, 0)),
        ] + aux_in,
        out_specs=[pl.BlockSpec((tm, D), lambda i, j: (i, 0))] + aux_out,
        scratch_shapes=[pltpu.VMEM((tm, D), BF16)],
        compiler_params=_cparams(("arbitrary", "arbitrary")),
        name="ffn_final" if final_norm else "ffn",
    )(x, g, wg, wu, wd, fn, *[w for _, w, _ in aux])


def _proj_kernel(*refs, gate, aux_meta):
    x_ref, g_ref, w_ref, b_ref = refs[:4]
    n_aux = len(aux_meta)
    o_ref = refs[4 + n_aux]
    h_scr = refs[-1]
    _aux_run(aux_meta, refs[4:4 + n_aux], refs[5 + n_aux:-1])

    @pl.when(pl.program_id(1) == 0)
    def _():
        h_scr[...] = _rmsnorm(x_ref[...], g_ref[...]).astype(BF16)

    y = _dot(h_scr[...], w_ref[...])
    if gate:
        y = _sigmoid(y + b_ref[...])
    o_ref[...] = y


def _norm_proj(x, g, w, b, *, gate, aux=(), tm=PROJ_TM, tn=PROJ_TN):
    T, D = x.shape
    N = w.shape[1]
    nj = N // tn
    aux_in, aux_out, aux_shapes, aux_meta = _aux_plan(aux, nj)
    return pl.pallas_call(
        functools.partial(_proj_kernel, gate=gate, aux_meta=aux_meta),
        out_shape=[jax.ShapeDtypeStruct((T, N), F32)] + aux_shapes,
        grid=(T // tm, nj),
        in_specs=[
            pl.BlockSpec((tm, D), lambda i, j: (i, 0)),
            pl.BlockSpec((1, D), lambda i, j: (0, 0)),
            pl.BlockSpec((D, tn), lambda i, j: (0, j)),
            pl.BlockSpec((1, tn), lambda i, j: (0, j)),
        ] + aux_in,
        out_specs=[pl.BlockSpec((tm, tn), lambda i, j: (i, j))] + aux_out,
        scratch_shapes=[pltpu.VMEM((tm, D), BF16)],
        compiler_params=_cparams(("arbitrary", "arbitrary")),
        name="gate_proj" if gate else "in_proj",
    )(x, g, w, b, *[w_aux for _, w_aux, _ in aux])


def _chunk_tri(n):
    r = lax.broadcasted_iota(jnp.int32, (n, n), 0)
    c = lax.broadcasted_iota(jnp.int32, (n, n), 1)
    return jnp.where((r >= c) & (r // CHUNK == c // CHUNK), 1.0, 0.0).astype(BF16)


def _gla_kernel(p_ref, wa2_ref, ba_ref, gn_ref, o_ref, st_scr, *, tg):
    @pl.when(pl.program_id(1) == 0)
    def _():
        st_scr[...] = jnp.zeros_like(st_scr)

    gad = p_ref[:, GLA_AD:GLA_AD + LANE].astype(BF16)
    z = _dot(gad, wa2_ref[...]) + ba_ref[...]
    log_alpha = -_softplus(-z) * (1.0 / GLA_TAU)
    tri = _chunk_tri(tg)
    hi = log_alpha.astype(BF16)
    lo = (log_alpha - hi.astype(F32)).astype(BF16)
    cum = _dot(tri, hi) + _dot(tri, lo)
    gn = gn_ref[...]
    scale = GLA_DK ** -0.5

    n_chunks = tg // CHUNK
    rows = [slice(c * CHUNK, (c + 1) * CHUNK) for c in range(n_chunks)]
    keys = [slice(h * GLA_DK, (h + 1) * GLA_DK) for h in range(GLA_HEADS)]
    vals = [slice(h * GLA_DV, (h + 1) * GLA_DV) for h in range(GLA_HEADS)]
    units = [(c, h) for c in range(n_chunks) for h in range(GLA_HEADS)]
    tot = [cum[r, :][CHUNK - 1:CHUNK, :] for r in rows]
    etot = [jnp.exp(t) for t in tot]
    kdec = [(p_ref[rows[c], GLA_QK:2 * GLA_QK] * jnp.exp(tot[c] - cum[rows[c], :])).astype(BF16)
            for c in range(n_chunks)]
    q = [(p_ref[r, 0:GLA_QK] * scale).astype(BF16) for r in rows]
    inc = {}
    for c, h in units:
        v_h = p_ref[rows[c], 2 * GLA_QK + h * GLA_DV:2 * GLA_QK + (h + 1) * GLA_DV].astype(BF16)
        inc[c, h] = _dot_tn(v_h, kdec[c][:, keys[h]])
    st = {}
    for h in range(GLA_HEADS):
        prev = st_scr[h]
        for c in range(n_chunks):
            prev = prev * etot[c][:, keys[h]] + inc[c, h]
            st[c, h] = prev
        st_scr[h] = prev
    out = {u: _dot_nt(q[u[0]][:, keys[u[1]]], st[u].astype(BF16)) for u in units}
    for c, h in units:
        o = out[c, h]
        r_h = p_ref[rows[c], 2 * GLA_QK + GLA_V + h * GLA_DV:2 * GLA_QK + GLA_V + (h + 1) * GLA_DV]
        o = o * lax.rsqrt(jnp.mean(o * o, axis=-1, keepdims=True) + NORM_EPS) * gn
        o = o * (r_h * _sigmoid(r_h))
        o_ref[rows[c], vals[h]] = o.astype(BF16)


def _gla(p, wa2, ba, gn, B, S, *, tg=256):
    T = B * S
    nb = S // tg
    return pl.pallas_call(
        functools.partial(_gla_kernel, tg=tg),
        out_shape=jax.ShapeDtypeStruct((T, GLA_V), BF16),
        grid=(B, nb),
        in_specs=[
            pl.BlockSpec((tg, GLA_GROUP), lambda b, i: (b * nb + i, 1)),
            pl.BlockSpec((LANE, GLA_QK), lambda b, i: (0, 0)),
            pl.BlockSpec((1, GLA_QK), lambda b, i: (0, 0)),
            pl.BlockSpec((1, GLA_DV), lambda b, i: (0, 0)),
        ],
        out_specs=pl.BlockSpec((tg, GLA_V), lambda b, i: (b * nb + i, 0)),
        scratch_shapes=[pltpu.VMEM((GLA_HEADS, GLA_DV, GLA_DK), F32)],
        compiler_params=_cparams(("parallel", "arbitrary")),
        name="gla",
    )(p, wa2, ba, gn)


def _head_ones(n):
    r = lax.broadcasted_iota(jnp.int32, (n, n), 0)
    c = lax.broadcasted_iota(jnp.int32, (n, n), 1)
    return jnp.where(r // RWKV_HD == c // RWKV_HD, 1.0, 0.0).astype(BF16)


def _head_sum(x, ones_pair):
    parts = [_dot_split(x[:, j * PAIR:(j + 1) * PAIR], ones_pair) for j in range(x.shape[1] // PAIR)]
    return jnp.concatenate(parts, axis=1)


def _rwkv_prep_kernel(p_ref, mu_ref, w0_ref, ww2_ref, a0_ref, wa2_ref, wg2_ref, kk_ref, ka_ref, rk_ref,
                      rt_ref, bt_ref, at_ref, kt_ref, v_ref, bonus_ref, g_ref, ptot_ref,
                      carry_scr, *, tm):
    @pl.when(pl.program_id(1) == 0)
    def _():
        carry_scr[...] = jnp.zeros_like(carry_scr)

    p = p_ref[...]
    last = carry_scr[...]
    carry_scr[...] = p[tm - 1:tm, :]
    row = lax.broadcasted_iota(jnp.int32, p.shape, 0)
    prev = jnp.where(row == 0, last, pltpu.roll(p, 1, axis=0))
    p = p + mu_ref[...] * (prev - p)

    r = p[:, 0:RWKV_W]
    k = p[:, RWKV_W:2 * RWKV_W]
    v = p[:, 2 * RWKV_W:3 * RWKV_W]
    wd = p[:, RW_WD:RW_WD + LANE]
    ad = p[:, RW_AD:RW_AD + LANE]
    gd = p[:, RW_GD:RW_GD + GATE_LORA]

    w_raw = w0_ref[...] + _dot(jnp.tanh(wd).astype(BF16), ww2_ref[...])
    log_w = -jnp.exp(-_softplus(-w_raw) - 0.5)
    a = _sigmoid(a0_ref[...] + _dot(ad.astype(BF16), wa2_ref[...]))
    g_ref[...] = _dot(_sigmoid(gd).astype(BF16), wg2_ref[...])

    ones_pair = _head_ones(PAIR)
    kk = k * kk_ref[...]
    kk = kk / jnp.maximum(jnp.sqrt(_head_sum(kk * kk, ones_pair)), 1e-12)
    kp = k * (1.0 + (a - 1.0) * ka_ref[...])
    bonus_ref[...] = _head_sum(r * kp * rk_ref[...], ones_pair) * v
    v_ref[...] = v.astype(BF16)

    tri = _chunk_tri(tm)
    hi = log_w.astype(BF16)
    lo = (log_w - hi.astype(F32)).astype(BF16)
    cum = _dot(tri, hi) + _dot(tri, lo)
    nalpha = -(kk * a)
    for ci in range(tm // CHUNK):
        rows = slice(ci * CHUNK, (ci + 1) * CHUNK)
        cum_c = cum[rows, :]
        tot = cum_c[CHUNK - 1:CHUNK, :]
        e_inv = jnp.exp(-cum_c)
        rt_ref[rows, :] = (r[rows, :] * jnp.exp(cum_c)).astype(BF16)
        bt_ref[rows, :] = (kk[rows, :] * jnp.exp(cum_c - log_w[rows, :])).astype(BF16)
        at_ref[rows, :] = (nalpha[rows, :] * e_inv).astype(BF16)
        kt_ref[rows, :] = (kp[rows, :] * e_inv).astype(BF16)
        ptot_ref[ci] = jnp.exp(tot)


def _rwkv_prep(p, mu, w0, ww2, a0, wa2, wg2, k_k, k_a, r_k, B, S, *, tm=256):
    T = B * S
    nb = S // tm
    cpt = tm // CHUNK
    vec = lambda n: pl.BlockSpec((1, n), lambda b, i: (0, 0))
    tok = lambda: pl.BlockSpec((tm, RWKV_W), lambda b, i: (b * nb + i, 0))
    bf = jax.ShapeDtypeStruct((T, RWKV_W), BF16)
    f32 = jax.ShapeDtypeStruct((T, RWKV_W), F32)
    return pl.pallas_call(
        functools.partial(_rwkv_prep_kernel, tm=tm),
        out_shape=[bf, bf, bf, bf, bf, f32, f32,
                   jax.ShapeDtypeStruct((T // CHUNK, 1, RWKV_W), F32)],
        grid=(B, nb),
        in_specs=[
            pl.BlockSpec((tm, RW_GROUP), lambda b, i: (b * nb + i, 0)),
            vec(RW_GROUP), vec(RWKV_W),
            pl.BlockSpec((LANE, RWKV_W), lambda b, i: (0, 0)),
            vec(RWKV_W),
            pl.BlockSpec((LANE, RWKV_W), lambda b, i: (0, 0)),
            pl.BlockSpec((GATE_LORA, RWKV_W), lambda b, i: (0, 0)),
            vec(RWKV_W), vec(RWKV_W), vec(RWKV_W),
        ],
        out_specs=[tok() for _ in range(7)]
        + [pl.BlockSpec((cpt, 1, RWKV_W), lambda b, i: (b * nb + i, 0, 0))],
        scratch_shapes=[pltpu.VMEM((1, RW_GROUP), F32)],
        compiler_params=_cparams(("parallel", "arbitrary")),
        name="rwkv_prep",
    )(p, mu, w0, ww2, a0, wa2, wg2, k_k, k_a, r_k)


def _rwkv_core_kernel(rt_ref, bt_ref, at_ref, kt_ref, v_ref, bonus_ref, g_ref, ptot_ref,
                      lw_ref, lb_ref, o_ref, h_scr):
    @pl.when(pl.program_id(0) == 0)
    def _():
        h_scr[...] = jnp.zeros_like(h_scr)

    n_batch = rt_ref.shape[0]
    ri = lax.broadcasted_iota(jnp.int32, (PAIR, PAIR), 0)
    ci = lax.broadcasted_iota(jnp.int32, (PAIR, PAIR), 1)
    head_blk = (ri // RWKV_HD) == (ci // RWKV_HD)
    strict = ri > ci
    lower = ri >= ci
    eye = ri == ci
    blk8 = (ri // 8) == (ci // 8)
    eye_f = jnp.where(eye, 1.0, 0.0)
    ones_pair = jnp.where(head_blk, 1.0, 0.0).astype(BF16)
    ones_2 = jnp.concatenate([ones_pair, ones_pair], axis=0)
    zeros_b = jnp.zeros((PAIR, PAIR), BF16)

    def stack(x):
        return jnp.where(head_blk, jnp.concatenate([x, x], axis=0), jnp.zeros((), x.dtype))

    def dot_packed(lhs, rhs):
        out = []
        for j in range(0, len(lhs), 2):
            l2 = jnp.concatenate([lhs[j], lhs[j + 1]], axis=1)
            r2 = jnp.concatenate([jnp.concatenate([rhs[j], zeros_b], axis=1),
                                  jnp.concatenate([zeros_b, rhs[j + 1]], axis=1)], axis=0)
            o = _dot(l2, r2)
            out += [o[:, :PAIR], o[:, PAIR:]]
        return out

    def head_sum(x):
        hi = x.astype(BF16)
        lo = (x - hi.astype(F32)).astype(BF16)
        return _dot(jnp.concatenate([hi, lo], axis=1), ones_2)

    pairs = range(n_batch * N_PAIR)
    where = [(b, slice(j * PAIR, (j + 1) * PAIR)) for b in range(n_batch) for j in range(N_PAIR)]
    ptot = [ptot_ref[b, 0, :, l] for b, l in where]
    bx = [stack(bt_ref[b, :, l]) for b, l in where]
    rx = [stack(rt_ref[b, :, l]) for b, l in where]
    at = [at_ref[b, :, l] for b, l in where]
    kt = [kt_ref[b, :, l] for b, l in where]
    ak = [jnp.concatenate([stack(at[j]), stack(kt[j])], axis=0) for j in pairs]
    ak2 = [jnp.concatenate([stack((at[j].astype(F32) * ptot[j]).astype(BF16)),
                            stack((kt[j].astype(F32) * ptot[j]).astype(BF16))], axis=0) for j in pairs]
    vx = [stack(v_ref[b, :, l]) for b, l in where]

    gb = [_dot_nt(bx[j], ak[j]) for j in pairs]
    gr = [_dot_nt(rx[j], ak[j]) for j in pairs]
    a_ab = [jnp.where(strict, x[:, :PAIR], 0.0) for x in gb]
    a_kb = [jnp.where(strict, x[:, PAIR:], 0.0).astype(BF16) for x in gb]
    a_r = [jnp.concatenate([jnp.where(lower, x[:, :PAIR], 0.0),
                            jnp.where(lower, x[:, PAIR:], 0.0)], axis=1).astype(BF16) for x in gr]

    a_d = [jnp.where(blk8, x, 0.0) for x in a_ab]
    a_db = [x.astype(BF16) for x in a_d]
    pw = [x.astype(BF16) for x in dot_packed(a_db, a_db)]
    s = [eye_f + x for x in a_d]
    sp = [_dot(pw[j], jnp.concatenate([s[j].astype(BF16), pw[j]], axis=1)) for j in pairs]
    s = [s[j] + sp[j][:, :PAIR] for j in pairs]
    pw = [x[:, PAIR:].astype(BF16) for x in sp]
    ps = dot_packed(pw, [x.astype(BF16) for x in s])
    s = [s[j] + ps[j] for j in pairs]
    for width in (8, 16, 32):
        off = ((ri // (2 * width)) == (ci // (2 * width))) & ((ri // width) != (ci // width))
        e = [jnp.where(off, x, 0.0).astype(BF16) for x in a_ab]
        sb = [x.astype(BF16) for x in s]
        es = [x.astype(BF16) for x in dot_packed(e, sb)]
        ses = dot_packed(sb, es)
        s = [s[j] + ses[j] for j in pairs]
    t_inv = [x.astype(BF16) for x in s]

    kv = [x.astype(BF16) for x in dot_packed(a_kb, vx)]
    wu = [_dot(t_inv[j], jnp.concatenate([bx[j], kv[j]], axis=1)).astype(BF16) for j in pairs]
    z = [jnp.concatenate([wu[j], jnp.concatenate([zeros_b, vx[j]], axis=1)], axis=0) for j in pairs]
    mc = [_dot_tn(ak2[j], z[j]) for j in pairs]
    qy = [_dot(a_r[j], z[j]) for j in pairs]

    m = [(mc[j][:, :PAIR] + jnp.where(eye, ptot[j], 0.0)).astype(BF16) for j in pairs]
    q = [(qy[j][:, :PAIR] + rx[j].astype(F32)).astype(BF16) for j in pairs]
    hb = [h_scr[j].astype(BF16) for j in pairs]
    qh = dot_packed(q, hb)
    mh = dot_packed(m, hb)
    for j in pairs:
        h_scr[j] = mh[j] + mc[j][:, PAIR:]
    y = [qh[j] + qy[j][:, PAIR:] for j in pairs]
    y = [x[:RWKV_HD, :] + x[RWKV_HD:, :] for x in y]

    mean = [head_sum(x) * (1.0 / RWKV_HD) for x in y]
    yc = [y[j] - mean[j] for j in pairs]
    var = [head_sum(x * x) * (1.0 / RWKV_HD) for x in yc]
    for j in pairs:
        b, l = where[j]
        yn = yc[j] * lax.rsqrt(var[j] + GN_EPS) * lw_ref[:, l] + lb_ref[:, l]
        o_ref[b, :, l] = ((yn + bonus_ref[b, :, l]) * g_ref[b, :, l]).astype(BF16)


def _rwkv_core(rt, bt, at, kt, v, bonus, g, ptot, lnx_w, lnx_b, B, S):
    nc = S // CHUNK
    seq = lambda a: a.reshape(B, S, RWKV_W)
    tok = lambda: pl.BlockSpec((B, CHUNK, RWKV_W), lambda c: (0, c, 0))
    vec = lambda: pl.BlockSpec((1, RWKV_W), lambda c: (0, 0))
    out = pl.pallas_call(
        _rwkv_core_kernel,
        out_shape=jax.ShapeDtypeStruct((B, S, RWKV_W), BF16),
        grid=(nc,),
        in_specs=[tok() for _ in range(7)]
        + [pl.BlockSpec((B, 1, 1, RWKV_W), lambda c: (0, c, 0, 0)), vec(), vec()],
        out_specs=tok(),
        scratch_shapes=[pltpu.VMEM((B * N_PAIR, PAIR, PAIR), F32)],
        compiler_params=_cparams(("arbitrary",)),
        name="rwkv_core",
    )(seq(rt), seq(bt), seq(at), seq(kt), seq(v), seq(bonus), seq(g),
      ptot.reshape(B, nc, 1, RWKV_W), lnx_w, lnx_b)
    return out.reshape(B * S, RWKV_W)


def _merge_kernel(x_ref, og_ref, or_ref, gt_ref, wb1_ref, wb2_ref, wo_ref, o_ref):
    D = x_ref.shape[1]
    y_gla = _dot(og_ref[...], wb1_ref[...])
    y_rw = _dot(or_ref[...], wb2_ref[...])
    merged = gt_ref[:, :D] * y_gla + gt_ref[:, D:] * y_rw
    o_ref[...] = x_ref[...] + _dot(merged.astype(BF16), wo_ref[...])


def _merge(x, o_gla, o_rw, gates, w_branch, wo, *, tm=256):
    T, D = x.shape
    const = lambda shape, r: pl.BlockSpec(shape, lambda i: (r, 0), pipeline_mode=pl.Buffered(1))
    assert GLA_V == RWKV_W
    return pl.pallas_call(
        _merge_kernel,
        out_shape=jax.ShapeDtypeStruct((T, D), F32),
        grid=(T // tm,),
        in_specs=[
            pl.BlockSpec((tm, D), lambda i: (i, 0)),
            pl.BlockSpec((tm, GLA_V), lambda i: (i, 0)),
            pl.BlockSpec((tm, RWKV_W), lambda i: (i, 0)),
            pl.BlockSpec((tm, 2 * D), lambda i: (i, 0)),
            const((GLA_V, D), 0), const((RWKV_W, D), 1), const((D, D), 0),
        ],
        out_specs=pl.BlockSpec((tm, D), lambda i: (i, 0)),
        compiler_params=_cparams(("parallel",)),
        name="merge_out",
    )(x, o_gla, o_rw, gates, w_branch, w_branch, wo)


def _pad_rows(w, n):
    return jnp.pad(w, ((0, n - w.shape[0]), (0, 0)))


def kernel(x, ffn1_norm, ffn1_wg, ffn1_wu, ffn1_wd, mix_norm, w_in, gla_w_a2, gla_b_a, gla_gn_w, rwkv_mu,
           rwkv_w0, rwkv_w_w2, rwkv_a0, rwkv_w_a2, rwkv_w_g2, rwkv_k_k, rwkv_k_a, rwkv_r_k, rwkv_lnx_w,
           rwkv_lnx_b, gate_b, w_branch, w_out, ffn2_norm, ffn2_wg, ffn2_wu, ffn2_wd, final_norm):
    B, S, D = x.shape
    T = B * S
    depth = ffn1_norm.shape[0]
    assert depth >= 1
    bf = lambda w: w.astype(BF16)
    row = lambda v: v.reshape(1, -1)
    ffn_steps = (T // FFN_TM) * (ffn1_wg.shape[2] // FFN_TF)
    in_steps = (T // PROJ_TM) * ((RW_GROUP + GLA_GROUP) // PROJ_TN)
    gate_steps = (T // PROJ_TM) * (gate_b.shape[1] // PROJ_TN)
    hosted = lambda kind, w, steps: (kind, w, _rows_per_block(w.shape[0], steps))
    xt = x.reshape(T, D)
    for l in range(depth):
        last = l == depth - 1
        xt, w_a, w_gate, w_br, w_o = _ffn(
            xt, row(ffn1_norm[l]), bf(ffn1_wg[l]), bf(ffn1_wu[l]), bf(ffn1_wd[l]), row(final_norm),
            final_norm=False,
            aux=[hosted(PACK_IN, w_in[l], ffn_steps), hosted(CAST, w_branch[l], ffn_steps),
                 hosted(CAST, w_out[l], ffn_steps)])

        mu_row = jnp.pad(rwkv_mu[l], (GLA_IN, w_in.shape[2] - GLA_IN - RWKV_IN))[None, :]
        mu = _pack_in_cols(mu_row)[0][:, :RW_GROUP]
        p, wg2, wu2 = _norm_proj(
            xt, row(mix_norm[l]), w_a, jnp.zeros((1, w_a.shape[1]), F32), gate=False,
            aux=[hosted(CAST, ffn2_wg[l], in_steps), hosted(CAST, ffn2_wu[l], in_steps)])
        gates, wd2 = _norm_proj(xt, row(mix_norm[l]), w_gate, row(gate_b[l]), gate=True,
                                aux=[hosted(CAST, ffn2_wd[l], gate_steps)])

        o_gla = _gla(p, bf(_pad_rows(gla_w_a2[l], LANE)), row(gla_b_a[l]), row(gla_gn_w[l]), B, S)

        prep = _rwkv_prep(p, mu, row(rwkv_w0[l]), bf(_pad_rows(rwkv_w_w2[l], LANE)), row(rwkv_a0[l]),
                          bf(_pad_rows(rwkv_w_a2[l], LANE)), bf(rwkv_w_g2[l]), row(rwkv_k_k[l]),
                          row(rwkv_k_a[l]), row(rwkv_r_k[l]), B, S)
        o_rw = _rwkv_core(*prep, row(rwkv_lnx_w[l]), row(rwkv_lnx_b[l]), B, S)

        xt = _merge(xt, o_gla, o_rw, gates, w_br, w_o)

        xt = _ffn(xt, row(ffn2_norm[l]), wg2, wu2, wd2, row(final_norm), final_norm=last)[0]
    return xt.reshape(B, S, D)
```

```python
import functools

import jax
import jax.numpy as jnp
from jax import lax
from jax.experimental import pallas as pl
from jax.experimental.pallas import tpu as pltpu

F32 = jnp.float32
BF16 = jnp.bfloat16

NORM_EPS = 1e-6
GN_EPS = 64e-5
GLA_TAU = 16.0
CHUNK = 64

GLA_HEADS = 4
GLA_DK = 128
GLA_DV = 256
GLA_QK = GLA_HEADS * GLA_DK
GLA_V = GLA_HEADS * GLA_DV
GLA_LORA = 16

RWKV_HD = 64
RWKV_W = 1024
DECAY_LORA = 96
AAA_LORA = 96
GATE_LORA = 256

LANE = 128
PAIR = 2 * RWKV_HD
N_PAIR = RWKV_W // PAIR

RW_GROUP = 3 * RWKV_W + LANE + LANE + GATE_LORA
GLA_GROUP = RW_GROUP
RW_WD = 3 * RWKV_W
GLA_AD = 2 * GLA_QK + 2 * GLA_V
RW_AD = RW_WD + LANE
RW_GD = RW_AD + LANE

VMEM_LIMIT = 60 * 1024 * 1024
BF16_SUBLANES = 16

FFN_TM, FFN_TF = 1024, 512
PROJ_TM, PROJ_TN = 2048, 1024


def _rows_per_block(rows, steps):
    for rb in range(BF16_SUBLANES, rows + 1, BF16_SUBLANES):
        if rows % rb == 0 and rows // rb <= steps:
            return rb
    raise ValueError(f"cannot split {rows} rows over {steps} steps")


def _cparams(sem):
    return pltpu.CompilerParams(dimension_semantics=sem, vmem_limit_bytes=VMEM_LIMIT)


def _dot(a, b):
    return jnp.dot(a, b, preferred_element_type=F32)


def _dot_nt(a, b):
    return lax.dot_general(a, b, (((1,), (1,)), ((), ())), preferred_element_type=F32)


def _dot_tn(a, b):
    return lax.dot_general(a, b, (((0,), (0,)), ((), ())), preferred_element_type=F32)


def _dot_split(a, b_bf16):
    hi = a.astype(BF16)
    lo = (a - hi.astype(F32)).astype(BF16)
    return _dot(hi, b_bf16) + _dot(lo, b_bf16)


def _rmsnorm(x, g):
    return x * lax.rsqrt(jnp.mean(x * x, axis=-1, keepdims=True) + NORM_EPS) * g


def _sigmoid(z):
    return 1.0 / (1.0 + jnp.exp(-z))


def _softplus(z):
    return jnp.maximum(z, 0.0) + jnp.log(1.0 + jnp.exp(-jnp.abs(z)))


CAST = "cast"
PACK_T = "pack_t"

D_MODEL = 2048
GLA_IN = 2 * GLA_QK + 2 * GLA_V + GLA_LORA
RWKV_IN = 3 * RWKV_W + DECAY_LORA + AAA_LORA + GATE_LORA
GATE_COL = RW_GROUP + GLA_GROUP
PACKED_COLS = GATE_COL + 2 * D_MODEL

IN_PIECES = (
    (0, GLA_IN, 3 * RWKV_W),
    (RW_WD, GLA_IN + 3 * RWKV_W, DECAY_LORA),
    (RW_AD, GLA_IN + 3 * RWKV_W + DECAY_LORA, AAA_LORA),
    (RW_GD, GLA_IN + 3 * RWKV_W + DECAY_LORA + AAA_LORA, GATE_LORA),
    (RW_GROUP, 0, GLA_IN),
    (GATE_COL, GLA_IN + RWKV_IN, 2 * D_MODEL),
)


def _pack_cols(w):
    zeros = lambda n: jnp.zeros((w.shape[0], n), w.dtype)
    parts, pos = [], 0
    for start, w_start, width in IN_PIECES:
        parts += [zeros(start - pos), w[:, w_start:w_start + width]]
        pos = start + width
    assert pos == PACKED_COLS
    return jnp.concatenate([p for p in parts if p.shape[1]], axis=1)


def _pack_src(k):
    col = k * LANE
    src = jnp.zeros_like(col)
    valid = jnp.zeros_like(col)
    for start, w_start, width in IN_PIECES:
        inside = (col >= start) & (col < start + width)
        src = jnp.where(inside, w_start + col - start, src)
        valid = jnp.where(inside, jnp.minimum(start + width - col, LANE), valid)
    return src, valid


def _aux_plan(aux, nj, steps):
    in_specs, operands, out_specs, out_shapes, meta = [], [], [], [], []
    for kind, w in aux:
        rows, cols = w.shape
        if kind == CAST:
            rb = _rows_per_block(rows, steps)
            n_active = rows // rb
            index = lambda i, j, n=n_active: (jnp.minimum(i * nj + j, n - 1), 0)
            in_specs.append(pl.BlockSpec((rb, cols), index))
            operands.append(w)
            out_specs.append(pl.BlockSpec((rb, cols), index))
            out_shapes.append(jax.ShapeDtypeStruct((rows, cols), BF16))
            meta.append((kind, n_active, 1, 1))
        else:
            n_blocks = PACKED_COLS // LANE
            per_step = next(d for d in range(1, n_blocks + 1) if n_blocks % d == 0 and n_blocks // d <= steps)
            n_active = n_blocks // per_step
            last = n_active - 1
            for q in range(per_step):
                src = lambda i, j, q=q, last=last, per_step=per_step: (
                    pl.multiple_of(_pack_src(jnp.minimum(i * nj + j, last) * per_step + q)[0],
                                   BF16_SUBLANES), 0)
                in_specs.append(pl.BlockSpec((pl.Element(LANE), pl.Element(cols)), src))
                operands.append(w)
            out_specs.append(pl.BlockSpec((cols, per_step * LANE),
                                          lambda i, j, last=last: (0, jnp.minimum(i * nj + j, last))))
            out_shapes.append(jax.ShapeDtypeStruct((cols, PACKED_COLS), BF16))
            meta.append((kind, n_active, per_step, 1))
    return in_specs, operands, out_specs, out_shapes, tuple(meta)


def _aux_run(meta, in_refs, out_refs):
    step = pl.program_id(0) * pl.num_programs(1) + pl.program_id(1)
    ki = ko = 0
    for kind, n_active, n_in, n_out in meta:
        ins, outs = in_refs[ki:ki + n_in], out_refs[ko:ko + n_out]
        ki += n_in
        ko += n_out

        @pl.when(step < n_active)
        def _(kind=kind, ins=ins, outs=outs, n_in=n_in):
            if kind == CAST:
                outs[0][...] = ins[0][...].astype(BF16)
            else:
                for q, w_ref in enumerate(ins):
                    w = w_ref[...]
                    _, valid = _pack_src(step * n_in + q)
                    row = lax.broadcasted_iota(jnp.int32, w.shape, 0)
                    w = jnp.where(row < valid, w, 0.0)
                    outs[0][:, q * LANE:(q + 1) * LANE] = w.T.astype(BF16)


RESIDUAL = "residual"
FINAL = "final"
EMIT_NORM = "emit"


def _ffn_kernel(*refs, mode, aux_meta):
    x_ref, g_ref, wg_ref, wu_ref, wd_ref, fn_ref = refs[:6]
    n_aux = sum(m[2] for m in aux_meta)
    n_main_out = 2 if mode == EMIT_NORM else 1
    o_ref = refs[6 + n_aux]
    h_scr = refs[-1]
    _aux_run(aux_meta, refs[6:6 + n_aux], refs[6 + n_aux + n_main_out:-1])
    j = pl.program_id(1)

    @pl.when(j == 0)
    def _():
        h_scr[...] = _rmsnorm(x_ref[...], g_ref[...]).astype(BF16)
        o_ref[...] = jnp.zeros_like(o_ref)

    h = h_scr[...]
    a = _dot(h, wg_ref[...])
    u = _dot(h, wu_ref[...])
    act = (a * _sigmoid(a) * u).astype(BF16)
    o_ref[...] += _dot(act, wd_ref[...])

    @pl.when(j == pl.num_programs(1) - 1)
    def _():
        y = x_ref[...] + 0.5 * o_ref[...]
        if mode == FINAL:
            y = _rmsnorm(y, fn_ref[...])
        o_ref[...] = y
        if mode == EMIT_NORM:
            refs[7 + n_aux][...] = _rmsnorm(y, fn_ref[...]).astype(BF16)


def _ffn(x, g, wg, wu, wd, fn, *, mode, aux=(), tm=FFN_TM, tf=FFN_TF):
    T, D = x.shape
    FF = wg.shape[1]
    nj = FF // tf
    aux_in, aux_ops, aux_out, aux_shapes, aux_meta = _aux_plan(aux, nj, (T // tm) * nj)
    tok = lambda **kw: pl.BlockSpec((tm, D), lambda i, j: (i, 0), **kw)
    main_shapes = [jax.ShapeDtypeStruct((T, D), F32)]
    main_specs = [tok()]
    if mode == EMIT_NORM:
        main_shapes.append(jax.ShapeDtypeStruct((T, D), BF16))
        main_specs = [tok(pipeline_mode=pl.Buffered(1)), tok(pipeline_mode=pl.Buffered(1))]
    return pl.pallas_call(
        functools.partial(_ffn_kernel, mode=mode, aux_meta=aux_meta),
        out_shape=main_shapes + aux_shapes,
        grid=(T // tm, nj),
        in_specs=[
            tok(pipeline_mode=pl.Buffered(1)),
            pl.BlockSpec((1, D), lambda i, j: (0, 0)),
            pl.BlockSpec((D, tf), lambda i, j: (0, j)),
            pl.BlockSpec((D, tf), lambda i, j: (0, j)),
            pl.BlockSpec((tf, D), lambda i, j: (j, 0)),
            pl.BlockSpec((1, D), lambda i, j: (0, 0)),
        ] + aux_in,
        out_specs=main_specs + aux_out,
        scratch_shapes=[pltpu.VMEM((tm, D), BF16)],
        compiler_params=_cparams(("arbitrary", "arbitrary")),
        name="ffn_" + mode,
    )(x, g, wg, wu, wd, fn, *aux_ops)


def _proj_kernel(*refs, gate, aux_meta):
    h_ref, w_ref, b_ref = refs[:3]
    n_aux = sum(m[2] for m in aux_meta)
    o_ref = refs[3 + n_aux]
    _aux_run(aux_meta, refs[3:3 + n_aux], refs[4 + n_aux:])
    y = _dot(h_ref[...], w_ref[...])
    if gate:
        y = _sigmoid(y + b_ref[...])
    o_ref[...] = y


def _proj(h, w, col0, n_cols, b, *, gate, aux=(), tm=PROJ_TM, tn=PROJ_TN):
    T, D = h.shape
    nj = n_cols // tn
    j0 = col0 // tn
    aux_in, aux_ops, aux_out, aux_shapes, aux_meta = _aux_plan(aux, nj, (T // tm) * nj)
    return pl.pallas_call(
        functools.partial(_proj_kernel, gate=gate, aux_meta=aux_meta),
        out_shape=[jax.ShapeDtypeStruct((T, n_cols), F32)] + aux_shapes,
        grid=(T // tm, nj),
        in_specs=[
            pl.BlockSpec((tm, D), lambda i, j: (i, 0), pipeline_mode=pl.Buffered(1)),
            pl.BlockSpec((D, tn), lambda i, j: (0, j0 + j)),
            pl.BlockSpec((1, tn), lambda i, j: (0, j)),
        ] + aux_in,
        out_specs=[pl.BlockSpec((tm, tn), lambda i, j: (i, j))] + aux_out,
        compiler_params=_cparams(("arbitrary", "arbitrary")),
        name="gate_proj" if gate else "in_proj",
    )(h, w, b, *aux_ops)


def _chunk_tri(n):
    r = lax.broadcasted_iota(jnp.int32, (n, n), 0)
    c = lax.broadcasted_iota(jnp.int32, (n, n), 1)
    return jnp.where((r >= c) & (r // CHUNK == c // CHUNK), 1.0, 0.0).astype(BF16)


def _gla_kernel(p_ref, wa2_ref, ba_ref, gn_ref, o_ref, st_scr, *, tg):
    @pl.when(pl.program_id(1) == 0)
    def _():
        st_scr[...] = jnp.zeros_like(st_scr)

    gad = p_ref[:, GLA_AD:GLA_AD + LANE].astype(BF16)
    z = _dot(gad, wa2_ref[...]) + ba_ref[...]
    log_alpha = -_softplus(-z) * (1.0 / GLA_TAU)
    tri = _chunk_tri(tg)
    hi = log_alpha.astype(BF16)
    lo = (log_alpha - hi.astype(F32)).astype(BF16)
    cum = _dot(tri, hi) + _dot(tri, lo)
    gn = gn_ref[...]
    scale = GLA_DK ** -0.5

    n_chunks = tg // CHUNK
    rows = [slice(c * CHUNK, (c + 1) * CHUNK) for c in range(n_chunks)]
    keys = [slice(h * GLA_DK, (h + 1) * GLA_DK) for h in range(GLA_HEADS)]
    vals = [slice(h * GLA_DV, (h + 1) * GLA_DV) for h in range(GLA_HEADS)]
    units = [(c, h) for c in range(n_chunks) for h in range(GLA_HEADS)]
    tot = [cum[r, :][CHUNK - 1:CHUNK, :] for r in rows]
    etot = [jnp.exp(t) for t in tot]
    kdec = [(p_ref[rows[c], GLA_QK:2 * GLA_QK] * jnp.exp(tot[c] - cum[rows[c], :])).astype(BF16)
            for c in range(n_chunks)]
    q = [(p_ref[r, 0:GLA_QK] * scale).astype(BF16) for r in rows]
    inc = {}
    for c, h in units:
        v_h = p_ref[rows[c], 2 * GLA_QK + h * GLA_DV:2 * GLA_QK + (h + 1) * GLA_DV].astype(BF16)
        inc[c, h] = _dot_tn(v_h, kdec[c][:, keys[h]])
    st = {}
    for h in range(GLA_HEADS):
        prev = st_scr[h]
        for c in range(n_chunks):
            prev = prev * etot[c][:, keys[h]] + inc[c, h]
            st[c, h] = prev
        st_scr[h] = prev
    out = {u: _dot_nt(q[u[0]][:, keys[u[1]]], st[u].astype(BF16)) for u in units}
    for c, h in units:
        o = out[c, h]
        r_h = p_ref[rows[c], 2 * GLA_QK + GLA_V + h * GLA_DV:2 * GLA_QK + GLA_V + (h + 1) * GLA_DV]
        o = o * lax.rsqrt(jnp.mean(o * o, axis=-1, keepdims=True) + NORM_EPS) * gn
        o = o * (r_h * _sigmoid(r_h))
        o_ref[rows[c], vals[h]] = o.astype(BF16)


def _gla(p, wa2, ba, gn, B, S, *, tg=256):
    T = B * S
    nb = S // tg
    return pl.pallas_call(
        functools.partial(_gla_kernel, tg=tg),
        out_shape=jax.ShapeDtypeStruct((T, GLA_V), BF16),
        grid=(B, nb),
        in_specs=[
            pl.BlockSpec((tg, GLA_GROUP), lambda b, i: (b * nb + i, 1)),
            pl.BlockSpec((LANE, GLA_QK), lambda b, i: (0, 0)),
            pl.BlockSpec((1, GLA_QK), lambda b, i: (0, 0)),
            pl.BlockSpec((1, GLA_DV), lambda b, i: (0, 0)),
        ],
        out_specs=pl.BlockSpec((tg, GLA_V), lambda b, i: (b * nb + i, 0)),
        scratch_shapes=[pltpu.VMEM((GLA_HEADS, GLA_DV, GLA_DK), F32)],
        compiler_params=_cparams(("parallel", "arbitrary")),
        name="gla",
    )(p, wa2, ba, gn)


def _head_ones(n):
    r = lax.broadcasted_iota(jnp.int32, (n, n), 0)
    c = lax.broadcasted_iota(jnp.int32, (n, n), 1)
    return jnp.where(r // RWKV_HD == c // RWKV_HD, 1.0, 0.0).astype(BF16)


def _head_sum(x, ones_pair):
    parts = [_dot_split(x[:, j * PAIR:(j + 1) * PAIR], ones_pair) for j in range(x.shape[1] // PAIR)]
    return jnp.concatenate(parts, axis=1)


def _rwkv_prep_kernel(p_ref, mu_ref, w0_ref, ww2_ref, a0_ref, wa2_ref, wg2_ref, kk_ref, ka_ref, rk_ref,
                      rt_ref, bt_ref, at_ref, kt_ref, v_ref, bonus_ref, g_ref, ptot_ref,
                      carry_scr, *, tm):
    @pl.when(pl.program_id(1) == 0)
    def _():
        carry_scr[...] = jnp.zeros_like(carry_scr)

    p = p_ref[...]
    last = carry_scr[...]
    carry_scr[...] = p[tm - 1:tm, :]
    row = lax.broadcasted_iota(jnp.int32, p.shape, 0)
    prev = jnp.where(row == 0, last, pltpu.roll(p, 1, axis=0))
    p = p + mu_ref[...] * (prev - p)

    r = p[:, 0:RWKV_W]
    k = p[:, RWKV_W:2 * RWKV_W]
    v = p[:, 2 * RWKV_W:3 * RWKV_W]
    wd = p[:, RW_WD:RW_WD + LANE]
    ad = p[:, RW_AD:RW_AD + LANE]
    gd = p[:, RW_GD:RW_GD + GATE_LORA]

    w_raw = w0_ref[...] + _dot(jnp.tanh(wd).astype(BF16), ww2_ref[...])
    log_w = -jnp.exp(-_softplus(-w_raw) - 0.5)
    a = _sigmoid(a0_ref[...] + _dot(ad.astype(BF16), wa2_ref[...]))
    g_ref[...] = _dot(_sigmoid(gd).astype(BF16), wg2_ref[...])

    ones_pair = _head_ones(PAIR)
    kk = k * kk_ref[...]
    kk = kk / jnp.maximum(jnp.sqrt(_head_sum(kk * kk, ones_pair)), 1e-12)
    kp = k * (1.0 + (a - 1.0) * ka_ref[...])
    bonus_ref[...] = _head_sum(r * kp * rk_ref[...], ones_pair) * v
    v_ref[...] = v.astype(BF16)

    tri = _chunk_tri(tm)
    hi = log_w.astype(BF16)
    lo = (log_w - hi.astype(F32)).astype(BF16)
    cum = _dot(tri, hi) + _dot(tri, lo)
    nalpha = -(kk * a)
    for ci in range(tm // CHUNK):
        rows = slice(ci * CHUNK, (ci + 1) * CHUNK)
        cum_c = cum[rows, :]
        tot = cum_c[CHUNK - 1:CHUNK, :]
        e_inv = jnp.exp(-cum_c)
        rt_ref[rows, :] = (r[rows, :] * jnp.exp(cum_c)).astype(BF16)
        bt_ref[rows, :] = (kk[rows, :] * jnp.exp(cum_c - log_w[rows, :])).astype(BF16)
        at_ref[rows, :] = (nalpha[rows, :] * e_inv).astype(BF16)
        kt_ref[rows, :] = (kp[rows, :] * e_inv).astype(BF16)
        ptot_ref[ci] = jnp.exp(tot)


def _rwkv_prep(p, mu, w0, ww2, a0, wa2, wg2, k_k, k_a, r_k, B, S, *, tm=256):
    T = B * S
    nb = S // tm
    cpt = tm // CHUNK
    vec = lambda n: pl.BlockSpec((1, n), lambda b, i: (0, 0))
    tok = lambda: pl.BlockSpec((tm, RWKV_W), lambda b, i: (b * nb + i, 0))
    bf = jax.ShapeDtypeStruct((T, RWKV_W), BF16)
    f32 = jax.ShapeDtypeStruct((T, RWKV_W), F32)
    return pl.pallas_call(
        functools.partial(_rwkv_prep_kernel, tm=tm),
        out_shape=[bf, bf, bf, bf, bf, f32, f32,
                   jax.ShapeDtypeStruct((T // CHUNK, 1, RWKV_W), F32)],
        grid=(B, nb),
        in_specs=[
            pl.BlockSpec((tm, RW_GROUP), lambda b, i: (b * nb + i, 0)),
            vec(RW_GROUP), vec(RWKV_W),
            pl.BlockSpec((LANE, RWKV_W), lambda b, i: (0, 0)),
            vec(RWKV_W),
            pl.BlockSpec((LANE, RWKV_W), lambda b, i: (0, 0)),
            pl.BlockSpec((GATE_LORA, RWKV_W), lambda b, i: (0, 0)),
            vec(RWKV_W), vec(RWKV_W), vec(RWKV_W),
        ],
        out_specs=[tok() for _ in range(7)]
        + [pl.BlockSpec((cpt, 1, RWKV_W), lambda b, i: (b * nb + i, 0, 0))],
        scratch_shapes=[pltpu.VMEM((1, RW_GROUP), F32)],
        compiler_params=_cparams(("parallel", "arbitrary")),
        name="rwkv_prep",
    )(p, mu, w0, ww2, a0, wa2, wg2, k_k, k_a, r_k)


def _rwkv_core_kernel(rt_ref, bt_ref, at_ref, kt_ref, v_ref, bonus_ref, g_ref, ptot_ref,
                      lw_ref, lb_ref, o_ref, h_scr):
    @pl.when(pl.program_id(0) == 0)
    def _():
        h_scr[...] = jnp.zeros_like(h_scr)

    n_batch = rt_ref.shape[0]
    ri = lax.broadcasted_iota(jnp.int32, (PAIR, PAIR), 0)
    ci = lax.broadcasted_iota(jnp.int32, (PAIR, PAIR), 1)
    head_blk = (ri // RWKV_HD) == (ci // RWKV_HD)
    strict = ri > ci
    lower = ri >= ci
    eye = ri == ci
    blk8 = (ri // 8) == (ci // 8)
    eye_f = jnp.where(eye, 1.0, 0.0)
    ones_pair = jnp.where(head_blk, 1.0, 0.0).astype(BF16)
    ones_2 = jnp.concatenate([ones_pair, ones_pair], axis=0)
    zeros_b = jnp.zeros((PAIR, PAIR), BF16)

    def stack(x):
        return jnp.where(head_blk, jnp.concatenate([x, x], axis=0), jnp.zeros((), x.dtype))

    def dot_packed(lhs, rhs):
        out = []
        for j in range(0, len(lhs), 2):
            l2 = jnp.concatenate([lhs[j], lhs[j + 1]], axis=1)
            r2 = jnp.concatenate([jnp.concatenate([rhs[j], zeros_b], axis=1),
                                  jnp.concatenate([zeros_b, rhs[j + 1]], axis=1)], axis=0)
            o = _dot(l2, r2)
            out += [o[:, :PAIR], o[:, PAIR:]]
        return out

    def head_sum(x):
        hi = x.astype(BF16)
        lo = (x - hi.astype(F32)).astype(BF16)
        return _dot(jnp.concatenate([hi, lo], axis=1), ones_2)

    pairs = range(n_batch * N_PAIR)
    where = [(b, slice(j * PAIR, (j + 1) * PAIR)) for b in range(n_batch) for j in range(N_PAIR)]
    ptot = [ptot_ref[b, 0, :, l] for b, l in where]
    bx = [stack(bt_ref[b, :, l]) for b, l in where]
    rx = [stack(rt_ref[b, :, l]) for b, l in where]
    at = [at_ref[b, :, l] for b, l in where]
    kt = [kt_ref[b, :, l] for b, l in where]
    ak = [jnp.concatenate([stack(at[j]), stack(kt[j])], axis=0) for j in pairs]
    ak2 = [jnp.concatenate([stack((at[j].astype(F32) * ptot[j]).astype(BF16)),
                            stack((kt[j].astype(F32) * ptot[j]).astype(BF16))], axis=0) for j in pairs]
    vx = [stack(v_ref[b, :, l]) for b, l in where]

    gb = [_dot_nt(bx[j], ak[j]) for j in pairs]
    gr = [_dot_nt(rx[j], ak[j]) for j in pairs]
    a_ab = [jnp.where(strict, x[:, :PAIR], 0.0) for x in gb]
    a_kb = [jnp.where(strict, x[:, PAIR:], 0.0).astype(BF16) for x in gb]
    a_r = [jnp.concatenate([jnp.where(lower, x[:, :PAIR], 0.0),
                            jnp.where(lower, x[:, PAIR:], 0.0)], axis=1).astype(BF16) for x in gr]

    a_d = [jnp.where(blk8, x, 0.0) for x in a_ab]
    a_db = [x.astype(BF16) for x in a_d]
    pw = [x.astype(BF16) for x in dot_packed(a_db, a_db)]
    s = [eye_f + x for x in a_d]
    sp = [_dot(pw[j], jnp.concatenate([s[j].astype(BF16), pw[j]], axis=1)) for j in pairs]
    s = [s[j] + sp[j][:, :PAIR] for j in pairs]
    pw = [x[:, PAIR:].astype(BF16) for x in sp]
    ps = dot_packed(pw, [x.astype(BF16) for x in s])
    s = [s[j] + ps[j] for j in pairs]
    for width in (8, 16, 32):
        off = ((ri // (2 * width)) == (ci // (2 * width))) & ((ri // width) != (ci // width))
        e = [jnp.where(off, x, 0.0).astype(BF16) for x in a_ab]
        sb = [x.astype(BF16) for x in s]
        es = [x.astype(BF16) for x in dot_packed(e, sb)]
        ses = dot_packed(sb, es)
        s = [s[j] + ses[j] for j in pairs]
    t_inv = [x.astype(BF16) for x in s]

    kv = [x.astype(BF16) for x in dot_packed(a_kb, vx)]
    wu = [_dot(t_inv[j], jnp.concatenate([bx[j], kv[j]], axis=1)).astype(BF16) for j in pairs]
    z = [jnp.concatenate([wu[j], jnp.concatenate([zeros_b, vx[j]], axis=1)], axis=0) for j in pairs]
    mc = [_dot_tn(ak2[j], z[j]) for j in pairs]
    qy = [_dot(a_r[j], z[j]) for j in pairs]

    m = [(mc[j][:, :PAIR] + jnp.where(eye, ptot[j], 0.0)).astype(BF16) for j in pairs]
    q = [(qy[j][:, :PAIR] + rx[j].astype(F32)).astype(BF16) for j in pairs]
    hb = [h_scr[j].astype(BF16) for j in pairs]
    qh = dot_packed(q, hb)
    mh = dot_packed(m, hb)
    for j in pairs:
        h_scr[j] = mh[j] + mc[j][:, PAIR:]
    y = [qh[j] + qy[j][:, PAIR:] for j in pairs]
    y = [x[:RWKV_HD, :] + x[RWKV_HD:, :] for x in y]

    mean = [head_sum(x) * (1.0 / RWKV_HD) for x in y]
    yc = [y[j] - mean[j] for j in pairs]
    var = [head_sum(x * x) * (1.0 / RWKV_HD) for x in yc]
    for j in pairs:
        b, l = where[j]
        yn = yc[j] * lax.rsqrt(var[j] + GN_EPS) * lw_ref[:, l] + lb_ref[:, l]
        o_ref[b, :, l] = ((yn + bonus_ref[b, :, l]) * g_ref[b, :, l]).astype(BF16)


def _rwkv_core(rt, bt, at, kt, v, bonus, g, ptot, lnx_w, lnx_b, B, S):
    nc = S // CHUNK
    seq = lambda a: a.reshape(B, S, RWKV_W)
    tok = lambda: pl.BlockSpec((B, CHUNK, RWKV_W), lambda c: (0, c, 0))
    vec = lambda: pl.BlockSpec((1, RWKV_W), lambda c: (0, 0))
    out = pl.pallas_call(
        _rwkv_core_kernel,
        out_shape=jax.ShapeDtypeStruct((B, S, RWKV_W), BF16),
        grid=(nc,),
        in_specs=[tok() for _ in range(7)]
        + [pl.BlockSpec((B, 1, 1, RWKV_W), lambda c: (0, c, 0, 0)), vec(), vec()],
        out_specs=tok(),
        scratch_shapes=[pltpu.VMEM((B * N_PAIR, PAIR, PAIR), F32)],
        compiler_params=_cparams(("arbitrary",)),
        name="rwkv_core",
    )(seq(rt), seq(bt), seq(at), seq(kt), seq(v), seq(bonus), seq(g),
      ptot.reshape(B, nc, 1, RWKV_W), lnx_w, lnx_b)
    return out.reshape(B * S, RWKV_W)


def _merge_kernel(x_ref, og_ref, or_ref, gt_ref, wb1_ref, wb2_ref, wo_ref, o_ref):
    D = x_ref.shape[1]
    y_gla = _dot(og_ref[...], wb1_ref[...])
    y_rw = _dot(or_ref[...], wb2_ref[...])
    merged = gt_ref[:, :D] * y_gla + gt_ref[:, D:] * y_rw
    o_ref[...] = x_ref[...] + _dot(merged.astype(BF16), wo_ref[...])


def _merge(x, o_gla, o_rw, gates, w_branch, wo, *, tm=256):
    T, D = x.shape
    const = lambda shape, r: pl.BlockSpec(shape, lambda i: (r, 0), pipeline_mode=pl.Buffered(1))
    assert GLA_V == RWKV_W
    return pl.pallas_call(
        _merge_kernel,
        out_shape=jax.ShapeDtypeStruct((T, D), F32),
        grid=(T // tm,),
        in_specs=[
            pl.BlockSpec((tm, D), lambda i: (i, 0)),
            pl.BlockSpec((tm, GLA_V), lambda i: (i, 0)),
            pl.BlockSpec((tm, RWKV_W), lambda i: (i, 0)),
            pl.BlockSpec((tm, 2 * D), lambda i: (i, 0)),
            const((GLA_V, D), 0), const((RWKV_W, D), 1), const((D, D), 0),
        ],
        out_specs=pl.BlockSpec((tm, D), lambda i: (i, 0)),
        compiler_params=_cparams(("parallel",)),
        name="merge_out",
    )(x, o_gla, o_rw, gates, w_branch, w_branch, wo)


def _pad_rows(w, n):
    return jnp.pad(w, ((0, n - w.shape[0]), (0, 0)))


def kernel(x, ffn1_norm, ffn1_wg, ffn1_wu, ffn1_wd, mix_norm, w_in, gla_w_a2, gla_b_a, gla_gn_w, rwkv_mu,
           rwkv_w0, rwkv_w_w2, rwkv_a0, rwkv_w_a2, rwkv_w_g2, rwkv_k_k, rwkv_k_a, rwkv_r_k, rwkv_lnx_w,
           rwkv_lnx_b, gate_b, w_branch, w_out, ffn2_norm, ffn2_wg, ffn2_wu, ffn2_wd, final_norm):
    B, S, D = x.shape
    T = B * S
    depth = ffn1_norm.shape[0]
    assert depth >= 1
    bf = lambda w: w.astype(BF16)
    row = lambda v: v.reshape(1, -1)
    assert D == D_MODEL and w_in.shape[2] == GLA_IN + RWKV_IN + 2 * D_MODEL
    xt = x.reshape(T, D)
    for l in range(depth):
        last = l == depth - 1
        xt, h_mix, w_p, w_br, w_o = _ffn(
            xt, row(ffn1_norm[l]), bf(ffn1_wg[l]), bf(ffn1_wu[l]), bf(ffn1_wd[l]), row(mix_norm[l]),
            mode=EMIT_NORM,
            aux=[(PACK_T, w_in[l].T), (CAST, w_branch[l]), (CAST, w_out[l])])

        mu_row = jnp.pad(rwkv_mu[l], (GLA_IN, 2 * D_MODEL))[None, :]
        mu = _pack_cols(mu_row)[:, :RW_GROUP]
        p, wg2, wu2 = _proj(h_mix, w_p, 0, GATE_COL, jnp.zeros((1, GATE_COL), F32), gate=False,
                            aux=[(CAST, ffn2_wg[l]), (CAST, ffn2_wu[l])])
        gates, wd2 = _proj(h_mix, w_p, GATE_COL, 2 * D_MODEL, row(gate_b[l]), gate=True,
                           aux=[(CAST, ffn2_wd[l])])

        o_gla = _gla(p, bf(_pad_rows(gla_w_a2[l], LANE)), row(gla_b_a[l]), row(gla_gn_w[l]), B, S)

        prep = _rwkv_prep(p, mu, row(rwkv_w0[l]), bf(_pad_rows(rwkv_w_w2[l], LANE)), row(rwkv_a0[l]),
                          bf(_pad_rows(rwkv_w_a2[l], LANE)), bf(rwkv_w_g2[l]), row(rwkv_k_k[l]),
                          row(rwkv_k_a[l]), row(rwkv_r_k[l]), B, S)
        o_rw = _rwkv_core(*prep, row(rwkv_lnx_w[l]), row(rwkv_lnx_b[l]), B, S)

        xt = _merge(xt, o_gla, o_rw, gates, w_br, w_o)

        xt = _ffn(xt, row(ffn2_norm[l]), wg2, wu2, wd2, row(final_norm),
                  mode=FINAL if last else RESIDUAL)[0]
    return xt.reshape(B, S, D)
```

```python
import functools

import jax
import jax.numpy as jnp
from jax import lax
from jax.experimental import pallas as pl
from jax.experimental.pallas import tpu as pltpu

F32 = jnp.float32
BF16 = jnp.bfloat16

NORM_EPS = 1e-6
GN_EPS = 64e-5
GLA_TAU = 16.0
CHUNK = 64

GLA_HEADS = 4
GLA_DK = 128
GLA_DV = 256
GLA_QK = GLA_HEADS * GLA_DK
GLA_V = GLA_HEADS * GLA_DV
GLA_LORA = 16

RWKV_HD = 64
RWKV_W = 1024
DECAY_LORA = 96
AAA_LORA = 96
GATE_LORA = 256

LANE = 128
PAIR = 2 * RWKV_HD
N_PAIR = RWKV_W // PAIR

RW_GROUP = 3 * RWKV_W + LANE + LANE + GATE_LORA
GLA_GROUP = RW_GROUP
RW_WD = 3 * RWKV_W
GLA_AD = 2 * GLA_QK + 2 * GLA_V
RW_AD = RW_WD + LANE
RW_GD = RW_AD + LANE

VMEM_LIMIT = 60 * 1024 * 1024
BF16_SUBLANES = 16

FFN_TM, FFN_TF = 1024, 512
PROJ_TM, PROJ_TN = 2048, 1024


def _rows_per_block(rows, steps):
    for rb in range(BF16_SUBLANES, rows + 1, BF16_SUBLANES):
        if rows % rb == 0 and rows // rb <= steps:
            return rb
    raise ValueError(f"cannot split {rows} rows over {steps} steps")


def _cparams(sem):
    return pltpu.CompilerParams(dimension_semantics=sem, vmem_limit_bytes=VMEM_LIMIT)


def _dot(a, b):
    return jnp.dot(a, b, preferred_element_type=F32)


def _dot_nt(a, b):
    return lax.dot_general(a, b, (((1,), (1,)), ((), ())), preferred_element_type=F32)


def _dot_tn(a, b):
    return lax.dot_general(a, b, (((0,), (0,)), ((), ())), preferred_element_type=F32)


def _dot_split(a, b_bf16):
    hi = a.astype(BF16)
    lo = (a - hi.astype(F32)).astype(BF16)
    return _dot(hi, b_bf16) + _dot(lo, b_bf16)


def _rmsnorm(x, g):
    return x * lax.rsqrt(jnp.mean(x * x, axis=-1, keepdims=True) + NORM_EPS) * g


def _sigmoid(z):
    return 1.0 / (1.0 + jnp.exp(-z))


def _softplus(z):
    return jnp.maximum(z, 0.0) + jnp.log(1.0 + jnp.exp(-jnp.abs(z)))


CAST = "cast"
PACK_T = "pack_t"

D_MODEL = 2048
GLA_IN = 2 * GLA_QK + 2 * GLA_V + GLA_LORA
RWKV_IN = 3 * RWKV_W + DECAY_LORA + AAA_LORA + GATE_LORA
GATE_COL = RW_GROUP + GLA_GROUP
PACKED_COLS = GATE_COL + 2 * D_MODEL

IN_PIECES = (
    (0, GLA_IN, 3 * RWKV_W),
    (RW_WD, GLA_IN + 3 * RWKV_W, DECAY_LORA),
    (RW_AD, GLA_IN + 3 * RWKV_W + DECAY_LORA, AAA_LORA),
    (RW_GD, GLA_IN + 3 * RWKV_W + DECAY_LORA + AAA_LORA, GATE_LORA),
    (RW_GROUP, 0, GLA_IN),
    (GATE_COL, GLA_IN + RWKV_IN, 2 * D_MODEL),
)


def _pack_cols(w):
    zeros = lambda n: jnp.zeros((w.shape[0], n), w.dtype)
    parts, pos = [], 0
    for start, w_start, width in IN_PIECES:
        parts += [zeros(start - pos), w[:, w_start:w_start + width]]
        pos = start + width
    assert pos == PACKED_COLS
    return jnp.concatenate([p for p in parts if p.shape[1]], axis=1)


def _pack_src(k):
    col = k * LANE
    src = jnp.zeros_like(col)
    valid = jnp.zeros_like(col)
    for start, w_start, width in IN_PIECES:
        inside = (col >= start) & (col < start + width)
        src = jnp.where(inside, w_start + col - start, src)
        valid = jnp.where(inside, jnp.minimum(start + width - col, LANE), valid)
    return src, valid


def _aux_plan(aux, nj, steps):
    in_specs, operands, out_specs, out_shapes, meta = [], [], [], [], []
    for kind, w in aux:
        rows, cols = w.shape
        if kind == CAST:
            rb = _rows_per_block(rows, steps)
            n_active = rows // rb
            index = lambda i, j, n=n_active: (jnp.minimum(i * nj + j, n - 1), 0)
            in_specs.append(pl.BlockSpec((rb, cols), index))
            operands.append(w)
            out_specs.append(pl.BlockSpec((rb, cols), index))
            out_shapes.append(jax.ShapeDtypeStruct((rows, cols), BF16))
            meta.append((kind, n_active, 1, 1))
        else:
            n_blocks = PACKED_COLS // LANE
            per_step = next(d for d in range(1, n_blocks + 1) if n_blocks % d == 0 and n_blocks // d <= steps)
            n_active = n_blocks // per_step
            last = n_active - 1
            for q in range(per_step):
                src = lambda i, j, q=q, last=last, per_step=per_step: (
                    pl.multiple_of(_pack_src(jnp.minimum(i * nj + j, last) * per_step + q)[0],
                                   BF16_SUBLANES), 0)
                in_specs.append(pl.BlockSpec((pl.Element(LANE), pl.Element(cols)), src))
                operands.append(w)
            out_specs.append(pl.BlockSpec((cols, per_step * LANE),
                                          lambda i, j, last=last: (0, jnp.minimum(i * nj + j, last))))
            out_shapes.append(jax.ShapeDtypeStruct((cols, PACKED_COLS), BF16))
            meta.append((kind, n_active, per_step, 1))
    return in_specs, operands, out_specs, out_shapes, tuple(meta)


def _aux_run(meta, in_refs, out_refs):
    step = pl.program_id(0) * pl.num_programs(1) + pl.program_id(1)
    ki = ko = 0
    for kind, n_active, n_in, n_out in meta:
        ins, outs = in_refs[ki:ki + n_in], out_refs[ko:ko + n_out]
        ki += n_in
        ko += n_out

        @pl.when(step < n_active)
        def _(kind=kind, ins=ins, outs=outs, n_in=n_in):
            if kind == CAST:
                outs[0][...] = ins[0][...].astype(BF16)
            else:
                for q, w_ref in enumerate(ins):
                    w = w_ref[...]
                    _, valid = _pack_src(step * n_in + q)
                    row = lax.broadcasted_iota(jnp.int32, w.shape, 0)
                    w = jnp.where(row < valid, w, 0.0)
                    outs[0][:, q * LANE:(q + 1) * LANE] = w.T.astype(BF16)


RESIDUAL = "residual"
FINAL = "final"
EMIT_NORM = "emit"


def _ffn_kernel(*refs, mode, aux_meta):
    x_ref, g_ref, wg_ref, wu_ref, wd_ref, fn_ref = refs[:6]
    n_aux = sum(m[2] for m in aux_meta)
    n_main_out = 2 if mode == EMIT_NORM else 1
    o_ref = refs[6 + n_aux]
    h_scr = refs[-1]
    _aux_run(aux_meta, refs[6:6 + n_aux], refs[6 + n_aux + n_main_out:-1])
    j = pl.program_id(1)

    @pl.when(j == 0)
    def _():
        h_scr[...] = _rmsnorm(x_ref[...], g_ref[...]).astype(BF16)
        o_ref[...] = jnp.zeros_like(o_ref)

    h = h_scr[...]
    a = _dot(h, wg_ref[...])
    u = _dot(h, wu_ref[...])
    act = (a * _sigmoid(a) * u).astype(BF16)
    o_ref[...] += _dot(act, wd_ref[...])

    @pl.when(j == pl.num_programs(1) - 1)
    def _():
        y = x_ref[...] + 0.5 * o_ref[...]
        if mode == FINAL:
            y = _rmsnorm(y, fn_ref[...])
        o_ref[...] = y
        if mode == EMIT_NORM:
            refs[7 + n_aux][...] = _rmsnorm(y, fn_ref[...]).astype(BF16)


def _ffn(x, g, wg, wu, wd, fn, *, mode, aux=(), tm=FFN_TM, tf=FFN_TF):
    T, D = x.shape
    FF = wg.shape[1]
    nj = FF // tf
    aux_in, aux_ops, aux_out, aux_shapes, aux_meta = _aux_plan(aux, nj, (T // tm) * nj)
    tok = lambda **kw: pl.BlockSpec((tm, D), lambda i, j: (i, 0), **kw)
    main_shapes = [jax.ShapeDtypeStruct((T, D), F32)]
    main_specs = [tok()]
    if mode == EMIT_NORM:
        main_shapes.append(jax.ShapeDtypeStruct((T, D), BF16))
        main_specs = [tok(pipeline_mode=pl.Buffered(1)), tok(pipeline_mode=pl.Buffered(1))]
    return pl.pallas_call(
        functools.partial(_ffn_kernel, mode=mode, aux_meta=aux_meta),
        out_shape=main_shapes + aux_shapes,
        grid=(T // tm, nj),
        in_specs=[
            tok(pipeline_mode=pl.Buffered(1)),
            pl.BlockSpec((1, D), lambda i, j: (0, 0)),
            pl.BlockSpec((D, tf), lambda i, j: (0, j)),
            pl.BlockSpec((D, tf), lambda i, j: (0, j)),
            pl.BlockSpec((tf, D), lambda i, j: (j, 0)),
            pl.BlockSpec((1, D), lambda i, j: (0, 0)),
        ] + aux_in,
        out_specs=main_specs + aux_out,
        scratch_shapes=[pltpu.VMEM((tm, D), BF16)],
        compiler_params=_cparams(("arbitrary", "arbitrary")),
        name="ffn_" + mode,
    )(x, g, wg, wu, wd, fn, *aux_ops)


def _proj_kernel(*refs, gate, aux_meta):
    h_ref, w_ref, b_ref = refs[:3]
    n_aux = sum(m[2] for m in aux_meta)
    o_ref = refs[3 + n_aux]
    _aux_run(aux_meta, refs[3:3 + n_aux], refs[4 + n_aux:])
    y = _dot(h_ref[...], w_ref[...])
    if gate:
        y = _sigmoid(y + b_ref[...])
    o_ref[...] = y


def _proj(h, w, col0, n_cols, b, *, gate, aux=(), tm=PROJ_TM, tn=PROJ_TN):
    T, D = h.shape
    nj = n_cols // tn
    j0 = col0 // tn
    aux_in, aux_ops, aux_out, aux_shapes, aux_meta = _aux_plan(aux, nj, (T // tm) * nj)
    return pl.pallas_call(
        functools.partial(_proj_kernel, gate=gate, aux_meta=aux_meta),
        out_shape=[jax.ShapeDtypeStruct((T, n_cols), F32)] + aux_shapes,
        grid=(T // tm, nj),
        in_specs=[
            pl.BlockSpec((tm, D), lambda i, j: (i, 0), pipeline_mode=pl.Buffered(1)),
            pl.BlockSpec((D, tn), lambda i, j: (0, j0 + j)),
            pl.BlockSpec((1, tn), lambda i, j: (0, j)),
        ] + aux_in,
        out_specs=[pl.BlockSpec((tm, tn), lambda i, j: (i, j))] + aux_out,
        compiler_params=_cparams(("arbitrary", "arbitrary")),
        name="gate_proj" if gate else "in_proj",
    )(h, w, b, *aux_ops)


def _chunk_tri(n):
    r = lax.broadcasted_iota(jnp.int32, (n, n), 0)
    c = lax.broadcasted_iota(jnp.int32, (n, n), 1)
    return jnp.where((r >= c) & (r // CHUNK == c // CHUNK), 1.0, 0.0).astype(BF16)


def _gla_kernel(p_ref, wa2_ref, ba_ref, gn_ref, o_ref, st_scr, *, tg):
    @pl.when(pl.program_id(1) == 0)
    def _():
        st_scr[...] = jnp.zeros_like(st_scr)

    gad = p_ref[:, GLA_AD:GLA_AD + LANE].astype(BF16)
    z = _dot(gad, wa2_ref[...]) + ba_ref[...]
    log_alpha = -_softplus(-z) * (1.0 / GLA_TAU)
    tri = _chunk_tri(tg)
    hi = log_alpha.astype(BF16)
    lo = (log_alpha - hi.astype(F32)).astype(BF16)
    cum = _dot(tri, hi) + _dot(tri, lo)
    gn = gn_ref[...]
    scale = GLA_DK ** -0.5

    n_chunks = tg // CHUNK
    rows = [slice(c * CHUNK, (c + 1) * CHUNK) for c in range(n_chunks)]
    keys = [slice(h * GLA_DK, (h + 1) * GLA_DK) for h in range(GLA_HEADS)]
    vals = [slice(h * GLA_DV, (h + 1) * GLA_DV) for h in range(GLA_HEADS)]
    units = [(c, h) for c in range(n_chunks) for h in range(GLA_HEADS)]
    tot = [cum[r, :][CHUNK - 1:CHUNK, :] for r in rows]
    etot = [jnp.exp(t) for t in tot]
    kdec = [(p_ref[rows[c], GLA_QK:2 * GLA_QK] * jnp.exp(tot[c] - cum[rows[c], :])).astype(BF16)
            for c in range(n_chunks)]
    q = [(p_ref[r, 0:GLA_QK] * scale).astype(BF16) for r in rows]
    inc = {}
    for c, h in units:
        v_h = p_ref[rows[c], 2 * GLA_QK + h * GLA_DV:2 * GLA_QK + (h + 1) * GLA_DV].astype(BF16)
        inc[c, h] = _dot_tn(v_h, kdec[c][:, keys[h]])
    st = {}
    for h in range(GLA_HEADS):
        prev = st_scr[h]
        for c in range(n_chunks):
            prev = prev * etot[c][:, keys[h]] + inc[c, h]
            st[c, h] = prev
        st_scr[h] = prev
    out = {u: _dot_nt(q[u[0]][:, keys[u[1]]], st[u].astype(BF16)) for u in units}
    for c, h in units:
        o = out[c, h]
        r_h = p_ref[rows[c], 2 * GLA_QK + GLA_V + h * GLA_DV:2 * GLA_QK + GLA_V + (h + 1) * GLA_DV]
        o = o * lax.rsqrt(jnp.mean(o * o, axis=-1, keepdims=True) + NORM_EPS) * gn
        o = o * (r_h * _sigmoid(r_h))
        o_ref[rows[c], vals[h]] = o.astype(BF16)


def _gla(p, wa2, ba, gn, B, S, *, tg=256):
    T = B * S
    nb = S // tg
    return pl.pallas_call(
        functools.partial(_gla_kernel, tg=tg),
        out_shape=jax.ShapeDtypeStruct((T, GLA_V), BF16),
        grid=(B, nb),
        in_specs=[
            pl.BlockSpec((tg, GLA_GROUP), lambda b, i: (b * nb + i, 1)),
            pl.BlockSpec((LANE, GLA_QK), lambda b, i: (0, 0)),
            pl.BlockSpec((1, GLA_QK), lambda b, i: (0, 0)),
            pl.BlockSpec((1, GLA_DV), lambda b, i: (0, 0)),
        ],
        out_specs=pl.BlockSpec((tg, GLA_V), lambda b, i: (b * nb + i, 0)),
        scratch_shapes=[pltpu.VMEM((GLA_HEADS, GLA_DV, GLA_DK), F32)],
        compiler_params=_cparams(("parallel", "arbitrary")),
        name="gla",
    )(p, wa2, ba, gn)


def _head_ones(n):
    r = lax.broadcasted_iota(jnp.int32, (n, n), 0)
    c = lax.broadcasted_iota(jnp.int32, (n, n), 1)
    return jnp.where(r // RWKV_HD == c // RWKV_HD, 1.0, 0.0).astype(BF16)


def _head_sum(x, ones_pair):
    parts = [_dot_split(x[:, j * PAIR:(j + 1) * PAIR], ones_pair) for j in range(x.shape[1] // PAIR)]
    return jnp.concatenate(parts, axis=1)


def _rwkv_prep_kernel(p_ref, mu_ref, w0_ref, ww2_ref, a0_ref, wa2_ref, wg2_ref, kk_ref, ka_ref, rk_ref,
                      rt_ref, bt_ref, at_ref, kt_ref, v_ref, bonus_ref, g_ref, ptot_ref,
                      carry_scr, *, tm):
    @pl.when(pl.program_id(1) == 0)
    def _():
        carry_scr[...] = jnp.zeros_like(carry_scr)

    p = p_ref[...]
    last = carry_scr[...]
    carry_scr[...] = p[tm - 1:tm, :]
    row = lax.broadcasted_iota(jnp.int32, p.shape, 0)
    prev = jnp.where(row == 0, last, pltpu.roll(p, 1, axis=0))
    p = p + mu_ref[...] * (prev - p)

    r = p[:, 0:RWKV_W]
    k = p[:, RWKV_W:2 * RWKV_W]
    v = p[:, 2 * RWKV_W:3 * RWKV_W]
    wd = p[:, RW_WD:RW_WD + LANE]
    ad = p[:, RW_AD:RW_AD + LANE]
    gd = p[:, RW_GD:RW_GD + GATE_LORA]

    w_raw = w0_ref[...] + _dot(jnp.tanh(wd).astype(BF16), ww2_ref[...])
    log_w = -jnp.exp(-_softplus(-w_raw) - 0.5)
    a = _sigmoid(a0_ref[...] + _dot(ad.astype(BF16), wa2_ref[...]))
    g_ref[...] = _dot(_sigmoid(gd).astype(BF16), wg2_ref[...])

    ones_pair = _head_ones(PAIR)
    kk = k * kk_ref[...]
    kk = kk / jnp.maximum(jnp.sqrt(_head_sum(kk * kk, ones_pair)), 1e-12)
    kp = k * (1.0 + (a - 1.0) * ka_ref[...])
    bonus_ref[...] = _head_sum(r * kp * rk_ref[...], ones_pair) * v
    v_ref[...] = v.astype(BF16)

    tri = _chunk_tri(tm)
    hi = log_w.astype(BF16)
    lo = (log_w - hi.astype(F32)).astype(BF16)
    cum = _dot(tri, hi) + _dot(tri, lo)
    nalpha = -(kk * a)
    for ci in range(tm // CHUNK):
        rows = slice(ci * CHUNK, (ci + 1) * CHUNK)
        cum_c = cum[rows, :]
        tot = cum_c[CHUNK - 1:CHUNK, :]
        e_inv = jnp.exp(-cum_c)
        rt_ref[rows, :] = (r[rows, :] * jnp.exp(cum_c)).astype(BF16)
        bt_ref[rows, :] = (kk[rows, :] * jnp.exp(cum_c - log_w[rows, :])).astype(BF16)
        at_ref[rows, :] = (nalpha[rows, :] * e_inv).astype(BF16)
        kt_ref[rows, :] = (kp[rows, :] * e_inv).astype(BF16)
        ptot_ref[ci] = jnp.exp(tot)


def _rwkv_prep(p, mu, w0, ww2, a0, wa2, wg2, k_k, k_a, r_k, B, S, *, tm=256):
    T = B * S
    nb = S // tm
    cpt = tm // CHUNK
    vec = lambda n: pl.BlockSpec((1, n), lambda b, i: (0, 0))
    tok = lambda: pl.BlockSpec((tm, RWKV_W), lambda b, i: (b * nb + i, 0))
    bf = jax.ShapeDtypeStruct((T, RWKV_W), BF16)
    f32 = jax.ShapeDtypeStruct((T, RWKV_W), F32)
    return pl.pallas_call(
        functools.partial(_rwkv_prep_kernel, tm=tm),
        out_shape=[bf, bf, bf, bf, bf, f32, f32,
                   jax.ShapeDtypeStruct((T // CHUNK, 1, RWKV_W), F32)],
        grid=(B, nb),
        in_specs=[
            pl.BlockSpec((tm, RW_GROUP), lambda b, i: (b * nb + i, 0)),
            vec(RW_GROUP), vec(RWKV_W),
            pl.BlockSpec((LANE, RWKV_W), lambda b, i: (0, 0)),
            vec(RWKV_W),
            pl.BlockSpec((LANE, RWKV_W), lambda b, i: (0, 0)),
            pl.BlockSpec((GATE_LORA, RWKV_W), lambda b, i: (0, 0)),
            vec(RWKV_W), vec(RWKV_W), vec(RWKV_W),
        ],
        out_specs=[tok() for _ in range(7)]
        + [pl.BlockSpec((cpt, 1, RWKV_W), lambda b, i: (b * nb + i, 0, 0))],
        scratch_shapes=[pltpu.VMEM((1, RW_GROUP), F32)],
        compiler_params=_cparams(("parallel", "arbitrary")),
        name="rwkv_prep",
    )(p, mu, w0, ww2, a0, wa2, wg2, k_k, k_a, r_k)


def _rwkv_core_kernel(rt_ref, bt_ref, at_ref, kt_ref, v_ref, bonus_ref, g_ref, ptot_ref,
                      lw_ref, lb_ref, o_ref, h_scr):
    @pl.when(pl.program_id(0) == 0)
    def _():
        h_scr[...] = jnp.zeros_like(h_scr)

    assert CHUNK == RWKV_HD
    n_batch = rt_ref.shape[0]
    ti = lax.broadcasted_iota(jnp.int32, (CHUNK, PAIR), 0)
    si = lax.broadcasted_iota(jnp.int32, (CHUNK, PAIR), 1) % RWKV_HD
    strict = ti > si
    lower = ti >= si
    eye = ti == si
    blk8 = (ti // 8) == (si // 8)
    eye_f = jnp.where(eye, 1.0, 0.0)
    ri = lax.broadcasted_iota(jnp.int32, (PAIR, PAIR), 0)
    ci = lax.broadcasted_iota(jnp.int32, (PAIR, PAIR), 1)
    head_blk = (ri // RWKV_HD) == (ci // RWKV_HD)
    ones_pair = jnp.where(head_blk, 1.0, 0.0).astype(BF16)
    ones_2 = jnp.concatenate([ones_pair, ones_pair], axis=0)
    zeros_b = jnp.zeros((PAIR, PAIR), BF16)

    def bd(x):
        return jnp.where(head_blk, jnp.concatenate([x, x], axis=0), jnp.zeros((), x.dtype))

    def head_t(x):
        xt = x.T
        return jnp.concatenate([xt[:RWKV_HD], xt[RWKV_HD:]], axis=1)

    def dot_packed(lhs, rhs_bd):
        out = []
        for j in range(0, len(lhs), 2):
            l2 = jnp.concatenate([lhs[j], lhs[j + 1]], axis=1)
            r2 = jnp.concatenate([jnp.concatenate([rhs_bd[j], zeros_b], axis=1),
                                  jnp.concatenate([zeros_b, rhs_bd[j + 1]], axis=1)], axis=0)
            o = _dot(l2, r2)
            out += [o[:, :PAIR], o[:, PAIR:]]
        return out

    bf = lambda xs: [x.astype(BF16) for x in xs]

    def head_sums(xs):
        hi = [x.astype(BF16) for x in xs]
        lo = [(x - h.astype(F32)).astype(BF16) for x, h in zip(xs, hi)]
        rows = jnp.concatenate([jnp.concatenate([h, l], axis=1) for h, l in zip(hi, lo)], axis=0)
        sums = _dot(rows, ones_2)
        return [sums[j * CHUNK:(j + 1) * CHUNK] for j in range(len(xs))]

    pairs = range(n_batch * N_PAIR)
    where = [(b, slice(j * PAIR, (j + 1) * PAIR)) for b in range(n_batch) for j in range(N_PAIR)]
    ptot = [ptot_ref[b, 0, :, l] for b, l in where]
    bt = [bt_ref[b, :, l] for b, l in where]
    rt = [rt_ref[b, :, l] for b, l in where]
    at = [at_ref[b, :, l] for b, l in where]
    kt = [kt_ref[b, :, l] for b, l in where]
    bx = [bd(x) for x in bt]
    vx = [bd(v_ref[b, :, l]) for b, l in where]
    ak = [jnp.concatenate([bd(at[j]), bd(kt[j])], axis=0) for j in pairs]
    ak2t = [jnp.concatenate([head_t(at[j].astype(F32) * ptot[j]).astype(BF16),
                             head_t(kt[j].astype(F32) * ptot[j]).astype(BF16)], axis=1) for j in pairs]

    g = [_dot_nt(jnp.concatenate([bt[j], rt[j]], axis=0), ak[j]) for j in pairs]
    gb = [x[:CHUNK] for x in g]
    gr = [x[CHUNK:] for x in g]
    a_ab = [jnp.where(strict, x[:, :PAIR], 0.0) for x in gb]
    a_kb = [jnp.where(strict, x[:, PAIR:], 0.0).astype(BF16) for x in gb]
    a_r = [jnp.concatenate([jnp.where(lower, x[:, :PAIR], 0.0),
                            jnp.where(lower, x[:, PAIR:], 0.0)], axis=1).astype(BF16) for x in gr]

    a_d = [jnp.where(blk8, x, 0.0) for x in a_ab]
    a_db = bf(a_d)
    pw = bf(dot_packed(a_db, [bd(x) for x in a_db]))
    s = [eye_f + x for x in a_d]
    sp = [_dot(pw[j], jnp.concatenate([bd(s[j].astype(BF16)), bd(pw[j])], axis=1)) for j in pairs]
    s = [s[j] + sp[j][:, :PAIR] for j in pairs]
    pw = [x[:, PAIR:].astype(BF16) for x in sp]
    ps = dot_packed(pw, [bd(x) for x in bf(s)])
    s = [s[j] + ps[j] for j in pairs]
    for width in (8, 16, 32):
        off = ((ti // (2 * width)) == (si // (2 * width))) & ((ti // width) != (si // width))
        e = [jnp.where(off, x, 0.0).astype(BF16) for x in a_ab]
        sb = bf(s)
        es = bf(dot_packed(e, [bd(x) for x in sb]))
        ses = dot_packed(sb, [bd(x) for x in es])
        s = [s[j] + ses[j] for j in pairs]
    t_inv = bf(s)

    kv = bf(dot_packed(a_kb, vx))
    wu = bf([_dot(t_inv[j], jnp.concatenate([bx[j], bd(kv[j])], axis=1)) for j in pairs])
    z = [jnp.concatenate([jnp.concatenate([bd(wu[j][:, :PAIR]), bd(wu[j][:, PAIR:])], axis=1),
                          jnp.concatenate([zeros_b, vx[j]], axis=1)], axis=0) for j in pairs]
    mcqy = [_dot(jnp.concatenate([ak2t[j], a_r[j]], axis=0), z[j]) for j in pairs]
    mc = [x[:CHUNK] for x in mcqy]
    qy = [x[CHUNK:] for x in mcqy]

    m = [(mc[j][:, :PAIR] + jnp.where(eye, ptot[j], 0.0)).astype(BF16) for j in pairs]
    q = [(qy[j][:, :PAIR] + rt[j].astype(F32)).astype(BF16) for j in pairs]
    hb = [bd(h_scr[j].astype(BF16)) for j in pairs]
    qmh = dot_packed([jnp.concatenate([q[j], m[j]], axis=0) for j in pairs], hb)
    for j in pairs:
        h_scr[j] = qmh[j][CHUNK:] + mc[j][:, PAIR:]
    y = [qmh[j][:CHUNK] + qy[j][:, PAIR:] for j in pairs]

    mean = [x * (1.0 / RWKV_HD) for x in head_sums(y)]
    yc = [y[j] - mean[j] for j in pairs]
    var = [x * (1.0 / RWKV_HD) for x in head_sums([x * x for x in yc])]
    for j in pairs:
        b, l = where[j]
        yn = yc[j] * lax.rsqrt(var[j] + GN_EPS) * lw_ref[:, l] + lb_ref[:, l]
        o_ref[b, :, l] = ((yn + bonus_ref[b, :, l]) * g_ref[b, :, l]).astype(BF16)


def _rwkv_core(rt, bt, at, kt, v, bonus, g, ptot, lnx_w, lnx_b, B, S):
    nc = S // CHUNK
    seq = lambda a: a.reshape(B, S, RWKV_W)
    tok = lambda: pl.BlockSpec((B, CHUNK, RWKV_W), lambda c: (0, c, 0))
    vec = lambda: pl.BlockSpec((1, RWKV_W), lambda c: (0, 0))
    out = pl.pallas_call(
        _rwkv_core_kernel,
        out_shape=jax.ShapeDtypeStruct((B, S, RWKV_W), BF16),
        grid=(nc,),
        in_specs=[tok() for _ in range(7)]
        + [pl.BlockSpec((B, 1, 1, RWKV_W), lambda c: (0, c, 0, 0)), vec(), vec()],
        out_specs=tok(),
        scratch_shapes=[pltpu.VMEM((B * N_PAIR, RWKV_HD, PAIR), F32)],
        compiler_params=_cparams(("arbitrary",)),
        name="rwkv_core",
    )(seq(rt), seq(bt), seq(at), seq(kt), seq(v), seq(bonus), seq(g),
      ptot.reshape(B, nc, 1, RWKV_W), lnx_w, lnx_b)
    return out.reshape(B * S, RWKV_W)


def _merge_kernel(x_ref, og_ref, or_ref, gt_ref, wb1_ref, wb2_ref, wo_ref, o_ref):
    D = x_ref.shape[1]
    y_gla = _dot(og_ref[...], wb1_ref[...])
    y_rw = _dot(or_ref[...], wb2_ref[...])
    merged = gt_ref[:, :D] * y_gla + gt_ref[:, D:] * y_rw
    o_ref[...] = x_ref[...] + _dot(merged.astype(BF16), wo_ref[...])


def _merge(x, o_gla, o_rw, gates, w_branch, wo, *, tm=256):
    T, D = x.shape
    const = lambda shape, r: pl.BlockSpec(shape, lambda i: (r, 0), pipeline_mode=pl.Buffered(1))
    assert GLA_V == RWKV_W
    return pl.pallas_call(
        _merge_kernel,
        out_shape=jax.ShapeDtypeStruct((T, D), F32),
        grid=(T // tm,),
        in_specs=[
            pl.BlockSpec((tm, D), lambda i: (i, 0)),
            pl.BlockSpec((tm, GLA_V), lambda i: (i, 0)),
            pl.BlockSpec((tm, RWKV_W), lambda i: (i, 0)),
            pl.BlockSpec((tm, 2 * D), lambda i: (i, 0)),
            const((GLA_V, D), 0), const((RWKV_W, D), 1), const((D, D), 0),
        ],
        out_specs=pl.BlockSpec((tm, D), lambda i: (i, 0)),
        compiler_params=_cparams(("parallel",)),
        name="merge_out",
    )(x, o_gla, o_rw, gates, w_branch, w_branch, wo)


def _pad_rows(w, n):
    return jnp.pad(w, ((0, n - w.shape[0]), (0, 0)))


def kernel(x, ffn1_norm, ffn1_wg, ffn1_wu, ffn1_wd, mix_norm, w_in, gla_w_a2, gla_b_a, gla_gn_w, rwkv_mu,
           rwkv_w0, rwkv_w_w2, rwkv_a0, rwkv_w_a2, rwkv_w_g2, rwkv_k_k, rwkv_k_a, rwkv_r_k, rwkv_lnx_w,
           rwkv_lnx_b, gate_b, w_branch, w_out, ffn2_norm, ffn2_wg, ffn2_wu, ffn2_wd, final_norm):
    B, S, D = x.shape
    T = B * S
    depth = ffn1_norm.shape[0]
    assert depth >= 1
    bf = lambda w: w.astype(BF16)
    row = lambda v: v.reshape(1, -1)
    assert D == D_MODEL and w_in.shape[2] == GLA_IN + RWKV_IN + 2 * D_MODEL
    xt = x.reshape(T, D)
    for l in range(depth):
        last = l == depth - 1
        xt, h_mix, w_p, w_br, w_o = _ffn(
            xt, row(ffn1_norm[l]), bf(ffn1_wg[l]), bf(ffn1_wu[l]), bf(ffn1_wd[l]), row(mix_norm[l]),
            mode=EMIT_NORM,
            aux=[(PACK_T, w_in[l].T), (CAST, w_branch[l]), (CAST, w_out[l])])

        mu_row = jnp.pad(rwkv_mu[l], (GLA_IN, 2 * D_MODEL))[None, :]
        mu = _pack_cols(mu_row)[:, :RW_GROUP]
        p, wg2, wu2 = _proj(h_mix, w_p, 0, GATE_COL, jnp.zeros((1, GATE_COL), F32), gate=False,
                            aux=[(CAST, ffn2_wg[l]), (CAST, ffn2_wu[l])])
        gates, wd2 = _proj(h_mix, w_p, GATE_COL, 2 * D_MODEL, row(gate_b[l]), gate=True,
                           aux=[(CAST, ffn2_wd[l])])

        o_gla = _gla(p, bf(_pad_rows(gla_w_a2[l], LANE)), row(gla_b_a[l]), row(gla_gn_w[l]), B, S)

        prep = _rwkv_prep(p, mu, row(rwkv_w0[l]), bf(_pad_rows(rwkv_w_w2[l], LANE)), row(rwkv_a0[l]),
                          bf(_pad_rows(rwkv_w_a2[l], LANE)), bf(rwkv_w_g2[l]), row(rwkv_k_k[l]),
                          row(rwkv_k_a[l]), row(rwkv_r_k[l]), B, S)
        o_rw = _rwkv_core(*prep, row(rwkv_lnx_w[l]), row(rwkv_lnx_b[l]), B, S)

        xt = _merge(xt, o_gla, o_rw, gates, w_br, w_o)

        xt = _ffn(xt, row(ffn2_norm[l]), wg2, wu2, wd2, row(final_norm),
                  mode=FINAL if last else RESIDUAL)[0]
    return xt.reshape(B, S, D)
```

```python
import functools
import math

import jax
import jax.numpy as jnp
from jax import lax
from jax.experimental import pallas as pl
from jax.experimental.pallas import tpu as pltpu

F32 = jnp.float32
BF16 = jnp.bfloat16

NORM_EPS = 1e-6
GN_EPS = 64e-5
GLA_TAU = 16.0
CHUNK = 64

GLA_HEADS = 4
GLA_DK = 128
GLA_DV = 256
GLA_QK = GLA_HEADS * GLA_DK
GLA_V = GLA_HEADS * GLA_DV
GLA_LORA = 16

RWKV_HD = 64
RWKV_W = 1024
DECAY_LORA = 96
AAA_LORA = 96
GATE_LORA = 256

LANE = 128
PAIR = 2 * RWKV_HD
N_PAIR = RWKV_W // PAIR

RW_GROUP = 3 * RWKV_W + LANE + LANE + GATE_LORA
GLA_GROUP = RW_GROUP
RW_WD = 3 * RWKV_W
GLA_AD = 2 * GLA_QK + 2 * GLA_V
RW_AD = RW_WD + LANE
RW_GD = RW_AD + LANE

VMEM_LIMIT = 60 * 1024 * 1024
BF16_SUBLANES = 16

FFN_TM, FFN_TF = 1024, 512
PROJ_TM, PROJ_TN = 2048, 1024


def _rows_per_block(rows, steps):
    for rb in range(BF16_SUBLANES, rows + 1, BF16_SUBLANES):
        if rows % rb == 0 and rows // rb <= steps:
            return rb
    raise ValueError(f"cannot split {rows} rows over {steps} steps")


def _cparams(sem):
    return pltpu.CompilerParams(dimension_semantics=sem, vmem_limit_bytes=VMEM_LIMIT)


def _dot(a, b):
    return jnp.dot(a, b, preferred_element_type=F32)


def _dot_nt(a, b):
    return lax.dot_general(a, b, (((1,), (1,)), ((), ())), preferred_element_type=F32)


def _dot_tn(a, b):
    return lax.dot_general(a, b, (((0,), (0,)), ((), ())), preferred_element_type=F32)


def _rmsnorm(x, g):
    return x * lax.rsqrt(jnp.mean(x * x, axis=-1, keepdims=True) + NORM_EPS) * g


def _sigmoid(z):
    return 0.5 + 0.5 * jnp.tanh(0.5 * z)


def _softplus(z):
    return jnp.maximum(z, 0.0) + jnp.log(1.0 + jnp.exp(-jnp.abs(z)))


CAST = "cast"
PACK_T = "pack_t"

D_MODEL = 2048
GLA_IN = 2 * GLA_QK + 2 * GLA_V + GLA_LORA
RWKV_IN = 3 * RWKV_W + DECAY_LORA + AAA_LORA + GATE_LORA
GATE_COL = RW_GROUP + GLA_GROUP
PACKED_COLS = GATE_COL + 2 * D_MODEL

IN_PIECES = (
    (0, GLA_IN, 3 * RWKV_W),
    (RW_WD, GLA_IN + 3 * RWKV_W, DECAY_LORA),
    (RW_AD, GLA_IN + 3 * RWKV_W + DECAY_LORA, AAA_LORA),
    (RW_GD, GLA_IN + 3 * RWKV_W + DECAY_LORA + AAA_LORA, GATE_LORA),
    (RW_GROUP, 0, GLA_IN),
    (GATE_COL, GLA_IN + RWKV_IN, 2 * D_MODEL),
)


def _pack_cols(w):
    zeros = lambda n: jnp.zeros((w.shape[0], n), w.dtype)
    parts, pos = [], 0
    for start, w_start, width in IN_PIECES:
        parts += [zeros(start - pos), w[:, w_start:w_start + width]]
        pos = start + width
    assert pos == PACKED_COLS
    return jnp.concatenate([p for p in parts if p.shape[1]], axis=1)


def _pack_src(k):
    col = k * LANE
    src = jnp.zeros_like(col)
    valid = jnp.zeros_like(col)
    for start, w_start, width in IN_PIECES:
        inside = (col >= start) & (col < start + width)
        src = jnp.where(inside, w_start + col - start, src)
        valid = jnp.where(inside, jnp.minimum(start + width - col, LANE), valid)
    return src, valid


def _aux_plan(aux, nj, steps):
    in_specs, operands, out_specs, out_shapes, meta = [], [], [], [], []
    for kind, w in aux:
        rows, cols = w.shape
        if kind == CAST:
            rb = _rows_per_block(rows, steps)
            n_active = rows // rb
            index = lambda i, j, n=n_active: (jnp.minimum(i * nj + j, n - 1), 0)
            in_specs.append(pl.BlockSpec((rb, cols), index))
            operands.append(w)
            out_specs.append(pl.BlockSpec((rb, cols), index))
            out_shapes.append(jax.ShapeDtypeStruct((rows, cols), BF16))
            meta.append((kind, n_active, 1, 1))
        else:
            n_blocks = PACKED_COLS // LANE
            per_step = next(d for d in range(1, n_blocks + 1) if n_blocks % d == 0 and n_blocks // d <= steps)
            n_active = n_blocks // per_step
            last = n_active - 1
            for q in range(per_step):
                src = lambda i, j, q=q, last=last, per_step=per_step: (
                    pl.multiple_of(_pack_src(jnp.minimum(i * nj + j, last) * per_step + q)[0],
                                   BF16_SUBLANES), 0)
                in_specs.append(pl.BlockSpec((pl.Element(LANE), pl.Element(cols)), src))
                operands.append(w)
            out_specs.append(pl.BlockSpec((cols, per_step * LANE),
                                          lambda i, j, last=last: (0, jnp.minimum(i * nj + j, last))))
            out_shapes.append(jax.ShapeDtypeStruct((cols, PACKED_COLS), BF16))
            meta.append((kind, n_active, per_step, 1))
    return in_specs, operands, out_specs, out_shapes, tuple(meta)


def _aux_run(meta, in_refs, out_refs):
    step = pl.program_id(0) * pl.num_programs(1) + pl.program_id(1)
    ki = ko = 0
    for kind, n_active, n_in, n_out in meta:
        ins, outs = in_refs[ki:ki + n_in], out_refs[ko:ko + n_out]
        ki += n_in
        ko += n_out

        @pl.when(step < n_active)
        def _(kind=kind, ins=ins, outs=outs, n_in=n_in):
            if kind == CAST:
                outs[0][...] = ins[0][...].astype(BF16)
            else:
                for q, w_ref in enumerate(ins):
                    w = w_ref[...]
                    _, valid = _pack_src(step * n_in + q)
                    row = lax.broadcasted_iota(jnp.int32, w.shape, 0)
                    w = jnp.where(row < valid, w, 0.0)
                    outs[0][:, q * LANE:(q + 1) * LANE] = w.T.astype(BF16)


RESIDUAL = "residual"
FINAL = "final"
EMIT_NORM = "emit"


def _ffn_kernel(*refs, mode, aux_meta):
    x_ref, g_ref, wg_ref, wu_ref, wd_ref, fn_ref = refs[:6]
    n_aux = sum(m[2] for m in aux_meta)
    n_main_out = 2 if mode == EMIT_NORM else 1
    o_ref = refs[6 + n_aux]
    h_scr = refs[-1]
    _aux_run(aux_meta, refs[6:6 + n_aux], refs[6 + n_aux + n_main_out:-1])
    j = pl.program_id(1)

    @pl.when(j == 0)
    def _():
        h_scr[...] = _rmsnorm(x_ref[...], g_ref[...]).astype(BF16)
        o_ref[...] = jnp.zeros_like(o_ref)

    h = h_scr[...]
    a = _dot(h, wg_ref[...])
    u = _dot(h, wu_ref[...])
    act = (a * _sigmoid(a) * u).astype(BF16)
    o_ref[...] += _dot(act, wd_ref[...])

    @pl.when(j == pl.num_programs(1) - 1)
    def _():
        y = x_ref[...] + 0.5 * o_ref[...]
        if mode == FINAL:
            y = _rmsnorm(y, fn_ref[...])
        o_ref[...] = y
        if mode == EMIT_NORM:
            refs[7 + n_aux][...] = _rmsnorm(y, fn_ref[...]).astype(BF16)


def _ffn(x, g, wg, wu, wd, fn, *, mode, aux=(), tm=FFN_TM, tf=FFN_TF):
    T, D = x.shape
    FF = wg.shape[1]
    nj = FF // tf
    aux_in, aux_ops, aux_out, aux_shapes, aux_meta = _aux_plan(aux, nj, (T // tm) * nj)
    tok = lambda **kw: pl.BlockSpec((tm, D), lambda i, j: (i, 0), **kw)
    main_shapes = [jax.ShapeDtypeStruct((T, D), F32)]
    main_specs = [tok()]
    if mode == EMIT_NORM:
        main_shapes.append(jax.ShapeDtypeStruct((T, D), BF16))
        main_specs = [tok(pipeline_mode=pl.Buffered(1)), tok(pipeline_mode=pl.Buffered(1))]
    return pl.pallas_call(
        functools.partial(_ffn_kernel, mode=mode, aux_meta=aux_meta),
        out_shape=main_shapes + aux_shapes,
        grid=(T // tm, nj),
        in_specs=[
            tok(pipeline_mode=pl.Buffered(1)),
            pl.BlockSpec((1, D), lambda i, j: (0, 0)),
            pl.BlockSpec((D, tf), lambda i, j: (0, j)),
            pl.BlockSpec((D, tf), lambda i, j: (0, j)),
            pl.BlockSpec((tf, D), lambda i, j: (j, 0)),
            pl.BlockSpec((1, D), lambda i, j: (0, 0)),
        ] + aux_in,
        out_specs=main_specs + aux_out,
        scratch_shapes=[pltpu.VMEM((tm, D), BF16)],
        compiler_params=_cparams(("arbitrary", "arbitrary")),
        name="ffn_" + mode,
    )(x, g, wg, wu, wd, fn, *aux_ops)


def _proj_kernel(*refs, aux_meta):
    h_ref, w_ref = refs[:2]
    n_aux = sum(m[2] for m in aux_meta)
    _aux_run(aux_meta, refs[2:2 + n_aux], refs[3 + n_aux:])
    refs[2 + n_aux][...] = _dot(h_ref[...], w_ref[...])


def _proj(h, w, *, aux=(), tm=PROJ_TM, tn=PROJ_TN):
    T, D = h.shape
    N = w.shape[1]
    nj = N // tn
    aux_in, aux_ops, aux_out, aux_shapes, aux_meta = _aux_plan(aux, nj, (T // tm) * nj)
    return pl.pallas_call(
        functools.partial(_proj_kernel, aux_meta=aux_meta),
        out_shape=[jax.ShapeDtypeStruct((T, N), F32)] + aux_shapes,
        grid=(T // tm, nj),
        in_specs=[
            pl.BlockSpec((tm, D), lambda i, j: (i, 0), pipeline_mode=pl.Buffered(1)),
            pl.BlockSpec((D, tn), lambda i, j: (0, j)),
        ] + aux_in,
        out_specs=[pl.BlockSpec((tm, tn), lambda i, j: (i, j))] + aux_out,
        compiler_params=_cparams(("arbitrary", "arbitrary")),
        name="in_proj",
    )(h, w, *aux_ops)


def _chunk_tri(n):
    r = lax.broadcasted_iota(jnp.int32, (n, n), 0)
    c = lax.broadcasted_iota(jnp.int32, (n, n), 1)
    return jnp.where((r >= c) & (r // CHUNK == c // CHUNK), 1.0, 0.0).astype(BF16)


def _gla_kernel(p_ref, wa2_ref, ba_ref, gn_ref, o_ref, st_scr, *, tg):
    @pl.when(pl.program_id(1) == 0)
    def _():
        st_scr[...] = jnp.zeros_like(st_scr)

    gad = p_ref[:, GLA_AD:GLA_AD + LANE].astype(BF16)
    z = _dot(gad, wa2_ref[...]) + ba_ref[...]
    log_alpha = -_softplus(-z) * (1.0 / GLA_TAU)
    tri = _chunk_tri(tg)
    hi = log_alpha.astype(BF16)
    lo = (log_alpha - hi.astype(F32)).astype(BF16)
    cum = _dot(tri, hi) + _dot(tri, lo)
    gn = gn_ref[...]
    scale = GLA_DK ** -0.5

    n_chunks = tg // CHUNK
    rows = [slice(c * CHUNK, (c + 1) * CHUNK) for c in range(n_chunks)]
    keys = [slice(h * GLA_DK, (h + 1) * GLA_DK) for h in range(GLA_HEADS)]
    vals = [slice(h * GLA_DV, (h + 1) * GLA_DV) for h in range(GLA_HEADS)]
    units = [(c, h) for c in range(n_chunks) for h in range(GLA_HEADS)]
    tot = [cum[r, :][CHUNK - 1:CHUNK, :] for r in rows]
    etot = [jnp.exp(t) for t in tot]
    kdec = [(p_ref[rows[c], GLA_QK:2 * GLA_QK] * jnp.exp(tot[c] - cum[rows[c], :])).astype(BF16)
            for c in range(n_chunks)]
    q = [(p_ref[r, 0:GLA_QK] * scale).astype(BF16) for r in rows]
    inc = {}
    for c, h in units:
        v_h = p_ref[rows[c], 2 * GLA_QK + h * GLA_DV:2 * GLA_QK + (h + 1) * GLA_DV].astype(BF16)
        inc[c, h] = _dot_tn(v_h, kdec[c][:, keys[h]])
    st = {}
    for h in range(GLA_HEADS):
        prev = st_scr[h]
        for c in range(n_chunks):
            prev = prev * etot[c][:, keys[h]] + inc[c, h]
            st[c, h] = prev
        st_scr[h] = prev
    out = {u: _dot_nt(q[u[0]][:, keys[u[1]]], st[u].astype(BF16)) for u in units}
    for c, h in units:
        o = out[c, h]
        r_h = p_ref[rows[c], 2 * GLA_QK + GLA_V + h * GLA_DV:2 * GLA_QK + GLA_V + (h + 1) * GLA_DV]
        o = o * lax.rsqrt(jnp.mean(o * o, axis=-1, keepdims=True) + NORM_EPS) * gn
        o = o * (r_h * _sigmoid(r_h))
        o_ref[rows[c], vals[h]] = o.astype(BF16)


def _gla(p, wa2, ba, gn, B, S, *, tg=256):
    T = B * S
    nb = S // tg
    return pl.pallas_call(
        functools.partial(_gla_kernel, tg=tg),
        out_shape=jax.ShapeDtypeStruct((T, GLA_V), BF16),
        grid=(B, nb),
        in_specs=[
            pl.BlockSpec((tg, GLA_GROUP), lambda b, i: (b * nb + i, 1)),
            pl.BlockSpec((LANE, GLA_QK), lambda b, i: (0, 0)),
            pl.BlockSpec((1, GLA_QK), lambda b, i: (0, 0)),
            pl.BlockSpec((1, GLA_DV), lambda b, i: (0, 0)),
        ],
        out_specs=pl.BlockSpec((tg, GLA_V), lambda b, i: (b * nb + i, 0)),
        scratch_shapes=[pltpu.VMEM((GLA_HEADS, GLA_DV, GLA_DK), F32)],
        compiler_params=_cparams(("parallel", "arbitrary")),
        name="gla",
    )(p, wa2, ba, gn)


MXU_DIM = 256


def _head_ones(n):
    r = lax.broadcasted_iota(jnp.int32, (n, n), 0)
    c = lax.broadcasted_iota(jnp.int32, (n, n), 1)
    return jnp.where(r // RWKV_HD == c // RWKV_HD, 1.0, 0.0).astype(BF16)


def _head_sum(x, ones_blk):
    xb = x.astype(BF16)
    parts = [_dot(xb[:, j:j + MXU_DIM], ones_blk) for j in range(0, x.shape[1], MXU_DIM)]
    return jnp.concatenate(parts, axis=1)


def _rwkv_prep_kernel(p_ref, mu_ref, w0_ref, ww2_ref, a0_ref, wa2_ref, wg2_ref, kk_ref, ka_ref, rk_ref,
                      rt_ref, bt_ref, at_ref, kt_ref, v_ref, bonus_ref, g_ref, ptot_ref,
                      carry_scr, *, tm):
    @pl.when(pl.program_id(1) == 0)
    def _():
        carry_scr[...] = jnp.zeros_like(carry_scr)

    p = p_ref[...]
    last = carry_scr[...]
    carry_scr[...] = p[tm - 1:tm, :]
    prev = pltpu.roll(p, 1, axis=0)
    sub = 8
    row = lax.broadcasted_iota(jnp.int32, (sub, p.shape[1]), 0)
    prev = jnp.concatenate([jnp.where(row == 0, last, prev[:sub]), prev[sub:]], axis=0)
    p = p + mu_ref[...] * (prev - p)

    r = p[:, 0:RWKV_W]
    k = p[:, RWKV_W:2 * RWKV_W]
    v = p[:, 2 * RWKV_W:3 * RWKV_W]
    wd = p[:, RW_WD:RW_WD + LANE]
    ad = p[:, RW_AD:RW_AD + LANE]
    gd = p[:, RW_GD:RW_GD + GATE_LORA]

    w_raw = w0_ref[...] + _dot(jnp.tanh(wd).astype(BF16), ww2_ref[...])
    log_w = (-math.exp(-0.5)) * _sigmoid(w_raw)
    a = _sigmoid(a0_ref[...] + _dot(ad.astype(BF16), wa2_ref[...]))
    g_ref[...] = _dot(_sigmoid(gd).astype(BF16), wg2_ref[...])

    ones_blk = _head_ones(MXU_DIM)
    kk = k * kk_ref[...]
    kk = kk * lax.rsqrt(jnp.maximum(_head_sum(kk * kk, ones_blk), 1e-24))
    kp = k * (1.0 + (a - 1.0) * ka_ref[...])
    bonus_ref[...] = _head_sum(r * kp * rk_ref[...], ones_blk) * v
    v_ref[...] = v.astype(BF16)

    tri = _chunk_tri(tm)
    hi = log_w.astype(BF16)
    lo = (log_w - hi.astype(F32)).astype(BF16)
    cum = _dot(tri, hi) + _dot(tri, lo)
    nalpha = -(kk * a)
    for ci in range(tm // CHUNK):
        rows = slice(ci * CHUNK, (ci + 1) * CHUNK)
        cum_c = cum[rows, :]
        tot = cum_c[CHUNK - 1:CHUNK, :]
        e_inv = jnp.exp(-cum_c)
        rt_ref[rows, :] = (r[rows, :] * jnp.exp(cum_c)).astype(BF16)
        bt_ref[rows, :] = (kk[rows, :] * jnp.exp(cum_c - log_w[rows, :])).astype(BF16)
        at_ref[rows, :] = (nalpha[rows, :] * e_inv).astype(BF16)
        kt_ref[rows, :] = (kp[rows, :] * e_inv).astype(BF16)
        ptot_ref[ci] = jnp.exp(tot)


def _rwkv_prep(p, mu, w0, ww2, a0, wa2, wg2, k_k, k_a, r_k, B, S, *, tm=256):
    T = B * S
    nb = S // tm
    cpt = tm // CHUNK
    vec = lambda n: pl.BlockSpec((1, n), lambda b, i: (0, 0))
    tok = lambda: pl.BlockSpec((tm, RWKV_W), lambda b, i: (b * nb + i, 0))
    bf = jax.ShapeDtypeStruct((T, RWKV_W), BF16)
    f32 = jax.ShapeDtypeStruct((T, RWKV_W), F32)
    return pl.pallas_call(
        functools.partial(_rwkv_prep_kernel, tm=tm),
        out_shape=[bf, bf, bf, bf, bf, f32, f32,
                   jax.ShapeDtypeStruct((T // CHUNK, 1, RWKV_W), F32)],
        grid=(B, nb),
        in_specs=[
            pl.BlockSpec((tm, RW_GROUP), lambda b, i: (b * nb + i, 0)),
            vec(RW_GROUP), vec(RWKV_W),
            pl.BlockSpec((LANE, RWKV_W), lambda b, i: (0, 0)),
            vec(RWKV_W),
            pl.BlockSpec((LANE, RWKV_W), lambda b, i: (0, 0)),
            pl.BlockSpec((GATE_LORA, RWKV_W), lambda b, i: (0, 0)),
            vec(RWKV_W), vec(RWKV_W), vec(RWKV_W),
        ],
        out_specs=[tok() for _ in range(7)]
        + [pl.BlockSpec((cpt, 1, RWKV_W), lambda b, i: (b * nb + i, 0, 0))],
        scratch_shapes=[pltpu.VMEM((1, RW_GROUP), F32)],
        compiler_params=_cparams(("parallel", "arbitrary")),
        name="rwkv_prep",
    )(p, mu, w0, ww2, a0, wa2, wg2, k_k, k_a, r_k)


def _rwkv_core_kernel(rt_ref, bt_ref, at_ref, kt_ref, v_ref, bonus_ref, g_ref, ptot_ref,
                      lw_ref, lb_ref, o_ref, h_scr):
    @pl.when(pl.program_id(0) == 0)
    def _():
        h_scr[...] = jnp.zeros_like(h_scr)

    assert CHUNK == RWKV_HD
    n_batch = rt_ref.shape[0]
    ti = lax.broadcasted_iota(jnp.int32, (CHUNK, PAIR), 0)
    si = lax.broadcasted_iota(jnp.int32, (CHUNK, PAIR), 1) % RWKV_HD
    strict = ti > si
    lower = ti >= si
    eye = ti == si
    blk8 = (ti // 8) == (si // 8)
    eye_f = jnp.where(eye, 1.0, 0.0)
    ri = lax.broadcasted_iota(jnp.int32, (PAIR, PAIR), 0)
    ci = lax.broadcasted_iota(jnp.int32, (PAIR, PAIR), 1)
    head_blk = (ri // RWKV_HD) == (ci // RWKV_HD)
    ones_pair = jnp.where(head_blk, 1.0, 0.0).astype(BF16)
    ones_2 = jnp.concatenate([ones_pair, ones_pair], axis=0)
    zeros_b = jnp.zeros((PAIR, PAIR), BF16)

    def bd(x):
        return jnp.where(head_blk, jnp.concatenate([x, x], axis=0), jnp.zeros((), x.dtype))

    def head_t(x):
        xt = x.T
        return jnp.concatenate([xt[:RWKV_HD], xt[RWKV_HD:]], axis=1)

    def dot_packed(lhs, rhs_bd):
        out = []
        for j in range(0, len(lhs), 2):
            l2 = jnp.concatenate([lhs[j], lhs[j + 1]], axis=1)
            r2 = jnp.concatenate([jnp.concatenate([rhs_bd[j], zeros_b], axis=1),
                                  jnp.concatenate([zeros_b, rhs_bd[j + 1]], axis=1)], axis=0)
            o = _dot(l2, r2)
            out += [o[:, :PAIR], o[:, PAIR:]]
        return out

    bf = lambda xs: [x.astype(BF16) for x in xs]

    def head_sums(xs):
        hi = [x.astype(BF16) for x in xs]
        lo = [(x - h.astype(F32)).astype(BF16) for x, h in zip(xs, hi)]
        rows = jnp.concatenate([jnp.concatenate([h, l], axis=1) for h, l in zip(hi, lo)], axis=0)
        sums = _dot(rows, ones_2)
        return [sums[j * CHUNK:(j + 1) * CHUNK] for j in range(len(xs))]

    pairs = range(n_batch * N_PAIR)
    where = [(b, slice(j * PAIR, (j + 1) * PAIR)) for b in range(n_batch) for j in range(N_PAIR)]
    ptot = [ptot_ref[b, 0, :, l] for b, l in where]
    bt = [bt_ref[b, :, l] for b, l in where]
    rt = [rt_ref[b, :, l] for b, l in where]
    at = [at_ref[b, :, l] for b, l in where]
    kt = [kt_ref[b, :, l] for b, l in where]
    bx = [bd(x) for x in bt]
    vx = [bd(v_ref[b, :, l]) for b, l in where]
    ak = [jnp.concatenate([bd(at[j]), bd(kt[j])], axis=0) for j in pairs]
    ak2t = [jnp.concatenate([head_t(at[j].astype(F32) * ptot[j]).astype(BF16),
                             head_t(kt[j].astype(F32) * ptot[j]).astype(BF16)], axis=1) for j in pairs]

    g = [_dot_nt(jnp.concatenate([bt[j], rt[j]], axis=0), ak[j]) for j in pairs]
    gb = [x[:CHUNK] for x in g]
    gr = [x[CHUNK:] for x in g]
    a_ab = [jnp.where(strict, x[:, :PAIR], 0.0) for x in gb]
    a_kb = [jnp.where(strict, x[:, PAIR:], 0.0).astype(BF16) for x in gb]
    a_r = [jnp.concatenate([jnp.where(lower, x[:, :PAIR], 0.0),
                            jnp.where(lower, x[:, PAIR:], 0.0)], axis=1).astype(BF16) for x in gr]

    a_d = [jnp.where(blk8, x, 0.0) for x in a_ab]
    a_db = bf(a_d)
    pw = bf(dot_packed(a_db, [bd(x) for x in a_db]))
    s = [eye_f + x for x in a_d]
    sp = [_dot(pw[j], jnp.concatenate([bd(s[j].astype(BF16)), bd(pw[j])], axis=1)) for j in pairs]
    s = [s[j] + sp[j][:, :PAIR] for j in pairs]
    pw = [x[:, PAIR:].astype(BF16) for x in sp]
    ps = dot_packed(pw, [bd(x) for x in bf(s)])
    s = [s[j] + ps[j] for j in pairs]
    for width in (8, 16, 32):
        off = ((ti // (2 * width)) == (si // (2 * width))) & ((ti // width) != (si // width))
        e = [jnp.where(off, x, 0.0).astype(BF16) for x in a_ab]
        sb = bf(s)
        es = bf(dot_packed(e, [bd(x) for x in sb]))
        ses = dot_packed(sb, [bd(x) for x in es])
        s = [s[j] + ses[j] for j in pairs]
    t_inv = bf(s)

    kv = bf(dot_packed(a_kb, vx))
    wu = bf([_dot(t_inv[j], jnp.concatenate([bx[j], bd(kv[j])], axis=1)) for j in pairs])
    z = [jnp.concatenate([jnp.concatenate([bd(wu[j][:, :PAIR]), bd(wu[j][:, PAIR:])], axis=1),
                          jnp.concatenate([zeros_b, vx[j]], axis=1)], axis=0) for j in pairs]
    mcqy = [_dot(jnp.concatenate([ak2t[j], a_r[j]], axis=0), z[j]) for j in pairs]
    mc = [x[:CHUNK] for x in mcqy]
    qy = [x[CHUNK:] for x in mcqy]

    m = [(mc[j][:, :PAIR] + jnp.where(eye, ptot[j], 0.0)).astype(BF16) for j in pairs]
    q = [(qy[j][:, :PAIR] + rt[j].astype(F32)).astype(BF16) for j in pairs]
    hb = [bd(h_scr[j].astype(BF16)) for j in pairs]
    qmh = dot_packed([jnp.concatenate([q[j], m[j]], axis=0) for j in pairs], hb)
    for j in pairs:
        h_scr[j] = qmh[j][CHUNK:] + mc[j][:, PAIR:]
    y = [qmh[j][:CHUNK] + qy[j][:, PAIR:] for j in pairs]

    mean = [x * (1.0 / RWKV_HD) for x in head_sums(y)]
    yc = [y[j] - mean[j] for j in pairs]
    var = [x * (1.0 / RWKV_HD) for x in head_sums([x * x for x in yc])]
    for j in pairs:
        b, l = where[j]
        yn = yc[j] * lax.rsqrt(var[j] + GN_EPS) * lw_ref[:, l] + lb_ref[:, l]
        o_ref[b, :, l] = ((yn + bonus_ref[b, :, l]) * g_ref[b, :, l]).astype(BF16)


def _rwkv_core(rt, bt, at, kt, v, bonus, g, ptot, lnx_w, lnx_b, B, S):
    nc = S // CHUNK
    seq = lambda a: a.reshape(B, S, RWKV_W)
    tok = lambda: pl.BlockSpec((B, CHUNK, RWKV_W), lambda c: (0, c, 0))
    vec = lambda: pl.BlockSpec((1, RWKV_W), lambda c: (0, 0))
    out = pl.pallas_call(
        _rwkv_core_kernel,
        out_shape=jax.ShapeDtypeStruct((B, S, RWKV_W), BF16),
        grid=(nc,),
        in_specs=[tok() for _ in range(7)]
        + [pl.BlockSpec((B, 1, 1, RWKV_W), lambda c: (0, c, 0, 0)), vec(), vec()],
        out_specs=tok(),
        scratch_shapes=[pltpu.VMEM((B * N_PAIR, RWKV_HD, PAIR), F32)],
        compiler_params=_cparams(("arbitrary",)),
        name="rwkv_core",
    )(seq(rt), seq(bt), seq(at), seq(kt), seq(v), seq(bonus), seq(g),
      ptot.reshape(B, nc, 1, RWKV_W), lnx_w, lnx_b)
    return out.reshape(B * S, RWKV_W)


def _merge_kernel(x_ref, og_ref, or_ref, gb_ref, wb1_ref, wb2_ref, wo_ref, *rest):
    o_ref = rest[-1]
    gate_refs = rest[:-1]
    n = len(gate_refs) // 2
    tn = gate_refs[0].shape[1]
    D = x_ref.shape[1]
    y_gla = _dot(og_ref[...], wb1_ref[...])
    y_rw = _dot(or_ref[...], wb2_ref[...])
    merged = []
    for c in range(n):
        cols = slice(c * tn, (c + 1) * tn)
        g_gla = _sigmoid(gate_refs[c][...] + gb_ref[:, c * tn:(c + 1) * tn])
        g_rw = _sigmoid(gate_refs[n + c][...] + gb_ref[:, D + c * tn:D + (c + 1) * tn])
        merged.append((g_gla * y_gla[:, cols] + g_rw * y_rw[:, cols]).astype(BF16))
    o_ref[...] = x_ref[...] + _dot(jnp.concatenate(merged, axis=1), wo_ref[...])


def _merge(x, o_gla, o_rw, p, gate_b, w_branch, wo, *, tm=256, tn=PROJ_TN):
    T, D = x.shape
    const = lambda shape, r: pl.BlockSpec(shape, lambda i: (r, 0), pipeline_mode=pl.Buffered(1))
    assert GLA_V == RWKV_W and GATE_COL % tn == 0 and D % tn == 0
    n_gate_blocks = 2 * D // tn
    gate_specs = [pl.BlockSpec((tm, tn), lambda i, c=GATE_COL // tn + c: (i, c)) for c in range(n_gate_blocks)]
    return pl.pallas_call(
        _merge_kernel,
        out_shape=jax.ShapeDtypeStruct((T, D), F32),
        grid=(T // tm,),
        in_specs=[
            pl.BlockSpec((tm, D), lambda i: (i, 0)),
            pl.BlockSpec((tm, GLA_V), lambda i: (i, 0)),
            pl.BlockSpec((tm, RWKV_W), lambda i: (i, 0)),
            pl.BlockSpec((1, 2 * D), lambda i: (0, 0)),
            const((GLA_V, D), 0), const((RWKV_W, D), 1), const((D, D), 0),
        ] + gate_specs,
        out_specs=pl.BlockSpec((tm, D), lambda i: (i, 0)),
        compiler_params=_cparams(("parallel",)),
        name="merge_out",
    )(x, o_gla, o_rw, gate_b, w_branch, w_branch, wo, *([p] * n_gate_blocks))


def _pad_rows(w, n):
    return jnp.pad(w, ((0, n - w.shape[0]), (0, 0)))


def kernel(x, ffn1_norm, ffn1_wg, ffn1_wu, ffn1_wd, mix_norm, w_in, gla_w_a2, gla_b_a, gla_gn_w, rwkv_mu,
           rwkv_w0, rwkv_w_w2, rwkv_a0, rwkv_w_a2, rwkv_w_g2, rwkv_k_k, rwkv_k_a, rwkv_r_k, rwkv_lnx_w,
           rwkv_lnx_b, gate_b, w_branch, w_out, ffn2_norm, ffn2_wg, ffn2_wu, ffn2_wd, final_norm):
    B, S, D = x.shape
    T = B * S
    depth = ffn1_norm.shape[0]
    assert depth >= 1
    bf = lambda w: w.astype(BF16)
    row = lambda v: v.reshape(1, -1)
    assert D == D_MODEL and w_in.shape[2] == GLA_IN + RWKV_IN + 2 * D_MODEL
    xt = x.reshape(T, D)
    for l in range(depth):
        last = l == depth - 1
        xt, h_mix, w_p, w_br, w_o = _ffn(
            xt, row(ffn1_norm[l]), bf(ffn1_wg[l]), bf(ffn1_wu[l]), bf(ffn1_wd[l]), row(mix_norm[l]),
            mode=EMIT_NORM,
            aux=[(PACK_T, w_in[l].T), (CAST, w_branch[l]), (CAST, w_out[l])])

        mu_row = jnp.pad(rwkv_mu[l], (GLA_IN, 2 * D_MODEL))[None, :]
        mu = _pack_cols(mu_row)[:, :RW_GROUP]
        p, wg2, wu2, wd2 = _proj(h_mix, w_p,
                                 aux=[(CAST, ffn2_wg[l]), (CAST, ffn2_wu[l]), (CAST, ffn2_wd[l])])

        o_gla = _gla(p, bf(_pad_rows(gla_w_a2[l], LANE)), row(gla_b_a[l]), row(gla_gn_w[l]), B, S)

        prep = _rwkv_prep(p, mu, row(rwkv_w0[l]), bf(_pad_rows(rwkv_w_w2[l], LANE)), row(rwkv_a0[l]),
                          bf(_pad_rows(rwkv_w_a2[l], LANE)), bf(rwkv_w_g2[l]), row(rwkv_k_k[l]),
                          row(rwkv_k_a[l]), row(rwkv_r_k[l]), B, S)
        o_rw = _rwkv_core(*prep, row(rwkv_lnx_w[l]), row(rwkv_lnx_b[l]), B, S)

        xt = _merge(xt, o_gla, o_rw, p, row(gate_b[l]), w_br, w_o)

        xt = _ffn(xt, row(ffn2_norm[l]), wg2, wu2, wd2, row(final_norm),
                  mode=FINAL if last else RESIDUAL)[0]
    return xt.reshape(B, S, D)
```

```python
import functools
import math

import jax
import jax.numpy as jnp
from jax import lax
from jax.experimental import pallas as pl
from jax.experimental.pallas import tpu as pltpu

F32 = jnp.float32
BF16 = jnp.bfloat16

NORM_EPS = 1e-6
GN_EPS = 64e-5
GLA_TAU = 16.0
CHUNK = 64

GLA_HEADS = 4
GLA_DK = 128
GLA_DV = 256
GLA_QK = GLA_HEADS * GLA_DK
GLA_V = GLA_HEADS * GLA_DV
GLA_LORA = 16

RWKV_HD = 64
RWKV_W = 1024
DECAY_LORA = 96
AAA_LORA = 96
GATE_LORA = 256

LANE = 128
PAIR = 2 * RWKV_HD
N_PAIR = RWKV_W // PAIR

RW_GROUP = 3 * RWKV_W + LANE + LANE + GATE_LORA
GLA_GROUP = RW_GROUP
RW_WD = 3 * RWKV_W
GLA_AD = 2 * GLA_QK + 2 * GLA_V
RW_AD = RW_WD + LANE
RW_GD = RW_AD + LANE

VMEM_LIMIT = 60 * 1024 * 1024
BF16_SUBLANES = 16

FFN_TM, FFN_TF = 512, 512
PROJ_TM, PROJ_TN = 2048, 1024


def _rows_per_block(rows, steps):
    for rb in range(BF16_SUBLANES, rows + 1, BF16_SUBLANES):
        if rows % rb == 0 and rows // rb <= steps:
            return rb
    raise ValueError(f"cannot split {rows} rows over {steps} steps")


def _cparams(sem):
    return pltpu.CompilerParams(dimension_semantics=sem, vmem_limit_bytes=VMEM_LIMIT)


def _dot(a, b):
    return jnp.dot(a, b, preferred_element_type=F32)


def _dot_nt(a, b):
    return lax.dot_general(a, b, (((1,), (1,)), ((), ())), preferred_element_type=F32)


def _dot_tn(a, b):
    return lax.dot_general(a, b, (((0,), (0,)), ((), ())), preferred_element_type=F32)


def _rmsnorm(x, g):
    return x * lax.rsqrt(jnp.mean(x * x, axis=-1, keepdims=True) + NORM_EPS) * g


def _sigmoid(z):
    return 0.5 + 0.5 * jnp.tanh(0.5 * z)


def _softplus(z):
    return jnp.maximum(z, 0.0) + jnp.log(1.0 + jnp.exp(-jnp.abs(z)))


CAST = "cast"
PACK_T = "pack_t"

D_MODEL = 2048
GLA_IN = 2 * GLA_QK + 2 * GLA_V + GLA_LORA
RWKV_IN = 3 * RWKV_W + DECAY_LORA + AAA_LORA + GATE_LORA
GATE_COL = RW_GROUP + GLA_GROUP
PACKED_COLS = GATE_COL + 2 * D_MODEL

IN_PIECES = (
    (0, GLA_IN, 3 * RWKV_W),
    (RW_WD, GLA_IN + 3 * RWKV_W, DECAY_LORA),
    (RW_AD, GLA_IN + 3 * RWKV_W + DECAY_LORA, AAA_LORA),
    (RW_GD, GLA_IN + 3 * RWKV_W + DECAY_LORA + AAA_LORA, GATE_LORA),
    (RW_GROUP, 0, GLA_IN),
    (GATE_COL, GLA_IN + RWKV_IN, 2 * D_MODEL),
)


def _pack_cols(w):
    zeros = lambda n: jnp.zeros((w.shape[0], n), w.dtype)
    parts, pos = [], 0
    for start, w_start, width in IN_PIECES:
        parts += [zeros(start - pos), w[:, w_start:w_start + width]]
        pos = start + width
    assert pos == PACKED_COLS
    return jnp.concatenate([p for p in parts if p.shape[1]], axis=1)


def _pack_src(k):
    col = k * LANE
    src = jnp.zeros_like(col)
    valid = jnp.zeros_like(col)
    for start, w_start, width in IN_PIECES:
        inside = (col >= start) & (col < start + width)
        src = jnp.where(inside, w_start + col - start, src)
        valid = jnp.where(inside, jnp.minimum(start + width - col, LANE), valid)
    return src, valid


def _aux_plan(aux, nj, steps):
    in_specs, operands, out_specs, out_shapes, meta = [], [], [], [], []
    for kind, w in aux:
        rows, cols = w.shape
        if kind == CAST:
            rb = _rows_per_block(rows, steps)
            n_active = rows // rb
            index = lambda i, j, n=n_active: (jnp.minimum(i * nj + j, n - 1), 0)
            in_specs.append(pl.BlockSpec((rb, cols), index))
            operands.append(w)
            out_specs.append(pl.BlockSpec((rb, cols), index))
            out_shapes.append(jax.ShapeDtypeStruct((rows, cols), BF16))
            meta.append((kind, n_active, 1, 1))
        else:
            n_blocks = PACKED_COLS // LANE
            per_step = next(d for d in range(1, n_blocks + 1) if n_blocks % d == 0 and n_blocks // d <= steps)
            n_active = n_blocks // per_step
            last = n_active - 1
            for q in range(per_step):
                src = lambda i, j, q=q, last=last, per_step=per_step: (
                    pl.multiple_of(_pack_src(jnp.minimum(i * nj + j, last) * per_step + q)[0],
                                   BF16_SUBLANES), 0)
                in_specs.append(pl.BlockSpec((pl.Element(LANE), pl.Element(cols)), src))
                operands.append(w)
            out_specs.append(pl.BlockSpec((cols, per_step * LANE),
                                          lambda i, j, last=last: (0, jnp.minimum(i * nj + j, last))))
            out_shapes.append(jax.ShapeDtypeStruct((cols, PACKED_COLS), BF16))
            meta.append((kind, n_active, per_step, 1))
    return in_specs, operands, out_specs, out_shapes, tuple(meta)


def _aux_run(meta, in_refs, out_refs):
    step = pl.program_id(0) * pl.num_programs(1) + pl.program_id(1)
    ki = ko = 0
    for kind, n_active, n_in, n_out in meta:
        ins, outs = in_refs[ki:ki + n_in], out_refs[ko:ko + n_out]
        ki += n_in
        ko += n_out

        @pl.when(step < n_active)
        def _(kind=kind, ins=ins, outs=outs, n_in=n_in):
            if kind == CAST:
                outs[0][...] = ins[0][...].astype(BF16)
            else:
                for q, w_ref in enumerate(ins):
                    w = w_ref[...]
                    _, valid = _pack_src(step * n_in + q)
                    row = lax.broadcasted_iota(jnp.int32, w.shape, 0)
                    w = jnp.where(row < valid, w, 0.0)
                    outs[0][:, q * LANE:(q + 1) * LANE] = w.T.astype(BF16)


RESIDUAL = "residual"
FINAL = "final"
EMIT_NORM = "emit"


def _ffn_kernel(*refs, mode, aux_meta):
    x_ref, g_ref, wg_ref, wu_ref, wd_ref, fn_ref = refs[:6]
    n_aux = sum(m[2] for m in aux_meta)
    n_main_out = 2 if mode == EMIT_NORM else 1
    o_ref = refs[6 + n_aux]
    h_scr = refs[-1]
    _aux_run(aux_meta, refs[6:6 + n_aux], refs[6 + n_aux + n_main_out:-1])
    j = pl.program_id(1)

    @pl.when(j == 0)
    def _():
        h_scr[...] = _rmsnorm(x_ref[...], g_ref[...]).astype(BF16)
        o_ref[...] = jnp.zeros_like(o_ref)

    h = h_scr[...]
    a = _dot(h, wg_ref[...])
    u = _dot(h, wu_ref[...])
    act = (a * _sigmoid(a) * u).astype(BF16)
    o_ref[...] += _dot(act, wd_ref[...])

    @pl.when(j == pl.num_programs(1) - 1)
    def _():
        y = x_ref[...] + 0.5 * o_ref[...]
        if mode == FINAL:
            y = _rmsnorm(y, fn_ref[...])
        o_ref[...] = y
        if mode == EMIT_NORM:
            refs[7 + n_aux][...] = _rmsnorm(y, fn_ref[...]).astype(BF16)


def _ffn(x, g, wg, wu, wd, fn, *, mode, aux=(), tm=FFN_TM, tf=FFN_TF):
    T, D = x.shape
    FF = wg.shape[1]
    nj = FF // tf
    aux_in, aux_ops, aux_out, aux_shapes, aux_meta = _aux_plan(aux, nj, (T // tm) * nj)
    tok = lambda: pl.BlockSpec((tm, D), lambda i, j: (i, 0))
    main_shapes = [jax.ShapeDtypeStruct((T, D), F32)]
    if mode == EMIT_NORM:
        main_shapes.append(jax.ShapeDtypeStruct((T, D), BF16))
    main_specs = [tok() for _ in main_shapes]
    return pl.pallas_call(
        functools.partial(_ffn_kernel, mode=mode, aux_meta=aux_meta),
        out_shape=main_shapes + aux_shapes,
        grid=(T // tm, nj),
        in_specs=[
            tok(),
            pl.BlockSpec((1, D), lambda i, j: (0, 0)),
            pl.BlockSpec((D, tf), lambda i, j: (0, j)),
            pl.BlockSpec((D, tf), lambda i, j: (0, j)),
            pl.BlockSpec((tf, D), lambda i, j: (j, 0)),
            pl.BlockSpec((1, D), lambda i, j: (0, 0)),
        ] + aux_in,
        out_specs=main_specs + aux_out,
        scratch_shapes=[pltpu.VMEM((tm, D), BF16)],
        compiler_params=_cparams(("arbitrary", "arbitrary")),
        name="ffn_" + mode,
    )(x, g, wg, wu, wd, fn, *aux_ops)


def _proj_kernel(*refs, aux_meta):
    h_ref, w_ref = refs[:2]
    n_aux = sum(m[2] for m in aux_meta)
    _aux_run(aux_meta, refs[2:2 + n_aux], refs[3 + n_aux:])
    refs[2 + n_aux][...] = _dot(h_ref[...], w_ref[...])


def _proj(h, w, *, aux=(), tm=PROJ_TM, tn=PROJ_TN):
    T, D = h.shape
    N = w.shape[1]
    nj = N // tn
    aux_in, aux_ops, aux_out, aux_shapes, aux_meta = _aux_plan(aux, nj, (T // tm) * nj)
    return pl.pallas_call(
        functools.partial(_proj_kernel, aux_meta=aux_meta),
        out_shape=[jax.ShapeDtypeStruct((T, N), F32)] + aux_shapes,
        grid=(T // tm, nj),
        in_specs=[
            pl.BlockSpec((tm, D), lambda i, j: (i, 0), pipeline_mode=pl.Buffered(1)),
            pl.BlockSpec((D, tn), lambda i, j: (0, j)),
        ] + aux_in,
        out_specs=[pl.BlockSpec((tm, tn), lambda i, j: (i, j))] + aux_out,
        compiler_params=_cparams(("arbitrary", "arbitrary")),
        name="in_proj",
    )(h, w, *aux_ops)


def _chunk_tri(n):
    r = lax.broadcasted_iota(jnp.int32, (n, n), 0)
    c = lax.broadcasted_iota(jnp.int32, (n, n), 1)
    return jnp.where((r >= c) & (r // CHUNK == c // CHUNK), 1.0, 0.0).astype(BF16)


def _gla_kernel(p_ref, wa2_ref, ba_ref, gn_ref, o_ref, st_scr, *, tg):
    @pl.when(pl.program_id(1) == 0)
    def _():
        st_scr[...] = jnp.zeros_like(st_scr)

    gad = p_ref[:, GLA_AD:GLA_AD + LANE].astype(BF16)
    z = _dot(gad, wa2_ref[...]) + ba_ref[...]
    log_alpha = -_softplus(-z) * (1.0 / GLA_TAU)
    tri = _chunk_tri(tg)
    hi = log_alpha.astype(BF16)
    lo = (log_alpha - hi.astype(F32)).astype(BF16)
    cum = _dot(tri, hi) + _dot(tri, lo)
    gn = gn_ref[...]
    scale = GLA_DK ** -0.5

    n_chunks = tg // CHUNK
    rows = [slice(c * CHUNK, (c + 1) * CHUNK) for c in range(n_chunks)]
    keys = [slice(h * GLA_DK, (h + 1) * GLA_DK) for h in range(GLA_HEADS)]
    vals = [slice(h * GLA_DV, (h + 1) * GLA_DV) for h in range(GLA_HEADS)]
    units = [(c, h) for c in range(n_chunks) for h in range(GLA_HEADS)]
    tot = [cum[r, :][CHUNK - 1:CHUNK, :] for r in rows]
    etot = [jnp.exp(t) for t in tot]
    kdec = [(p_ref[rows[c], GLA_QK:2 * GLA_QK] * jnp.exp(tot[c] - cum[rows[c], :])).astype(BF16)
            for c in range(n_chunks)]
    q = [(p_ref[r, 0:GLA_QK] * scale).astype(BF16) for r in rows]
    inc = {}
    for c, h in units:
        v_h = p_ref[rows[c], 2 * GLA_QK + h * GLA_DV:2 * GLA_QK + (h + 1) * GLA_DV].astype(BF16)
        inc[c, h] = _dot_tn(v_h, kdec[c][:, keys[h]])
    st = {}
    for h in range(GLA_HEADS):
        prev = st_scr[h]
        for c in range(n_chunks):
            prev = prev * etot[c][:, keys[h]] + inc[c, h]
            st[c, h] = prev
        st_scr[h] = prev
    out = {u: _dot_nt(q[u[0]][:, keys[u[1]]], st[u].astype(BF16)) for u in units}
    for c, h in units:
        o = out[c, h]
        r_h = p_ref[rows[c], 2 * GLA_QK + GLA_V + h * GLA_DV:2 * GLA_QK + GLA_V + (h + 1) * GLA_DV]
        o = o * lax.rsqrt(jnp.mean(o * o, axis=-1, keepdims=True) + NORM_EPS) * gn
        o = o * (r_h * _sigmoid(r_h))
        o_ref[rows[c], vals[h]] = o.astype(BF16)


def _gla(p, wa2, ba, gn, B, S, *, tg=256):
    T = B * S
    nb = S // tg
    return pl.pallas_call(
        functools.partial(_gla_kernel, tg=tg),
        out_shape=jax.ShapeDtypeStruct((T, GLA_V), BF16),
        grid=(B, nb),
        in_specs=[
            pl.BlockSpec((tg, GLA_GROUP), lambda b, i: (b * nb + i, 1)),
            pl.BlockSpec((LANE, GLA_QK), lambda b, i: (0, 0)),
            pl.BlockSpec((1, GLA_QK), lambda b, i: (0, 0)),
            pl.BlockSpec((1, GLA_DV), lambda b, i: (0, 0)),
        ],
        out_specs=pl.BlockSpec((tg, GLA_V), lambda b, i: (b * nb + i, 0)),
        scratch_shapes=[pltpu.VMEM((GLA_HEADS, GLA_DV, GLA_DK), F32)],
        compiler_params=_cparams(("parallel", "arbitrary")),
        name="gla",
    )(p, wa2, ba, gn)


MXU_DIM = 256


def _head_ones(n):
    r = lax.broadcasted_iota(jnp.int32, (n, n), 0)
    c = lax.broadcasted_iota(jnp.int32, (n, n), 1)
    return jnp.where(r // RWKV_HD == c // RWKV_HD, 1.0, 0.0).astype(BF16)


def _head_sum(x, ones_blk):
    xb = x.astype(BF16)
    parts = [_dot(xb[:, j:j + MXU_DIM], ones_blk) for j in range(0, x.shape[1], MXU_DIM)]
    return jnp.concatenate(parts, axis=1)


def _rwkv_prep_kernel(p_ref, mu_ref, w0_ref, ww2_ref, a0_ref, wa2_ref, wg2_ref, kk_ref, ka_ref, rk_ref,
                      rt_ref, bt_ref, at_ref, kt_ref, v_ref, bonus_ref, g_ref, ptot_ref,
                      carry_scr, *, tm):
    @pl.when(pl.program_id(1) == 0)
    def _():
        carry_scr[...] = jnp.zeros_like(carry_scr)

    p = p_ref[...]
    last = carry_scr[...]
    carry_scr[...] = p[tm - 1:tm, :]
    prev = pltpu.roll(p, 1, axis=0)
    sub = 8
    row = lax.broadcasted_iota(jnp.int32, (sub, p.shape[1]), 0)
    prev = jnp.concatenate([jnp.where(row == 0, last, prev[:sub]), prev[sub:]], axis=0)
    p = p + mu_ref[...] * (prev - p)

    r = p[:, 0:RWKV_W]
    k = p[:, RWKV_W:2 * RWKV_W]
    v = p[:, 2 * RWKV_W:3 * RWKV_W]
    wd = p[:, RW_WD:RW_WD + LANE]
    ad = p[:, RW_AD:RW_AD + LANE]
    gd = p[:, RW_GD:RW_GD + GATE_LORA]

    w_raw = w0_ref[...] + _dot(jnp.tanh(wd).astype(BF16), ww2_ref[...])
    log_w = (-math.exp(-0.5)) * _sigmoid(w_raw)
    a = _sigmoid(a0_ref[...] + _dot(ad.astype(BF16), wa2_ref[...]))
    g_ref[...] = _dot(_sigmoid(gd).astype(BF16), wg2_ref[...])

    ones_blk = _head_ones(MXU_DIM)
    kk = k * kk_ref[...]
    kk = kk * lax.rsqrt(jnp.maximum(_head_sum(kk * kk, ones_blk), 1e-24))
    kp = k * (1.0 + (a - 1.0) * ka_ref[...])
    bonus_ref[...] = _head_sum(r * kp * rk_ref[...], ones_blk) * v
    v_ref[...] = v.astype(BF16)

    tri = _chunk_tri(tm)
    hi = log_w.astype(BF16)
    lo = (log_w - hi.astype(F32)).astype(BF16)
    cum = _dot(tri, hi) + _dot(tri, lo)
    nalpha = -(kk * a)
    for ci in range(tm // CHUNK):
        rows = slice(ci * CHUNK, (ci + 1) * CHUNK)
        cum_c = cum[rows, :]
        tot = cum_c[CHUNK - 1:CHUNK, :]
        e_inv = jnp.exp(-cum_c)
        rt_ref[rows, :] = (r[rows, :] * jnp.exp(cum_c)).astype(BF16)
        bt_ref[rows, :] = (kk[rows, :] * jnp.exp(cum_c - log_w[rows, :])).astype(BF16)
        at_ref[rows, :] = (nalpha[rows, :] * e_inv).astype(BF16)
        kt_ref[rows, :] = (kp[rows, :] * e_inv).astype(BF16)
        ptot_ref[ci] = jnp.exp(tot)


def _rwkv_prep(p, mu, w0, ww2, a0, wa2, wg2, k_k, k_a, r_k, B, S, *, tm=256):
    T = B * S
    nb = S // tm
    cpt = tm // CHUNK
    vec = lambda n: pl.BlockSpec((1, n), lambda b, i: (0, 0))
    tok = lambda: pl.BlockSpec((tm, RWKV_W), lambda b, i: (b * nb + i, 0))
    bf = jax.ShapeDtypeStruct((T, RWKV_W), BF16)
    f32 = jax.ShapeDtypeStruct((T, RWKV_W), F32)
    return pl.pallas_call(
        functools.partial(_rwkv_prep_kernel, tm=tm),
        out_shape=[bf, bf, bf, bf, bf, f32, f32,
                   jax.ShapeDtypeStruct((T // CHUNK, 1, RWKV_W), F32)],
        grid=(B, nb),
        in_specs=[
            pl.BlockSpec((tm, RW_GROUP), lambda b, i: (b * nb + i, 0)),
            vec(RW_GROUP), vec(RWKV_W),
            pl.BlockSpec((LANE, RWKV_W), lambda b, i: (0, 0)),
            vec(RWKV_W),
            pl.BlockSpec((LANE, RWKV_W), lambda b, i: (0, 0)),
            pl.BlockSpec((GATE_LORA, RWKV_W), lambda b, i: (0, 0)),
            vec(RWKV_W), vec(RWKV_W), vec(RWKV_W),
        ],
        out_specs=[tok() for _ in range(7)]
        + [pl.BlockSpec((cpt, 1, RWKV_W), lambda b, i: (b * nb + i, 0, 0))],
        scratch_shapes=[pltpu.VMEM((1, RW_GROUP), F32)],
        compiler_params=_cparams(("parallel", "arbitrary")),
        name="rwkv_prep",
    )(p, mu, w0, ww2, a0, wa2, wg2, k_k, k_a, r_k)


def _rwkv_core_kernel(rt_ref, bt_ref, at_ref, kt_ref, v_ref, bonus_ref, g_ref, ptot_ref,
                      lw_ref, lb_ref, o_ref, h_scr):
    @pl.when(pl.program_id(0) == 0)
    def _():
        h_scr[...] = jnp.zeros_like(h_scr)

    assert CHUNK == RWKV_HD
    n_batch = rt_ref.shape[0]
    ti = lax.broadcasted_iota(jnp.int32, (CHUNK, PAIR), 0)
    si = lax.broadcasted_iota(jnp.int32, (CHUNK, PAIR), 1) % RWKV_HD
    strict = ti > si
    lower = ti >= si
    eye = ti == si
    blk8 = (ti // 8) == (si // 8)
    eye_f = jnp.where(eye, 1.0, 0.0)
    ri = lax.broadcasted_iota(jnp.int32, (PAIR, PAIR), 0)
    ci = lax.broadcasted_iota(jnp.int32, (PAIR, PAIR), 1)
    head_blk = (ri // RWKV_HD) == (ci // RWKV_HD)
    ones_pair = jnp.where(head_blk, 1.0, 0.0).astype(BF16)
    ones_2 = jnp.concatenate([ones_pair, ones_pair], axis=0)
    zeros_b = jnp.zeros((PAIR, PAIR), BF16)

    def bd(x):
        return jnp.where(head_blk, jnp.concatenate([x, x], axis=0), jnp.zeros((), x.dtype))

    def head_t(x):
        xt = x.T
        return jnp.concatenate([xt[:RWKV_HD], xt[RWKV_HD:]], axis=1)

    def dot_packed(lhs, rhs_bd):
        out = []
        for j in range(0, len(lhs), 2):
            l2 = jnp.concatenate([lhs[j], lhs[j + 1]], axis=1)
            r2 = jnp.concatenate([jnp.concatenate([rhs_bd[j], zeros_b], axis=1),
                                  jnp.concatenate([zeros_b, rhs_bd[j + 1]], axis=1)], axis=0)
            o = _dot(l2, r2)
            out += [o[:, :PAIR], o[:, PAIR:]]
        return out

    bf = lambda xs: [x.astype(BF16) for x in xs]

    def head_sums(xs):
        hi = [x.astype(BF16) for x in xs]
        lo = [(x - h.astype(F32)).astype(BF16) for x, h in zip(xs, hi)]
        rows = jnp.concatenate([jnp.concatenate([h, l], axis=1) for h, l in zip(hi, lo)], axis=0)
        sums = _dot(rows, ones_2)
        return [sums[j * CHUNK:(j + 1) * CHUNK] for j in range(len(xs))]

    pairs = range(n_batch * N_PAIR)
    where = [(b, slice(j * PAIR, (j + 1) * PAIR)) for b in range(n_batch) for j in range(N_PAIR)]
    ptot = [ptot_ref[b, 0, :, l] for b, l in where]
    bt = [bt_ref[b, :, l] for b, l in where]
    rt = [rt_ref[b, :, l] for b, l in where]
    at = [at_ref[b, :, l] for b, l in where]
    kt = [kt_ref[b, :, l] for b, l in where]
    bx = [bd(x) for x in bt]
    vx = [bd(v_ref[b, :, l]) for b, l in where]
    ak = [jnp.concatenate([bd(at[j]), bd(kt[j])], axis=0) for j in pairs]
    ak2t = [jnp.concatenate([head_t(at[j].astype(F32) * ptot[j]).astype(BF16),
                             head_t(kt[j].astype(F32) * ptot[j]).astype(BF16)], axis=1) for j in pairs]

    g = [_dot_nt(jnp.concatenate([bt[j], rt[j]], axis=0), ak[j]) for j in pairs]
    gb = [x[:CHUNK] for x in g]
    gr = [x[CHUNK:] for x in g]
    a_ab = [jnp.where(strict, x[:, :PAIR], 0.0) for x in gb]
    a_kb = [jnp.where(strict, x[:, PAIR:], 0.0).astype(BF16) for x in gb]
    a_r = [jnp.concatenate([jnp.where(lower, x[:, :PAIR], 0.0),
                            jnp.where(lower, x[:, PAIR:], 0.0)], axis=1).astype(BF16) for x in gr]

    a_d = [jnp.where(blk8, x, 0.0) for x in a_ab]
    a_db = bf(a_d)
    pw = bf(dot_packed(a_db, [bd(x) for x in a_db]))
    s = [eye_f + x for x in a_d]
    sp = [_dot(pw[j], jnp.concatenate([bd(s[j].astype(BF16)), bd(pw[j])], axis=1)) for j in pairs]
    s = [s[j] + sp[j][:, :PAIR] for j in pairs]
    pw = [x[:, PAIR:].astype(BF16) for x in sp]
    ps = dot_packed(pw, [bd(x) for x in bf(s)])
    s = [s[j] + ps[j] for j in pairs]
    for width in (8, 16, 32):
        off = ((ti // (2 * width)) == (si // (2 * width))) & ((ti // width) != (si // width))
        e = [jnp.where(off, x, 0.0).astype(BF16) for x in a_ab]
        sb = bf(s)
        es = bf(dot_packed(e, [bd(x) for x in sb]))
        ses = dot_packed(sb, [bd(x) for x in es])
        s = [s[j] + ses[j] for j in pairs]
    t_inv = bf(s)

    kv = bf(dot_packed(a_kb, vx))
    wu = bf([_dot(t_inv[j], jnp.concatenate([bx[j], bd(kv[j])], axis=1)) for j in pairs])
    z = [jnp.concatenate([jnp.concatenate([bd(wu[j][:, :PAIR]), bd(wu[j][:, PAIR:])], axis=1),
                          jnp.concatenate([zeros_b, vx[j]], axis=1)], axis=0) for j in pairs]
    mcqy = [_dot(jnp.concatenate([ak2t[j], a_r[j]], axis=0), z[j]) for j in pairs]
    mc = [x[:CHUNK] for x in mcqy]
    qy = [x[CHUNK:] for x in mcqy]

    m = [(mc[j][:, :PAIR] + jnp.where(eye, ptot[j], 0.0)).astype(BF16) for j in pairs]
    q = [(qy[j][:, :PAIR] + rt[j].astype(F32)).astype(BF16) for j in pairs]
    hb = [bd(h_scr[j].astype(BF16)) for j in pairs]
    qmh = dot_packed([jnp.concatenate([q[j], m[j]], axis=0) for j in pairs], hb)
    for j in pairs:
        h_scr[j] = qmh[j][CHUNK:] + mc[j][:, PAIR:]
    y = [qmh[j][:CHUNK] + qy[j][:, PAIR:] for j in pairs]

    mean = [x * (1.0 / RWKV_HD) for x in head_sums(y)]
    yc = [y[j] - mean[j] for j in pairs]
    var = [x * (1.0 / RWKV_HD) for x in head_sums([x * x for x in yc])]
    for j in pairs:
        b, l = where[j]
        yn = yc[j] * lax.rsqrt(var[j] + GN_EPS) * lw_ref[:, l] + lb_ref[:, l]
        o_ref[b, :, l] = ((yn + bonus_ref[b, :, l]) * g_ref[b, :, l]).astype(BF16)


def _rwkv_core(rt, bt, at, kt, v, bonus, g, ptot, lnx_w, lnx_b, B, S):
    nc = S // CHUNK
    seq = lambda a: a.reshape(B, S, RWKV_W)
    tok = lambda: pl.BlockSpec((B, CHUNK, RWKV_W), lambda c: (0, c, 0))
    vec = lambda: pl.BlockSpec((1, RWKV_W), lambda c: (0, 0))
    out = pl.pallas_call(
        _rwkv_core_kernel,
        out_shape=jax.ShapeDtypeStruct((B, S, RWKV_W), BF16),
        grid=(nc,),
        in_specs=[tok() for _ in range(7)]
        + [pl.BlockSpec((B, 1, 1, RWKV_W), lambda c: (0, c, 0, 0)), vec(), vec()],
        out_specs=tok(),
        scratch_shapes=[pltpu.VMEM((B * N_PAIR, RWKV_HD, PAIR), F32)],
        compiler_params=_cparams(("arbitrary",)),
        name="rwkv_core",
    )(seq(rt), seq(bt), seq(at), seq(kt), seq(v), seq(bonus), seq(g),
      ptot.reshape(B, nc, 1, RWKV_W), lnx_w, lnx_b)
    return out.reshape(B * S, RWKV_W)


def _merge_kernel(x_ref, og_ref, or_ref, gb_ref, wb1_ref, wb2_ref, wo_ref, *rest):
    o_ref = rest[-1]
    gate_refs = rest[:-1]
    n = len(gate_refs) // 2
    tn = gate_refs[0].shape[1]
    D = x_ref.shape[1]
    y_gla = _dot(og_ref[...], wb1_ref[...])
    y_rw = _dot(or_ref[...], wb2_ref[...])
    merged = []
    for c in range(n):
        cols = slice(c * tn, (c + 1) * tn)
        g_gla = _sigmoid(gate_refs[c][...] + gb_ref[:, c * tn:(c + 1) * tn])
        g_rw = _sigmoid(gate_refs[n + c][...] + gb_ref[:, D + c * tn:D + (c + 1) * tn])
        merged.append((g_gla * y_gla[:, cols] + g_rw * y_rw[:, cols]).astype(BF16))
    o_ref[...] = x_ref[...] + _dot(jnp.concatenate(merged, axis=1), wo_ref[...])


def _merge(x, o_gla, o_rw, p, gate_b, w_branch, wo, *, tm=256, tn=PROJ_TN):
    T, D = x.shape
    const = lambda shape, r: pl.BlockSpec(shape, lambda i: (r, 0), pipeline_mode=pl.Buffered(1))
    assert GLA_V == RWKV_W and GATE_COL % tn == 0 and D % tn == 0
    n_gate_blocks = 2 * D // tn
    gate_specs = [pl.BlockSpec((tm, tn), lambda i, c=GATE_COL // tn + c: (i, c)) for c in range(n_gate_blocks)]
    return pl.pallas_call(
        _merge_kernel,
        out_shape=jax.ShapeDtypeStruct((T, D), F32),
        grid=(T // tm,),
        in_specs=[
            pl.BlockSpec((tm, D), lambda i: (i, 0)),
            pl.BlockSpec((tm, GLA_V), lambda i: (i, 0)),
            pl.BlockSpec((tm, RWKV_W), lambda i: (i, 0)),
            pl.BlockSpec((1, 2 * D), lambda i: (0, 0)),
            const((GLA_V, D), 0), const((RWKV_W, D), 1), const((D, D), 0),
        ] + gate_specs,
        out_specs=pl.BlockSpec((tm, D), lambda i: (i, 0)),
        compiler_params=_cparams(("parallel",)),
        name="merge_out",
    )(x, o_gla, o_rw, gate_b, w_branch, w_branch, wo, *([p] * n_gate_blocks))


def _pad_rows(w, n):
    return jnp.pad(w, ((0, n - w.shape[0]), (0, 0)))


def kernel(x, ffn1_norm, ffn1_wg, ffn1_wu, ffn1_wd, mix_norm, w_in, gla_w_a2, gla_b_a, gla_gn_w, rwkv_mu,
           rwkv_w0, rwkv_w_w2, rwkv_a0, rwkv_w_a2, rwkv_w_g2, rwkv_k_k, rwkv_k_a, rwkv_r_k, rwkv_lnx_w,
           rwkv_lnx_b, gate_b, w_branch, w_out, ffn2_norm, ffn2_wg, ffn2_wu, ffn2_wd, final_norm):
    B, S, D = x.shape
    T = B * S
    depth = ffn1_norm.shape[0]
    assert depth >= 1
    bf = lambda w: w.astype(BF16)
    row = lambda v: v.reshape(1, -1)
    assert D == D_MODEL and w_in.shape[2] == GLA_IN + RWKV_IN + 2 * D_MODEL
    xt = x.reshape(T, D)
    for l in range(depth):
        last = l == depth - 1
        xt, h_mix, w_p, w_br, w_o = _ffn(
            xt, row(ffn1_norm[l]), bf(ffn1_wg[l]), bf(ffn1_wu[l]), bf(ffn1_wd[l]), row(mix_norm[l]),
            mode=EMIT_NORM,
            aux=[(PACK_T, w_in[l].T), (CAST, w_branch[l]), (CAST, w_out[l])])

        mu_row = jnp.pad(rwkv_mu[l], (GLA_IN, 2 * D_MODEL))[None, :]
        mu = _pack_cols(mu_row)[:, :RW_GROUP]
        p, wg2, wu2, wd2 = _proj(h_mix, w_p,
                                 aux=[(CAST, ffn2_wg[l]), (CAST, ffn2_wu[l]), (CAST, ffn2_wd[l])])

        o_gla = _gla(p, bf(_pad_rows(gla_w_a2[l], LANE)), row(gla_b_a[l]), row(gla_gn_w[l]), B, S)

        prep = _rwkv_prep(p, mu, row(rwkv_w0[l]), bf(_pad_rows(rwkv_w_w2[l], LANE)), row(rwkv_a0[l]),
                          bf(_pad_rows(rwkv_w_a2[l], LANE)), bf(rwkv_w_g2[l]), row(rwkv_k_k[l]),
                          row(rwkv_k_a[l]), row(rwkv_r_k[l]), B, S)
        o_rw = _rwkv_core(*prep, row(rwkv_lnx_w[l]), row(rwkv_lnx_b[l]), B, S)

        xt = _merge(xt, o_gla, o_rw, p, row(gate_b[l]), w_br, w_o)

        xt = _ffn(xt, row(ffn2_norm[l]), wg2, wu2, wd2, row(final_norm),
                  mode=FINAL if last else RESIDUAL)[0]
    return xt.reshape(B, S, D)
```

```python
import functools
import math

import jax
import jax.numpy as jnp
from jax import lax
from jax.experimental import pallas as pl
from jax.experimental.pallas import tpu as pltpu

F32 = jnp.float32
BF16 = jnp.bfloat16

NORM_EPS = 1e-6
GN_EPS = 64e-5
GLA_TAU = 16.0
CHUNK = 64

GLA_HEADS = 4
GLA_DK = 128
GLA_DV = 256
GLA_QK = GLA_HEADS * GLA_DK
GLA_V = GLA_HEADS * GLA_DV
GLA_LORA = 16

RWKV_HD = 64
RWKV_W = 1024
DECAY_LORA = 96
AAA_LORA = 96
GATE_LORA = 256

LANE = 128
PAIR = 2 * RWKV_HD
N_PAIR = RWKV_W // PAIR

RW_GROUP = 3 * RWKV_W + LANE + LANE + GATE_LORA
GLA_GROUP = RW_GROUP
RW_WD = 3 * RWKV_W
GLA_AD = 2 * GLA_QK + 2 * GLA_V
RW_AD = RW_WD + LANE
RW_GD = RW_AD + LANE

VMEM_LIMIT = 60 * 1024 * 1024
BF16_SUBLANES = 16

FFN_TM, FFN_TF = 512, 512
PROJ_TM, PROJ_TN = 2048, 1024


def _rows_per_block(rows, steps):
    for rb in range(BF16_SUBLANES, rows + 1, BF16_SUBLANES):
        if rows % rb == 0 and rows // rb <= steps:
            return rb
    raise ValueError(f"cannot split {rows} rows over {steps} steps")


def _cparams(sem):
    return pltpu.CompilerParams(dimension_semantics=sem, vmem_limit_bytes=VMEM_LIMIT)


def _dot(a, b):
    return jnp.dot(a, b, preferred_element_type=F32)


def _dot_nt(a, b):
    return lax.dot_general(a, b, (((1,), (1,)), ((), ())), preferred_element_type=F32)


def _dot_tn(a, b):
    return lax.dot_general(a, b, (((0,), (0,)), ((), ())), preferred_element_type=F32)


def _rmsnorm(x, g):
    return x * lax.rsqrt(jnp.mean(x * x, axis=-1, keepdims=True) + NORM_EPS) * g


def _sigmoid(z):
    return 0.5 + 0.5 * jnp.tanh(0.5 * z)


def _softplus(z):
    return jnp.maximum(z, 0.0) + jnp.log(1.0 + jnp.exp(-jnp.abs(z)))


CAST = "cast"
PACK_T = "pack_t"

D_MODEL = 2048
GLA_IN = 2 * GLA_QK + 2 * GLA_V + GLA_LORA
RWKV_IN = 3 * RWKV_W + DECAY_LORA + AAA_LORA + GATE_LORA
GATE_COL = RW_GROUP + GLA_GROUP
PACKED_COLS = GATE_COL + 2 * D_MODEL

IN_PIECES = (
    (0, GLA_IN, 3 * RWKV_W),
    (RW_WD, GLA_IN + 3 * RWKV_W, DECAY_LORA),
    (RW_AD, GLA_IN + 3 * RWKV_W + DECAY_LORA, AAA_LORA),
    (RW_GD, GLA_IN + 3 * RWKV_W + DECAY_LORA + AAA_LORA, GATE_LORA),
    (RW_GROUP, 0, GLA_IN),
    (GATE_COL, GLA_IN + RWKV_IN, 2 * D_MODEL),
)


def _pack_cols(w):
    zeros = lambda n: jnp.zeros((w.shape[0], n), w.dtype)
    parts, pos = [], 0
    for start, w_start, width in IN_PIECES:
        parts += [zeros(start - pos), w[:, w_start:w_start + width]]
        pos = start + width
    assert pos == PACKED_COLS
    return jnp.concatenate([p for p in parts if p.shape[1]], axis=1)


def _pack_src(k):
    col = k * LANE
    src = jnp.zeros_like(col)
    valid = jnp.zeros_like(col)
    for start, w_start, width in IN_PIECES:
        inside = (col >= start) & (col < start + width)
        src = jnp.where(inside, w_start + col - start, src)
        valid = jnp.where(inside, jnp.minimum(start + width - col, LANE), valid)
    return src, valid


def _aux_plan(aux, nj, steps):
    in_specs, operands, out_specs, out_shapes, meta = [], [], [], [], []
    for kind, w in aux:
        rows, cols = w.shape
        if kind == CAST:
            rb = _rows_per_block(rows, steps)
            n_active = rows // rb
            index = lambda i, j, n=n_active: (jnp.minimum(i * nj + j, n - 1), 0)
            in_specs.append(pl.BlockSpec((rb, cols), index))
            operands.append(w)
            out_specs.append(pl.BlockSpec((rb, cols), index))
            out_shapes.append(jax.ShapeDtypeStruct((rows, cols), BF16))
            meta.append((kind, n_active, 1, 1))
        else:
            n_blocks = PACKED_COLS // LANE
            per_step = next(d for d in range(1, n_blocks + 1) if n_blocks % d == 0 and n_blocks // d <= steps)
            n_active = n_blocks // per_step
            last = n_active - 1
            for q in range(per_step):
                src = lambda i, j, q=q, last=last, per_step=per_step: (
                    pl.multiple_of(_pack_src(jnp.minimum(i * nj + j, last) * per_step + q)[0],
                                   BF16_SUBLANES), 0)
                in_specs.append(pl.BlockSpec((pl.Element(LANE), pl.Element(cols)), src))
                operands.append(w)
            out_specs.append(pl.BlockSpec((cols, per_step * LANE),
                                          lambda i, j, last=last: (0, jnp.minimum(i * nj + j, last))))
            out_shapes.append(jax.ShapeDtypeStruct((cols, PACKED_COLS), BF16))
            meta.append((kind, n_active, per_step, 1))
    return in_specs, operands, out_specs, out_shapes, tuple(meta)


def _aux_run(meta, in_refs, out_refs):
    step = pl.program_id(0) * pl.num_programs(1) + pl.program_id(1)
    ki = ko = 0
    for kind, n_active, n_in, n_out in meta:
        ins, outs = in_refs[ki:ki + n_in], out_refs[ko:ko + n_out]
        ki += n_in
        ko += n_out
        if kind == CAST:
            outs[0][...] = ins[0][...].astype(BF16)
        else:
            block = jnp.minimum(step, n_active - 1) * n_in
            for q, w_ref in enumerate(ins):
                w = w_ref[...]
                _, valid = _pack_src(block + q)
                row = lax.broadcasted_iota(jnp.int32, w.shape, 0)
                w = jnp.where(row < valid, w, 0.0)
                outs[0][:, q * LANE:(q + 1) * LANE] = w.T.astype(BF16)


RESIDUAL = "residual"
FINAL = "final"
EMIT_NORM = "emit"


def _ffn_kernel(*refs, mode, aux_meta):
    x_ref, g_ref, wg_ref, wu_ref, wd_ref, fn_ref = refs[:6]
    n_aux = sum(m[2] for m in aux_meta)
    n_main_out = 2 if mode == EMIT_NORM else 1
    o_ref = refs[6 + n_aux]
    h_scr = refs[-1]
    j = pl.program_id(1)

    @pl.when(j == 0)
    def _():
        h_scr[...] = _rmsnorm(x_ref[...], g_ref[...]).astype(BF16)
        o_ref[...] = jnp.zeros_like(o_ref)

    _aux_run(aux_meta, refs[6:6 + n_aux], refs[6 + n_aux + n_main_out:-1])
    h = h_scr[...]
    a = _dot(h, wg_ref[...])
    u = _dot(h, wu_ref[...])
    act = (a * _sigmoid(a) * u).astype(BF16)
    o_ref[...] += _dot(act, wd_ref[...])

    @pl.when(j == pl.num_programs(1) - 1)
    def _():
        y = x_ref[...] + 0.5 * o_ref[...]
        if mode == FINAL:
            y = _rmsnorm(y, fn_ref[...])
        o_ref[...] = y
        if mode == EMIT_NORM:
            refs[7 + n_aux][...] = _rmsnorm(y, fn_ref[...]).astype(BF16)


def _ffn(x, g, wg, wu, wd, fn, *, mode, aux=(), tm=FFN_TM, tf=FFN_TF):
    T, D = x.shape
    FF = wg.shape[1]
    nj = FF // tf
    aux_in, aux_ops, aux_out, aux_shapes, aux_meta = _aux_plan(aux, nj, (T // tm) * nj)
    tok = lambda: pl.BlockSpec((tm, D), lambda i, j: (i, 0))
    main_shapes = [jax.ShapeDtypeStruct((T, D), F32)]
    if mode == EMIT_NORM:
        main_shapes.append(jax.ShapeDtypeStruct((T, D), BF16))
    main_specs = [tok() for _ in main_shapes]
    return pl.pallas_call(
        functools.partial(_ffn_kernel, mode=mode, aux_meta=aux_meta),
        out_shape=main_shapes + aux_shapes,
        grid=(T // tm, nj),
        in_specs=[
            tok(),
            pl.BlockSpec((1, D), lambda i, j: (0, 0)),
            pl.BlockSpec((D, tf), lambda i, j: (0, j)),
            pl.BlockSpec((D, tf), lambda i, j: (0, j)),
            pl.BlockSpec((tf, D), lambda i, j: (j, 0)),
            pl.BlockSpec((1, D), lambda i, j: (0, 0)),
        ] + aux_in,
        out_specs=main_specs + aux_out,
        scratch_shapes=[pltpu.VMEM((tm, D), BF16)],
        compiler_params=_cparams(("arbitrary", "arbitrary")),
        name="ffn_" + mode,
    )(x, g, wg, wu, wd, fn, *aux_ops)


def _proj_kernel(*refs, aux_meta):
    h_ref, w_ref = refs[:2]
    n_aux = sum(m[2] for m in aux_meta)
    _aux_run(aux_meta, refs[2:2 + n_aux], refs[3 + n_aux:])
    refs[2 + n_aux][...] = _dot(h_ref[...], w_ref[...])


def _proj(h, w, *, aux=(), tm=PROJ_TM, tn=PROJ_TN):
    T, D = h.shape
    N = w.shape[1]
    nj = N // tn
    aux_in, aux_ops, aux_out, aux_shapes, aux_meta = _aux_plan(aux, nj, (T // tm) * nj)
    return pl.pallas_call(
        functools.partial(_proj_kernel, aux_meta=aux_meta),
        out_shape=[jax.ShapeDtypeStruct((T, N), F32)] + aux_shapes,
        grid=(T // tm, nj),
        in_specs=[
            pl.BlockSpec((tm, D), lambda i, j: (i, 0), pipeline_mode=pl.Buffered(1)),
            pl.BlockSpec((D, tn), lambda i, j: (0, j)),
        ] + aux_in,
        out_specs=[pl.BlockSpec((tm, tn), lambda i, j: (i, j))] + aux_out,
        compiler_params=_cparams(("arbitrary", "arbitrary")),
        name="in_proj",
    )(h, w, *aux_ops)


def _chunk_tri(n):
    r = lax.broadcasted_iota(jnp.int32, (n, n), 0)
    c = lax.broadcasted_iota(jnp.int32, (n, n), 1)
    return jnp.where((r >= c) & (r // CHUNK == c // CHUNK), 1.0, 0.0).astype(BF16)


def _gla_kernel(p_ref, wa2_ref, ba_ref, gn_ref, o_ref, st_scr, *, tg):
    @pl.when(pl.program_id(1) == 0)
    def _():
        st_scr[...] = jnp.zeros_like(st_scr)

    gad = p_ref[:, GLA_AD:GLA_AD + LANE].astype(BF16)
    z = _dot(gad, wa2_ref[...]) + ba_ref[...]
    log_alpha = -_softplus(-z) * (1.0 / GLA_TAU)
    tri = _chunk_tri(tg)
    hi = log_alpha.astype(BF16)
    lo = (log_alpha - hi.astype(F32)).astype(BF16)
    cum = _dot(tri, hi) + _dot(tri, lo)
    gn = gn_ref[...]
    scale = GLA_DK ** -0.5

    n_chunks = tg // CHUNK
    rows = [slice(c * CHUNK, (c + 1) * CHUNK) for c in range(n_chunks)]
    keys = [slice(h * GLA_DK, (h + 1) * GLA_DK) for h in range(GLA_HEADS)]
    vals = [slice(h * GLA_DV, (h + 1) * GLA_DV) for h in range(GLA_HEADS)]
    units = [(c, h) for c in range(n_chunks) for h in range(GLA_HEADS)]
    tot = [cum[r, :][CHUNK - 1:CHUNK, :] for r in rows]
    etot = [jnp.exp(t) for t in tot]
    kdec = [(p_ref[rows[c], GLA_QK:2 * GLA_QK] * jnp.exp(tot[c] - cum[rows[c], :])).astype(BF16)
            for c in range(n_chunks)]
    q = [(p_ref[r, 0:GLA_QK] * scale).astype(BF16) for r in rows]
    inc = {}
    for c, h in units:
        v_h = p_ref[rows[c], 2 * GLA_QK + h * GLA_DV:2 * GLA_QK + (h + 1) * GLA_DV].astype(BF16)
        inc[c, h] = _dot_tn(v_h, kdec[c][:, keys[h]])
    st = {}
    for h in range(GLA_HEADS):
        prev = st_scr[h]
        for c in range(n_chunks):
            prev = prev * etot[c][:, keys[h]] + inc[c, h]
            st[c, h] = prev
        st_scr[h] = prev
    out = {u: _dot_nt(q[u[0]][:, keys[u[1]]], st[u].astype(BF16)) for u in units}
    for c, h in units:
        o = out[c, h]
        r_h = p_ref[rows[c], 2 * GLA_QK + GLA_V + h * GLA_DV:2 * GLA_QK + GLA_V + (h + 1) * GLA_DV]
        o = o * lax.rsqrt(jnp.mean(o * o, axis=-1, keepdims=True) + NORM_EPS) * gn
        o = o * (r_h * _sigmoid(r_h))
        o_ref[rows[c], vals[h]] = o.astype(BF16)


def _gla(p, wa2, ba, gn, B, S, *, tg=256):
    T = B * S
    nb = S // tg
    return pl.pallas_call(
        functools.partial(_gla_kernel, tg=tg),
        out_shape=jax.ShapeDtypeStruct((T, GLA_V), BF16),
        grid=(B, nb),
        in_specs=[
            pl.BlockSpec((tg, GLA_GROUP), lambda b, i: (b * nb + i, 1)),
            pl.BlockSpec((LANE, GLA_QK), lambda b, i: (0, 0)),
            pl.BlockSpec((1, GLA_QK), lambda b, i: (0, 0)),
            pl.BlockSpec((1, GLA_DV), lambda b, i: (0, 0)),
        ],
        out_specs=pl.BlockSpec((tg, GLA_V), lambda b, i: (b * nb + i, 0)),
        scratch_shapes=[pltpu.VMEM((GLA_HEADS, GLA_DV, GLA_DK), F32)],
        compiler_params=_cparams(("parallel", "arbitrary")),
        name="gla",
    )(p, wa2, ba, gn)


MXU_DIM = 256


def _head_ones(n):
    r = lax.broadcasted_iota(jnp.int32, (n, n), 0)
    c = lax.broadcasted_iota(jnp.int32, (n, n), 1)
    return jnp.where(r // RWKV_HD == c // RWKV_HD, 1.0, 0.0).astype(BF16)


def _head_sum(x, ones_blk):
    xb = x.astype(BF16)
    parts = [_dot(xb[:, j:j + MXU_DIM], ones_blk) for j in range(0, x.shape[1], MXU_DIM)]
    return jnp.concatenate(parts, axis=1)


def _rwkv_prep_kernel(p_ref, mu_ref, w0_ref, ww2_ref, a0_ref, wa2_ref, wg2_ref, kk_ref, ka_ref, rk_ref,
                      rt_ref, bt_ref, at_ref, kt_ref, v_ref, bonus_ref, g_ref, ptot_ref,
                      carry_scr, *, tm):
    @pl.when(pl.program_id(1) == 0)
    def _():
        carry_scr[...] = jnp.zeros_like(carry_scr)

    p = p_ref[...]
    last = carry_scr[...]
    carry_scr[...] = p[tm - 1:tm, :]
    prev = pltpu.roll(p, 1, axis=0)
    sub = 8
    row = lax.broadcasted_iota(jnp.int32, (sub, p.shape[1]), 0)
    prev = jnp.concatenate([jnp.where(row == 0, last, prev[:sub]), prev[sub:]], axis=0)
    p = p + mu_ref[...] * (prev - p)

    r = p[:, 0:RWKV_W]
    k = p[:, RWKV_W:2 * RWKV_W]
    v = p[:, 2 * RWKV_W:3 * RWKV_W]
    wd = p[:, RW_WD:RW_WD + LANE]
    ad = p[:, RW_AD:RW_AD + LANE]
    gd = p[:, RW_GD:RW_GD + GATE_LORA]

    w_raw = w0_ref[...] + _dot(jnp.tanh(wd).astype(BF16), ww2_ref[...])
    log_w = (-math.exp(-0.5)) * _sigmoid(w_raw)
    a = _sigmoid(a0_ref[...] + _dot(ad.astype(BF16), wa2_ref[...]))
    g_ref[...] = _dot(_sigmoid(gd).astype(BF16), wg2_ref[...])

    ones_blk = _head_ones(MXU_DIM)
    kk = k * kk_ref[...]
    kk = kk * lax.rsqrt(jnp.maximum(_head_sum(kk * kk, ones_blk), 1e-24))
    kp = k * (1.0 + (a - 1.0) * ka_ref[...])
    bonus_ref[...] = _head_sum(r * kp * rk_ref[...], ones_blk) * v
    v_ref[...] = v.astype(BF16)

    tri = _chunk_tri(tm)
    hi = log_w.astype(BF16)
    lo = (log_w - hi.astype(F32)).astype(BF16)
    cum = _dot(tri, hi) + _dot(tri, lo)
    nalpha = -(kk * a)
    for ci in range(tm // CHUNK):
        rows = slice(ci * CHUNK, (ci + 1) * CHUNK)
        cum_c = cum[rows, :]
        tot = cum_c[CHUNK - 1:CHUNK, :]
        e_inv = jnp.exp(-cum_c)
        rt_ref[rows, :] = (r[rows, :] * jnp.exp(cum_c)).astype(BF16)
        bt_ref[rows, :] = (kk[rows, :] * jnp.exp(cum_c - log_w[rows, :])).astype(BF16)
        at_ref[rows, :] = (nalpha[rows, :] * e_inv).astype(BF16)
        kt_ref[rows, :] = (kp[rows, :] * e_inv).astype(BF16)
        ptot_ref[ci] = jnp.exp(tot)


def _rwkv_prep(p, mu, w0, ww2, a0, wa2, wg2, k_k, k_a, r_k, B, S, *, tm=256):
    T = B * S
    nb = S // tm
    cpt = tm // CHUNK
    vec = lambda n: pl.BlockSpec((1, n), lambda b, i: (0, 0))
    tok = lambda: pl.BlockSpec((tm, RWKV_W), lambda b, i: (b * nb + i, 0))
    bf = jax.ShapeDtypeStruct((T, RWKV_W), BF16)
    f32 = jax.ShapeDtypeStruct((T, RWKV_W), F32)
    return pl.pallas_call(
        functools.partial(_rwkv_prep_kernel, tm=tm),
        out_shape=[bf, bf, bf, bf, bf, f32, f32,
                   jax.ShapeDtypeStruct((T // CHUNK, 1, RWKV_W), F32)],
        grid=(B, nb),
        in_specs=[
            pl.BlockSpec((tm, RW_GROUP), lambda b, i: (b * nb + i, 0)),
            vec(RW_GROUP), vec(RWKV_W),
            pl.BlockSpec((LANE, RWKV_W), lambda b, i: (0, 0)),
            vec(RWKV_W),
            pl.BlockSpec((LANE, RWKV_W), lambda b, i: (0, 0)),
            pl.BlockSpec((GATE_LORA, RWKV_W), lambda b, i: (0, 0)),
            vec(RWKV_W), vec(RWKV_W), vec(RWKV_W),
        ],
        out_specs=[tok() for _ in range(7)]
        + [pl.BlockSpec((cpt, 1, RWKV_W), lambda b, i: (b * nb + i, 0, 0))],
        scratch_shapes=[pltpu.VMEM((1, RW_GROUP), F32)],
        compiler_params=_cparams(("parallel", "arbitrary")),
        name="rwkv_prep",
    )(p, mu, w0, ww2, a0, wa2, wg2, k_k, k_a, r_k)


def _rwkv_core_kernel(rt_ref, bt_ref, at_ref, kt_ref, v_ref, bonus_ref, g_ref, ptot_ref,
                      lw_ref, lb_ref, o_ref, h_scr):
    @pl.when(pl.program_id(0) == 0)
    def _():
        h_scr[...] = jnp.zeros_like(h_scr)

    assert CHUNK == RWKV_HD
    n_batch = rt_ref.shape[0]
    ti = lax.broadcasted_iota(jnp.int32, (CHUNK, PAIR), 0)
    si = lax.broadcasted_iota(jnp.int32, (CHUNK, PAIR), 1) % RWKV_HD
    strict = ti > si
    lower = ti >= si
    eye = ti == si
    blk8 = (ti // 8) == (si // 8)
    eye_f = jnp.where(eye, 1.0, 0.0)
    ri = lax.broadcasted_iota(jnp.int32, (PAIR, PAIR), 0)
    ci = lax.broadcasted_iota(jnp.int32, (PAIR, PAIR), 1)
    head_blk = (ri // RWKV_HD) == (ci // RWKV_HD)
    ones_pair = jnp.where(head_blk, 1.0, 0.0).astype(BF16)
    ones_2 = jnp.concatenate([ones_pair, ones_pair], axis=0)
    zeros_b = jnp.zeros((PAIR, PAIR), BF16)

    def bd(x):
        return jnp.where(head_blk, jnp.concatenate([x, x], axis=0), jnp.zeros((), x.dtype))

    def head_t(x):
        xt = x.T
        return jnp.concatenate([xt[:RWKV_HD], xt[RWKV_HD:]], axis=1)

    def dot_packed(lhs, rhs_bd):
        out = []
        for j in range(0, len(lhs), 2):
            l2 = jnp.concatenate([lhs[j], lhs[j + 1]], axis=1)
            r2 = jnp.concatenate([jnp.concatenate([rhs_bd[j], zeros_b], axis=1),
                                  jnp.concatenate([zeros_b, rhs_bd[j + 1]], axis=1)], axis=0)
            o = _dot(l2, r2)
            out += [o[:, :PAIR], o[:, PAIR:]]
        return out

    bf = lambda xs: [x.astype(BF16) for x in xs]

    def head_sums(xs):
        hi = [x.astype(BF16) for x in xs]
        lo = [(x - h.astype(F32)).astype(BF16) for x, h in zip(xs, hi)]
        rows = jnp.concatenate([jnp.concatenate([h, l], axis=1) for h, l in zip(hi, lo)], axis=0)
        sums = _dot(rows, ones_2)
        return [sums[j * CHUNK:(j + 1) * CHUNK] for j in range(len(xs))]

    pairs = range(n_batch * N_PAIR)
    where = [(b, slice(j * PAIR, (j + 1) * PAIR)) for b in range(n_batch) for j in range(N_PAIR)]
    ptot = [ptot_ref[b, 0, :, l] for b, l in where]
    bt = [bt_ref[b, :, l] for b, l in where]
    rt = [rt_ref[b, :, l] for b, l in where]
    at = [at_ref[b, :, l] for b, l in where]
    kt = [kt_ref[b, :, l] for b, l in where]
    bx = [bd(x) for x in bt]
    vx = [bd(v_ref[b, :, l]) for b, l in where]
    ak = [jnp.concatenate([bd(at[j]), bd(kt[j])], axis=0) for j in pairs]
    ak2t = [jnp.concatenate([head_t(at[j].astype(F32) * ptot[j]).astype(BF16),
                             head_t(kt[j].astype(F32) * ptot[j]).astype(BF16)], axis=1) for j in pairs]

    g = [_dot_nt(jnp.concatenate([bt[j], rt[j]], axis=0), ak[j]) for j in pairs]
    gb = [x[:CHUNK] for x in g]
    gr = [x[CHUNK:] for x in g]
    a_ab = [jnp.where(strict, x[:, :PAIR], 0.0) for x in gb]
    a_kb = [jnp.where(strict, x[:, PAIR:], 0.0).astype(BF16) for x in gb]
    a_r = [jnp.concatenate([jnp.where(lower, x[:, :PAIR], 0.0),
                            jnp.where(lower, x[:, PAIR:], 0.0)], axis=1).astype(BF16) for x in gr]

    a_d = [jnp.where(blk8, x, 0.0) for x in a_ab]
    a_db = bf(a_d)
    pw = bf(dot_packed(a_db, [bd(x) for x in a_db]))
    s = [eye_f + x for x in a_d]
    sp = [_dot(pw[j], jnp.concatenate([bd(s[j].astype(BF16)), bd(pw[j])], axis=1)) for j in pairs]
    s = [s[j] + sp[j][:, :PAIR] for j in pairs]
    pw = [x[:, PAIR:].astype(BF16) for x in sp]
    ps = dot_packed(pw, [bd(x) for x in bf(s)])
    s = [s[j] + ps[j] for j in pairs]
    for width in (8, 16, 32):
        off = ((ti // (2 * width)) == (si // (2 * width))) & ((ti // width) != (si // width))
        e = [jnp.where(off, x, 0.0).astype(BF16) for x in a_ab]
        sb = bf(s)
        es = bf(dot_packed(e, [bd(x) for x in sb]))
        ses = dot_packed(sb, [bd(x) for x in es])
        s = [s[j] + ses[j] for j in pairs]
    t_inv = bf(s)

    kv = bf(dot_packed(a_kb, vx))
    wu = bf([_dot(t_inv[j], jnp.concatenate([bx[j], bd(kv[j])], axis=1)) for j in pairs])
    z = [jnp.concatenate([jnp.concatenate([bd(wu[j][:, :PAIR]), bd(wu[j][:, PAIR:])], axis=1),
                          jnp.concatenate([zeros_b, vx[j]], axis=1)], axis=0) for j in pairs]
    mcqy = [_dot(jnp.concatenate([ak2t[j], a_r[j]], axis=0), z[j]) for j in pairs]
    mc = [x[:CHUNK] for x in mcqy]
    qy = [x[CHUNK:] for x in mcqy]

    m = [(mc[j][:, :PAIR] + jnp.where(eye, ptot[j], 0.0)).astype(BF16) for j in pairs]
    q = [(qy[j][:, :PAIR] + rt[j].astype(F32)).astype(BF16) for j in pairs]
    hb = [bd(h_scr[j].astype(BF16)) for j in pairs]
    qmh = dot_packed([jnp.concatenate([q[j], m[j]], axis=0) for j in pairs], hb)
    for j in pairs:
        h_scr[j] = qmh[j][CHUNK:] + mc[j][:, PAIR:]
    y = [qmh[j][:CHUNK] + qy[j][:, PAIR:] for j in pairs]

    mean = [x * (1.0 / RWKV_HD) for x in head_sums(y)]
    yc = [y[j] - mean[j] for j in pairs]
    var = [x * (1.0 / RWKV_HD) for x in head_sums([x * x for x in yc])]
    for j in pairs:
        b, l = where[j]
        yn = yc[j] * lax.rsqrt(var[j] + GN_EPS) * lw_ref[:, l] + lb_ref[:, l]
        o_ref[b, :, l] = ((yn + bonus_ref[b, :, l]) * g_ref[b, :, l]).astype(BF16)


def _rwkv_core(rt, bt, at, kt, v, bonus, g, ptot, lnx_w, lnx_b, B, S):
    nc = S // CHUNK
    seq = lambda a: a.reshape(B, S, RWKV_W)
    tok = lambda: pl.BlockSpec((B, CHUNK, RWKV_W), lambda c: (0, c, 0))
    vec = lambda: pl.BlockSpec((1, RWKV_W), lambda c: (0, 0))
    out = pl.pallas_call(
        _rwkv_core_kernel,
        out_shape=jax.ShapeDtypeStruct((B, S, RWKV_W), BF16),
        grid=(nc,),
        in_specs=[tok() for _ in range(7)]
        + [pl.BlockSpec((B, 1, 1, RWKV_W), lambda c: (0, c, 0, 0)), vec(), vec()],
        out_specs=tok(),
        scratch_shapes=[pltpu.VMEM((B * N_PAIR, RWKV_HD, PAIR), F32)],
        compiler_params=_cparams(("arbitrary",)),
        name="rwkv_core",
    )(seq(rt), seq(bt), seq(at), seq(kt), seq(v), seq(bonus), seq(g),
      ptot.reshape(B, nc, 1, RWKV_W), lnx_w, lnx_b)
    return out.reshape(B * S, RWKV_W)


def _merge_kernel(x_ref, og_ref, or_ref, gb_ref, wb1_ref, wb2_ref, wo_ref, *rest):
    o_ref = rest[-1]
    gate_refs = rest[:-1]
    n = len(gate_refs) // 2
    tn = gate_refs[0].shape[1]
    D = x_ref.shape[1]
    y_gla = _dot(og_ref[...], wb1_ref[...])
    y_rw = _dot(or_ref[...], wb2_ref[...])
    merged = []
    for c in range(n):
        cols = slice(c * tn, (c + 1) * tn)
        g_gla = _sigmoid(gate_refs[c][...] + gb_ref[:, c * tn:(c + 1) * tn])
        g_rw = _sigmoid(gate_refs[n + c][...] + gb_ref[:, D + c * tn:D + (c + 1) * tn])
        merged.append((g_gla * y_gla[:, cols] + g_rw * y_rw[:, cols]).astype(BF16))
    o_ref[...] = x_ref[...] + _dot(jnp.concatenate(merged, axis=1), wo_ref[...])


def _merge(x, o_gla, o_rw, p, gate_b, w_branch, wo, *, tm=256, tn=PROJ_TN):
    T, D = x.shape
    const = lambda shape, r: pl.BlockSpec(shape, lambda i: (r, 0), pipeline_mode=pl.Buffered(1))
    assert GLA_V == RWKV_W and GATE_COL % tn == 0 and D % tn == 0
    n_gate_blocks = 2 * D // tn
    gate_specs = [pl.BlockSpec((tm, tn), lambda i, c=GATE_COL // tn + c: (i, c)) for c in range(n_gate_blocks)]
    return pl.pallas_call(
        _merge_kernel,
        out_shape=jax.ShapeDtypeStruct((T, D), F32),
        grid=(T // tm,),
        in_specs=[
            pl.BlockSpec((tm, D), lambda i: (i, 0)),
            pl.BlockSpec((tm, GLA_V), lambda i: (i, 0)),
            pl.BlockSpec((tm, RWKV_W), lambda i: (i, 0)),
            pl.BlockSpec((1, 2 * D), lambda i: (0, 0)),
            const((GLA_V, D), 0), const((RWKV_W, D), 1), const((D, D), 0),
        ] + gate_specs,
        out_specs=pl.BlockSpec((tm, D), lambda i: (i, 0)),
        compiler_params=_cparams(("parallel",)),
        name="merge_out",
    )(x, o_gla, o_rw, gate_b, w_branch, w_branch, wo, *([p] * n_gate_blocks))


def _pad_rows(w, n):
    return jnp.pad(w, ((0, n - w.shape[0]), (0, 0)))


def kernel(x, ffn1_norm, ffn1_wg, ffn1_wu, ffn1_wd, mix_norm, w_in, gla_w_a2, gla_b_a, gla_gn_w, rwkv_mu,
           rwkv_w0, rwkv_w_w2, rwkv_a0, rwkv_w_a2, rwkv_w_g2, rwkv_k_k, rwkv_k_a, rwkv_r_k, rwkv_lnx_w,
           rwkv_lnx_b, gate_b, w_branch, w_out, ffn2_norm, ffn2_wg, ffn2_wu, ffn2_wd, final_norm):
    B, S, D = x.shape
    T = B * S
    depth = ffn1_norm.shape[0]
    assert depth >= 1
    bf = lambda w: w.astype(BF16)
    row = lambda v: v.reshape(1, -1)
    assert D == D_MODEL and w_in.shape[2] == GLA_IN + RWKV_IN + 2 * D_MODEL
    xt = x.reshape(T, D)
    for l in range(depth):
        last = l == depth - 1
        xt, h_mix, w_p, w_br, w_o = _ffn(
            xt, row(ffn1_norm[l]), bf(ffn1_wg[l]), bf(ffn1_wu[l]), bf(ffn1_wd[l]), row(mix_norm[l]),
            mode=EMIT_NORM,
            aux=[(PACK_T, w_in[l].T), (CAST, w_branch[l]), (CAST, w_out[l])])

        mu_row = jnp.pad(rwkv_mu[l], (GLA_IN, 2 * D_MODEL))[None, :]
        mu = _pack_cols(mu_row)[:, :RW_GROUP]
        p, wg2, wu2, wd2 = _proj(h_mix, w_p,
                                 aux=[(CAST, ffn2_wg[l]), (CAST, ffn2_wu[l]), (CAST, ffn2_wd[l])])

        o_gla = _gla(p, bf(_pad_rows(gla_w_a2[l], LANE)), row(gla_b_a[l]), row(gla_gn_w[l]), B, S)

        prep = _rwkv_prep(p, mu, row(rwkv_w0[l]), bf(_pad_rows(rwkv_w_w2[l], LANE)), row(rwkv_a0[l]),
                          bf(_pad_rows(rwkv_w_a2[l], LANE)), bf(rwkv_w_g2[l]), row(rwkv_k_k[l]),
                          row(rwkv_k_a[l]), row(rwkv_r_k[l]), B, S)
        o_rw = _rwkv_core(*prep, row(rwkv_lnx_w[l]), row(rwkv_lnx_b[l]), B, S)

        xt = _merge(xt, o_gla, o_rw, p, row(gate_b[l]), w_br, w_o)

        xt = _ffn(xt, row(ffn2_norm[l]), wg2, wu2, wd2, row(final_norm),
                  mode=FINAL if last else RESIDUAL)[0]
    return xt.reshape(B, S, D)
```

```python
import functools
import math

import jax
import jax.numpy as jnp
from jax import lax
from jax.experimental import pallas as pl
from jax.experimental.pallas import tpu as pltpu

F32 = jnp.float32
BF16 = jnp.bfloat16

NORM_EPS = 1e-6
GN_EPS = 64e-5
GLA_TAU = 16.0
CHUNK = 64

GLA_HEADS = 4
GLA_DK = 128
GLA_DV = 256
GLA_QK = GLA_HEADS * GLA_DK
GLA_V = GLA_HEADS * GLA_DV
GLA_LORA = 16

RWKV_HD = 64
RWKV_W = 1024
DECAY_LORA = 96
AAA_LORA = 96
GATE_LORA = 256

LANE = 128
PAIR = 2 * RWKV_HD
N_PAIR = RWKV_W // PAIR

RW_GROUP = 3 * RWKV_W + LANE + LANE + GATE_LORA
GLA_GROUP = RW_GROUP
RW_WD = 3 * RWKV_W
GLA_AD = 2 * GLA_QK + 2 * GLA_V
RW_AD = RW_WD + LANE
RW_GD = RW_AD + LANE

VMEM_LIMIT = 60 * 1024 * 1024
BF16_SUBLANES = 16
F32_SUBLANES = 8

FFN_TM, FFN_TF = 512, 512
PROJ_TM, PROJ_TN = 2048, 1024


def _rows_per_block(rows, steps):
    for rb in range(BF16_SUBLANES, rows + 1, BF16_SUBLANES):
        if rows % rb == 0 and rows // rb <= steps:
            return rb
    raise ValueError(f"cannot split {rows} rows over {steps} steps")


def _cparams(sem):
    return pltpu.CompilerParams(dimension_semantics=sem, vmem_limit_bytes=VMEM_LIMIT)


def _dot(a, b):
    return jnp.dot(a, b, preferred_element_type=F32)


def _dot_nt(a, b):
    return lax.dot_general(a, b, (((1,), (1,)), ((), ())), preferred_element_type=F32)


def _dot_tn(a, b):
    return lax.dot_general(a, b, (((0,), (0,)), ((), ())), preferred_element_type=F32)


def _rmsnorm(x, g):
    return x * lax.rsqrt(jnp.mean(x * x, axis=-1, keepdims=True) + NORM_EPS) * g


def _sigmoid(z):
    return 0.5 + 0.5 * jnp.tanh(0.5 * z)


def _softplus(z):
    return jnp.maximum(z, 0.0) + jnp.log(1.0 + jnp.exp(-jnp.abs(z)))


CAST = "cast"
PACK_T = "pack_t"

D_MODEL = 2048
GLA_IN = 2 * GLA_QK + 2 * GLA_V + GLA_LORA
RWKV_IN = 3 * RWKV_W + DECAY_LORA + AAA_LORA + GATE_LORA
GATE_COL = RW_GROUP + GLA_GROUP
PACKED_COLS = GATE_COL + 2 * D_MODEL

IN_PIECES = (
    (0, GLA_IN, 3 * RWKV_W),
    (RW_WD, GLA_IN + 3 * RWKV_W, DECAY_LORA),
    (RW_AD, GLA_IN + 3 * RWKV_W + DECAY_LORA, AAA_LORA),
    (RW_GD, GLA_IN + 3 * RWKV_W + DECAY_LORA + AAA_LORA, GATE_LORA),
    (RW_GROUP, 0, GLA_IN),
    (GATE_COL, GLA_IN + RWKV_IN, 2 * D_MODEL),
)


def _pack_cols(w):
    zeros = lambda n: jnp.zeros((w.shape[0], n), w.dtype)
    parts, pos = [], 0
    for start, w_start, width in IN_PIECES:
        parts += [zeros(start - pos), w[:, w_start:w_start + width]]
        pos = start + width
    assert pos == PACKED_COLS
    return jnp.concatenate([p for p in parts if p.shape[1]], axis=1)


def _pack_src(k):
    col = k * LANE
    src = jnp.zeros_like(col)
    valid = jnp.zeros_like(col)
    for start, w_start, width in IN_PIECES:
        inside = (col >= start) & (col < start + width)
        src = jnp.where(inside, w_start + col - start, src)
        valid = jnp.where(inside, jnp.minimum(start + width - col, LANE), valid)
    return src, valid


def _aux_plan(aux, nj, steps):
    in_specs, operands, out_specs, out_shapes, meta = [], [], [], [], []
    for kind, w in aux:
        rows, cols = w.shape
        if kind == CAST:
            rb = _rows_per_block(rows, steps)
            n_active = rows // rb
            index = lambda i, j, n=n_active: (jnp.minimum(i * nj + j, n - 1), 0)
            in_specs.append(pl.BlockSpec((rb, cols), index))
            operands.append(w)
            out_specs.append(pl.BlockSpec((rb, cols), index))
            out_shapes.append(jax.ShapeDtypeStruct((rows, cols), BF16))
            meta.append((kind, n_active, 1, 1))
        else:
            n_blocks = PACKED_COLS // LANE
            per_step = next(d for d in range(1, n_blocks + 1) if n_blocks % d == 0 and n_blocks // d <= steps)
            n_active = n_blocks // per_step
            last = n_active - 1
            for q in range(per_step):
                src = lambda i, j, q=q, last=last, per_step=per_step: (
                    pl.multiple_of(_pack_src(jnp.minimum(i * nj + j, last) * per_step + q)[0],
                                   BF16_SUBLANES), 0)
                in_specs.append(pl.BlockSpec((pl.Element(LANE), pl.Element(cols)), src))
                operands.append(w)
            out_specs.append(pl.BlockSpec((per_step * LANE, cols),
                                          lambda i, j, last=last: (jnp.minimum(i * nj + j, last), 0)))
            out_shapes.append(jax.ShapeDtypeStruct((PACKED_COLS, cols), BF16))
            meta.append((kind, n_active, per_step, 1))
    return in_specs, operands, out_specs, out_shapes, tuple(meta)


def _aux_run(meta, in_refs, out_refs):
    step = pl.program_id(0) * pl.num_programs(1) + pl.program_id(1)
    ki = ko = 0
    for kind, n_active, n_in, n_out in meta:
        ins, outs = in_refs[ki:ki + n_in], out_refs[ko:ko + n_out]
        ki += n_in
        ko += n_out
        if kind == CAST:
            outs[0][...] = ins[0][...].astype(BF16)
        else:
            block = jnp.minimum(step, n_active - 1) * n_in
            for q, w_ref in enumerate(ins):
                w = w_ref[...]
                _, valid = _pack_src(block + q)
                row = lax.broadcasted_iota(jnp.int32, w.shape, 0)
                outs[0][q * LANE:(q + 1) * LANE, :] = jnp.where(row < valid, w, 0.0).astype(BF16)


RESIDUAL = "residual"
FINAL = "final"
EMIT_NORM = "emit"


def _ffn_kernel(*refs, mode, aux_meta):
    x_ref, g_ref, wg_ref, wu_ref, wd_ref, fn_ref = refs[:6]
    n_aux = sum(m[2] for m in aux_meta)
    n_main_out = 2 if mode == EMIT_NORM else 1
    o_ref = refs[6 + n_aux]
    h_scr = refs[-1]
    j = pl.program_id(1)

    @pl.when(j == 0)
    def _():
        h_scr[...] = _rmsnorm(x_ref[...], g_ref[...]).astype(BF16)
        o_ref[...] = jnp.zeros_like(o_ref)

    _aux_run(aux_meta, refs[6:6 + n_aux], refs[6 + n_aux + n_main_out:-1])
    h = h_scr[...]
    a = _dot(h, wg_ref[...])
    u = _dot(h, wu_ref[...])
    act = (a * _sigmoid(a) * u).astype(BF16)
    o_ref[...] += _dot(act, wd_ref[...])

    @pl.when(j == pl.num_programs(1) - 1)
    def _():
        y = x_ref[...] + 0.5 * o_ref[...]
        if mode == FINAL:
            y = _rmsnorm(y, fn_ref[...])
        o_ref[...] = y
        if mode == EMIT_NORM:
            refs[7 + n_aux][...] = _rmsnorm(y, fn_ref[...]).astype(BF16)


def _ffn(x, g, wg, wu, wd, fn, *, mode, aux=(), tm=FFN_TM, tf=FFN_TF):
    T, D = x.shape
    FF = wg.shape[1]
    nj = FF // tf
    aux_in, aux_ops, aux_out, aux_shapes, aux_meta = _aux_plan(aux, nj, (T // tm) * nj)
    tok = lambda: pl.BlockSpec((tm, D), lambda i, j: (i, 0))
    main_shapes = [jax.ShapeDtypeStruct((T, D), F32)]
    if mode == EMIT_NORM:
        main_shapes.append(jax.ShapeDtypeStruct((T, D), BF16))
    main_specs = [tok() for _ in main_shapes]
    return pl.pallas_call(
        functools.partial(_ffn_kernel, mode=mode, aux_meta=aux_meta),
        out_shape=main_shapes + aux_shapes,
        grid=(T // tm, nj),
        in_specs=[
            tok(),
            pl.BlockSpec((1, D), lambda i, j: (0, 0)),
            pl.BlockSpec((D, tf), lambda i, j: (0, j)),
            pl.BlockSpec((D, tf), lambda i, j: (0, j)),
            pl.BlockSpec((tf, D), lambda i, j: (j, 0)),
            pl.BlockSpec((1, D), lambda i, j: (0, 0)),
        ] + aux_in,
        out_specs=main_specs + aux_out,
        scratch_shapes=[pltpu.VMEM((tm, D), BF16)],
        compiler_params=_cparams(("arbitrary", "arbitrary")),
        name="ffn_" + mode,
    )(x, g, wg, wu, wd, fn, *aux_ops)


def _proj_kernel(*refs, aux_meta):
    h_ref, w_ref = refs[:2]
    n_aux = sum(m[2] for m in aux_meta)
    _aux_run(aux_meta, refs[2:2 + n_aux], refs[3 + n_aux:])
    refs[2 + n_aux][...] = _dot_nt(h_ref[...], w_ref[...])


def _proj(h, w, *, aux=(), tm=PROJ_TM, tn=PROJ_TN):
    T, D = h.shape
    N = w.shape[0]
    nj = N // tn
    aux_in, aux_ops, aux_out, aux_shapes, aux_meta = _aux_plan(aux, nj, (T // tm) * nj)
    return pl.pallas_call(
        functools.partial(_proj_kernel, aux_meta=aux_meta),
        out_shape=[jax.ShapeDtypeStruct((T, N), F32)] + aux_shapes,
        grid=(T // tm, nj),
        in_specs=[
            pl.BlockSpec((tm, D), lambda i, j: (i, 0), pipeline_mode=pl.Buffered(1)),
            pl.BlockSpec((tn, D), lambda i, j: (j, 0)),
        ] + aux_in,
        out_specs=[pl.BlockSpec((tm, tn), lambda i, j: (i, j))] + aux_out,
        compiler_params=_cparams(("arbitrary", "arbitrary")),
        name="in_proj",
    )(h, w, *aux_ops)


def _chunk_tri(n):
    r = lax.broadcasted_iota(jnp.int32, (n, n), 0)
    c = lax.broadcasted_iota(jnp.int32, (n, n), 1)
    return jnp.where((r >= c) & (r // CHUNK == c // CHUNK), 1.0, 0.0).astype(BF16)


def _gla_kernel(p_ref, wa2_ref, ba_ref, gn_ref, o_ref, st_scr, *, tg):
    @pl.when(pl.program_id(1) == 0)
    def _():
        st_scr[...] = jnp.zeros_like(st_scr)

    gad = p_ref[:, GLA_AD:GLA_AD + LANE].astype(BF16)
    z = _dot(gad, wa2_ref[...]) + ba_ref[...]
    log_alpha = -_softplus(-z) * (1.0 / GLA_TAU)
    tri = _chunk_tri(tg)
    hi = log_alpha.astype(BF16)
    lo = (log_alpha - hi.astype(F32)).astype(BF16)
    cum = _dot(tri, hi) + _dot(tri, lo)
    gn = gn_ref[...]
    scale = GLA_DK ** -0.5

    n_chunks = tg // CHUNK
    rows = [slice(c * CHUNK, (c + 1) * CHUNK) for c in range(n_chunks)]
    keys = [slice(h * GLA_DK, (h + 1) * GLA_DK) for h in range(GLA_HEADS)]
    vals = [slice(h * GLA_DV, (h + 1) * GLA_DV) for h in range(GLA_HEADS)]
    units = [(c, h) for c in range(n_chunks) for h in range(GLA_HEADS)]
    tot = [cum[r, :][CHUNK - 1:CHUNK, :] for r in rows]
    etot = [jnp.exp(t) for t in tot]
    kdec = [(p_ref[rows[c], GLA_QK:2 * GLA_QK] * jnp.exp(tot[c] - cum[rows[c], :])).astype(BF16)
            for c in range(n_chunks)]
    q = [(p_ref[r, 0:GLA_QK] * scale).astype(BF16) for r in rows]
    inc = {}
    for c, h in units:
        v_h = p_ref[rows[c], 2 * GLA_QK + h * GLA_DV:2 * GLA_QK + (h + 1) * GLA_DV].astype(BF16)
        inc[c, h] = _dot_tn(v_h, kdec[c][:, keys[h]])
    st = {}
    for h in range(GLA_HEADS):
        prev = st_scr[h]
        for c in range(n_chunks):
            prev = prev * etot[c][:, keys[h]] + inc[c, h]
            st[c, h] = prev
        st_scr[h] = prev
    out = {u: _dot_nt(q[u[0]][:, keys[u[1]]], st[u].astype(BF16)) for u in units}
    for c, h in units:
        o = out[c, h]
        r_h = p_ref[rows[c], 2 * GLA_QK + GLA_V + h * GLA_DV:2 * GLA_QK + GLA_V + (h + 1) * GLA_DV]
        o = o * lax.rsqrt(jnp.mean(o * o, axis=-1, keepdims=True) + NORM_EPS) * gn
        o = o * (r_h * _sigmoid(r_h))
        o_ref[rows[c], vals[h]] = o.astype(BF16)


def _gla(p, wa2, ba, gn, B, S, *, tg=256):
    T = B * S
    nb = S // tg
    return pl.pallas_call(
        functools.partial(_gla_kernel, tg=tg),
        out_shape=jax.ShapeDtypeStruct((T, GLA_V), BF16),
        grid=(B, nb),
        in_specs=[
            pl.BlockSpec((tg, GLA_GROUP), lambda b, i: (b * nb + i, 1)),
            pl.BlockSpec((LANE, GLA_QK), lambda b, i: (0, 0)),
            pl.BlockSpec((1, GLA_QK), lambda b, i: (0, 0)),
            pl.BlockSpec((1, GLA_DV), lambda b, i: (0, 0)),
        ],
        out_specs=pl.BlockSpec((tg, GLA_V), lambda b, i: (b * nb + i, 0)),
        scratch_shapes=[pltpu.VMEM((GLA_HEADS, GLA_DV, GLA_DK), F32)],
        compiler_params=_cparams(("parallel", "arbitrary")),
        name="gla",
    )(p, wa2, ba, gn)


MXU_DIM = 256


def _head_ones(n):
    r = lax.broadcasted_iota(jnp.int32, (n, n), 0)
    c = lax.broadcasted_iota(jnp.int32, (n, n), 1)
    return jnp.where(r // RWKV_HD == c // RWKV_HD, 1.0, 0.0).astype(BF16)


def _head_sum(x, ones_blk):
    xb = x.astype(BF16)
    parts = [_dot(xb[:, j:j + MXU_DIM], ones_blk) for j in range(0, x.shape[1], MXU_DIM)]
    return jnp.concatenate(parts, axis=1)


def _rwkv_prep_kernel(p_ref, mu_ref, w0_ref, ww2_ref, a0_ref, wa2_ref, wg2_ref, kk_ref, ka_ref, rk_ref,
                      rt_ref, bt_ref, at_ref, kt_ref, v_ref, bonus_ref, g_ref, ptot_ref,
                      carry_scr, *, tm):
    @pl.when(pl.program_id(1) == 0)
    def _():
        carry_scr[...] = jnp.zeros_like(carry_scr)

    p = p_ref[...]
    last = carry_scr[...]
    carry_scr[...] = p[tm - 1:tm, :]
    prev = pltpu.roll(p, 1, axis=0)
    row = lax.broadcasted_iota(jnp.int32, (F32_SUBLANES, p.shape[1]), 0)
    prev = jnp.concatenate([jnp.where(row == 0, last, prev[:F32_SUBLANES]), prev[F32_SUBLANES:]], axis=0)
    p = p + mu_ref[...] * (prev - p)

    r = p[:, 0:RWKV_W]
    k = p[:, RWKV_W:2 * RWKV_W]
    v = p[:, 2 * RWKV_W:3 * RWKV_W]
    wd = p[:, RW_WD:RW_WD + LANE]
    ad = p[:, RW_AD:RW_AD + LANE]
    gd = p[:, RW_GD:RW_GD + GATE_LORA]

    w_raw = w0_ref[...] + _dot(jnp.tanh(wd).astype(BF16), ww2_ref[...])
    log_w = (-math.exp(-0.5)) * _sigmoid(w_raw)
    a = _sigmoid(a0_ref[...] + _dot(ad.astype(BF16), wa2_ref[...]))
    g_ref[...] = _dot(_sigmoid(gd).astype(BF16), wg2_ref[...])

    ones_blk = _head_ones(MXU_DIM)
    kk = k * kk_ref[...]
    kk = kk * lax.rsqrt(jnp.maximum(_head_sum(kk * kk, ones_blk), 1e-24))
    kp = k * (1.0 + (a - 1.0) * ka_ref[...])
    bonus_ref[...] = _head_sum(r * kp * rk_ref[...], ones_blk) * v
    v_ref[...] = v.astype(BF16)

    tri = _chunk_tri(tm)
    hi = log_w.astype(BF16)
    lo = (log_w - hi.astype(F32)).astype(BF16)
    cum = _dot(tri, hi) + _dot(tri, lo)
    nalpha = -(kk * a)
    for ci in range(tm // CHUNK):
        rows = slice(ci * CHUNK, (ci + 1) * CHUNK)
        cum_c = cum[rows, :]
        tot = cum_c[CHUNK - 1:CHUNK, :]
        e_inv = jnp.exp(-cum_c)
        rt_ref[rows, :] = (r[rows, :] * jnp.exp(cum_c)).astype(BF16)
        bt_ref[rows, :] = (kk[rows, :] * jnp.exp(cum_c - log_w[rows, :])).astype(BF16)
        at_ref[rows, :] = (nalpha[rows, :] * e_inv).astype(BF16)
        kt_ref[rows, :] = (kp[rows, :] * e_inv).astype(BF16)
        ptot_ref[ci] = jnp.exp(tot)


def _rwkv_prep(p, mu, w0, ww2, a0, wa2, wg2, k_k, k_a, r_k, B, S, *, tm=256):
    T = B * S
    nb = S // tm
    cpt = tm // CHUNK
    vec = lambda n: pl.BlockSpec((1, n), lambda b, i: (0, 0))
    tok = lambda: pl.BlockSpec((tm, RWKV_W), lambda b, i: (b * nb + i, 0))
    bf = jax.ShapeDtypeStruct((T, RWKV_W), BF16)
    f32 = jax.ShapeDtypeStruct((T, RWKV_W), F32)
    return pl.pallas_call(
        functools.partial(_rwkv_prep_kernel, tm=tm),
        out_shape=[bf, bf, bf, bf, bf, f32, f32,
                   jax.ShapeDtypeStruct((T // CHUNK, 1, RWKV_W), F32)],
        grid=(B, nb),
        in_specs=[
            pl.BlockSpec((tm, RW_GROUP), lambda b, i: (b * nb + i, 0)),
            vec(RW_GROUP), vec(RWKV_W),
            pl.BlockSpec((LANE, RWKV_W), lambda b, i: (0, 0)),
            vec(RWKV_W),
            pl.BlockSpec((LANE, RWKV_W), lambda b, i: (0, 0)),
            pl.BlockSpec((GATE_LORA, RWKV_W), lambda b, i: (0, 0)),
            vec(RWKV_W), vec(RWKV_W), vec(RWKV_W),
        ],
        out_specs=[tok() for _ in range(7)]
        + [pl.BlockSpec((cpt, 1, RWKV_W), lambda b, i: (b * nb + i, 0, 0))],
        scratch_shapes=[pltpu.VMEM((1, RW_GROUP), F32)],
        compiler_params=_cparams(("parallel", "arbitrary")),
        name="rwkv_prep",
    )(p, mu, w0, ww2, a0, wa2, wg2, k_k, k_a, r_k)


def _rwkv_core_kernel(rt_ref, bt_ref, at_ref, kt_ref, v_ref, bonus_ref, g_ref, ptot_ref,
                      lw_ref, lb_ref, o_ref, h_scr):
    @pl.when(pl.program_id(0) == 0)
    def _():
        h_scr[...] = jnp.zeros_like(h_scr)

    assert CHUNK == RWKV_HD
    n_batch = rt_ref.shape[0]
    ti = lax.broadcasted_iota(jnp.int32, (CHUNK, PAIR), 0)
    si = lax.broadcasted_iota(jnp.int32, (CHUNK, PAIR), 1) % RWKV_HD
    strict = ti > si
    lower = ti >= si
    eye = ti == si
    blk8 = (ti // 8) == (si // 8)
    eye_f = jnp.where(eye, 1.0, 0.0)
    ri = lax.broadcasted_iota(jnp.int32, (PAIR, PAIR), 0)
    ci = lax.broadcasted_iota(jnp.int32, (PAIR, PAIR), 1)
    head_blk = (ri // RWKV_HD) == (ci // RWKV_HD)
    ones_pair = jnp.where(head_blk, 1.0, 0.0).astype(BF16)
    ones_2 = jnp.concatenate([ones_pair, ones_pair], axis=0)
    zeros_b = jnp.zeros((PAIR, PAIR), BF16)

    def bd(x):
        return jnp.where(head_blk, jnp.concatenate([x, x], axis=0), jnp.zeros((), x.dtype))

    def head_t(x):
        xt = x.T
        return jnp.concatenate([xt[:RWKV_HD], xt[RWKV_HD:]], axis=1)

    def dot_packed(lhs, rhs_bd):
        out = []
        for j in range(0, len(lhs), 2):
            l2 = jnp.concatenate([lhs[j], lhs[j + 1]], axis=1)
            r2 = jnp.concatenate([jnp.concatenate([rhs_bd[j], zeros_b], axis=1),
                                  jnp.concatenate([zeros_b, rhs_bd[j + 1]], axis=1)], axis=0)
            o = _dot(l2, r2)
            out += [o[:, :PAIR], o[:, PAIR:]]
        return out

    bf = lambda xs: [x.astype(BF16) for x in xs]

    def head_sums(xs):
        hi = [x.astype(BF16) for x in xs]
        lo = [(x - h.astype(F32)).astype(BF16) for x, h in zip(xs, hi)]
        rows = jnp.concatenate([jnp.concatenate([h, l], axis=1) for h, l in zip(hi, lo)], axis=0)
        sums = _dot(rows, ones_2)
        return [sums[j * CHUNK:(j + 1) * CHUNK] for j in range(len(xs))]

    pairs = range(n_batch * N_PAIR)
    where = [(b, slice(j * PAIR, (j + 1) * PAIR)) for b in range(n_batch) for j in range(N_PAIR)]
    ptot = [ptot_ref[b, 0, :, l] for b, l in where]
    bt = [bt_ref[b, :, l] for b, l in where]
    rt = [rt_ref[b, :, l] for b, l in where]
    at = [at_ref[b, :, l] for b, l in where]
    kt = [kt_ref[b, :, l] for b, l in where]
    bx = [bd(x) for x in bt]
    vx = [bd(v_ref[b, :, l]) for b, l in where]
    ak = [jnp.concatenate([bd(at[j]), bd(kt[j])], axis=0) for j in pairs]
    ak2t = [jnp.concatenate([head_t(at[j].astype(F32) * ptot[j]).astype(BF16),
                             head_t(kt[j].astype(F32) * ptot[j]).astype(BF16)], axis=1) for j in pairs]

    g = [_dot_nt(jnp.concatenate([bt[j], rt[j]], axis=0), ak[j]) for j in pairs]
    gb = [x[:CHUNK] for x in g]
    gr = [x[CHUNK:] for x in g]
    a_ab = [jnp.where(strict, x[:, :PAIR], 0.0) for x in gb]
    a_kb = [jnp.where(strict, x[:, PAIR:], 0.0).astype(BF16) for x in gb]
    a_r = [jnp.concatenate([jnp.where(lower, x[:, :PAIR], 0.0),
                            jnp.where(lower, x[:, PAIR:], 0.0)], axis=1).astype(BF16) for x in gr]

    a_d = [jnp.where(blk8, x, 0.0) for x in a_ab]
    a_db = bf(a_d)
    pw = bf(dot_packed(a_db, [bd(x) for x in a_db]))
    s = [eye_f + x for x in a_d]
    sp = [_dot(pw[j], jnp.concatenate([bd(s[j].astype(BF16)), bd(pw[j])], axis=1)) for j in pairs]
    s = [s[j] + sp[j][:, :PAIR] for j in pairs]
    pw = [x[:, PAIR:].astype(BF16) for x in sp]
    ps = dot_packed(pw, [bd(x) for x in bf(s)])
    s = [s[j] + ps[j] for j in pairs]
    for width in (8, 16, 32):
        off = ((ti // (2 * width)) == (si // (2 * width))) & ((ti // width) != (si // width))
        e = [jnp.where(off, x, 0.0).astype(BF16) for x in a_ab]
        sb = bf(s)
        es = bf(dot_packed(e, [bd(x) for x in sb]))
        ses = dot_packed(sb, [bd(x) for x in es])
        s = [s[j] + ses[j] for j in pairs]
    t_inv = bf(s)

    kv = bf(dot_packed(a_kb, vx))
    wu = bf([_dot(t_inv[j], jnp.concatenate([bx[j], bd(kv[j])], axis=1)) for j in pairs])
    z = [jnp.concatenate([jnp.concatenate([bd(wu[j][:, :PAIR]), bd(wu[j][:, PAIR:])], axis=1),
                          jnp.concatenate([zeros_b, vx[j]], axis=1)], axis=0) for j in pairs]
    mcqy = [_dot(jnp.concatenate([ak2t[j], a_r[j]], axis=0), z[j]) for j in pairs]
    mc = [x[:CHUNK] for x in mcqy]
    qy = [x[CHUNK:] for x in mcqy]

    m = [(mc[j][:, :PAIR] + jnp.where(eye, ptot[j], 0.0)).astype(BF16) for j in pairs]
    q = [(qy[j][:, :PAIR] + rt[j].astype(F32)).astype(BF16) for j in pairs]
    hb = [bd(h_scr[j].astype(BF16)) for j in pairs]
    qmh = dot_packed([jnp.concatenate([q[j], m[j]], axis=0) for j in pairs], hb)
    for j in pairs:
        h_scr[j] = qmh[j][CHUNK:] + mc[j][:, PAIR:]
    y = [qmh[j][:CHUNK] + qy[j][:, PAIR:] for j in pairs]

    mean = [x * (1.0 / RWKV_HD) for x in head_sums(y)]
    yc = [y[j] - mean[j] for j in pairs]
    var = [x * (1.0 / RWKV_HD) for x in head_sums([x * x for x in yc])]
    for j in pairs:
        b, l = where[j]
        yn = yc[j] * lax.rsqrt(var[j] + GN_EPS) * lw_ref[:, l] + lb_ref[:, l]
        o_ref[b, :, l] = ((yn + bonus_ref[b, :, l]) * g_ref[b, :, l]).astype(BF16)


def _rwkv_core(rt, bt, at, kt, v, bonus, g, ptot, lnx_w, lnx_b, B, S):
    nc = S // CHUNK
    seq = lambda a: a.reshape(B, S, RWKV_W)
    tok = lambda: pl.BlockSpec((B, CHUNK, RWKV_W), lambda c: (0, c, 0))
    vec = lambda: pl.BlockSpec((1, RWKV_W), lambda c: (0, 0))
    out = pl.pallas_call(
        _rwkv_core_kernel,
        out_shape=jax.ShapeDtypeStruct((B, S, RWKV_W), BF16),
        grid=(nc,),
        in_specs=[tok() for _ in range(7)]
        + [pl.BlockSpec((B, 1, 1, RWKV_W), lambda c: (0, c, 0, 0)), vec(), vec()],
        out_specs=tok(),
        scratch_shapes=[pltpu.VMEM((B * N_PAIR, RWKV_HD, PAIR), F32)],
        compiler_params=_cparams(("arbitrary",)),
        name="rwkv_core",
    )(seq(rt), seq(bt), seq(at), seq(kt), seq(v), seq(bonus), seq(g),
      ptot.reshape(B, nc, 1, RWKV_W), lnx_w, lnx_b)
    return out.reshape(B * S, RWKV_W)


def _merge_kernel(x_ref, og_ref, or_ref, gb_ref, wb1_ref, wb2_ref, wo_ref, *rest):
    o_ref = rest[-1]
    gate_refs = rest[:-1]
    n = len(gate_refs) // 2
    tn = gate_refs[0].shape[1]
    D = x_ref.shape[1]
    y_gla = _dot(og_ref[...], wb1_ref[...])
    y_rw = _dot(or_ref[...], wb2_ref[...])
    merged = []
    for c in range(n):
        cols = slice(c * tn, (c + 1) * tn)
        g_gla = _sigmoid(gate_refs[c][...] + gb_ref[:, c * tn:(c + 1) * tn])
        g_rw = _sigmoid(gate_refs[n + c][...] + gb_ref[:, D + c * tn:D + (c + 1) * tn])
        merged.append((g_gla * y_gla[:, cols] + g_rw * y_rw[:, cols]).astype(BF16))
    o_ref[...] = x_ref[...] + _dot(jnp.concatenate(merged, axis=1), wo_ref[...])


def _merge(x, o_gla, o_rw, p, gate_b, w_branch, wo, *, tm=256, tn=PROJ_TN):
    T, D = x.shape
    const = lambda shape, r: pl.BlockSpec(shape, lambda i: (r, 0), pipeline_mode=pl.Buffered(1))
    assert GLA_V == RWKV_W and GATE_COL % tn == 0 and D % tn == 0
    n_gate_blocks = 2 * D // tn
    gate_specs = [pl.BlockSpec((tm, tn), lambda i, c=GATE_COL // tn + c: (i, c)) for c in range(n_gate_blocks)]
    return pl.pallas_call(
        _merge_kernel,
        out_shape=jax.ShapeDtypeStruct((T, D), F32),
        grid=(T // tm,),
        in_specs=[
            pl.BlockSpec((tm, D), lambda i: (i, 0)),
            pl.BlockSpec((tm, GLA_V), lambda i: (i, 0)),
            pl.BlockSpec((tm, RWKV_W), lambda i: (i, 0)),
            pl.BlockSpec((1, 2 * D), lambda i: (0, 0)),
            const((GLA_V, D), 0), const((RWKV_W, D), 1), const((D, D), 0),
        ] + gate_specs,
        out_specs=pl.BlockSpec((tm, D), lambda i: (i, 0)),
        compiler_params=_cparams(("parallel",)),
        name="merge_out",
    )(x, o_gla, o_rw, gate_b, w_branch, w_branch, wo, *([p] * n_gate_blocks))


def _pad_rows(w, n):
    return jnp.pad(w, ((0, n - w.shape[0]), (0, 0)))


def kernel(x, ffn1_norm, ffn1_wg, ffn1_wu, ffn1_wd, mix_norm, w_in, gla_w_a2, gla_b_a, gla_gn_w, rwkv_mu,
           rwkv_w0, rwkv_w_w2, rwkv_a0, rwkv_w_a2, rwkv_w_g2, rwkv_k_k, rwkv_k_a, rwkv_r_k, rwkv_lnx_w,
           rwkv_lnx_b, gate_b, w_branch, w_out, ffn2_norm, ffn2_wg, ffn2_wu, ffn2_wd, final_norm):
    B, S, D = x.shape
    T = B * S
    depth = ffn1_norm.shape[0]
    assert depth >= 1
    bf = lambda w: w.astype(BF16)
    row = lambda v: v.reshape(1, -1)
    assert D == D_MODEL and w_in.shape[2] == GLA_IN + RWKV_IN + 2 * D_MODEL
    xt = x.reshape(T, D)
    for l in range(depth):
        last = l == depth - 1
        xt, h_mix, w_p, w_br, w_o = _ffn(
            xt, row(ffn1_norm[l]), bf(ffn1_wg[l]), bf(ffn1_wu[l]), bf(ffn1_wd[l]), row(mix_norm[l]),
            mode=EMIT_NORM,
            aux=[(PACK_T, w_in[l].T), (CAST, w_branch[l]), (CAST, w_out[l])])

        mu_row = jnp.pad(rwkv_mu[l], (GLA_IN, 2 * D_MODEL))[None, :]
        mu = _pack_cols(mu_row)[:, :RW_GROUP]
        p, wg2, wu2, wd2 = _proj(h_mix, w_p,
                                 aux=[(CAST, ffn2_wg[l]), (CAST, ffn2_wu[l]), (CAST, ffn2_wd[l])])

        o_gla = _gla(p, bf(_pad_rows(gla_w_a2[l], LANE)), row(gla_b_a[l]), row(gla_gn_w[l]), B, S)

        prep = _rwkv_prep(p, mu, row(rwkv_w0[l]), bf(_pad_rows(rwkv_w_w2[l], LANE)), row(rwkv_a0[l]),
                          bf(_pad_rows(rwkv_w_a2[l], LANE)), bf(rwkv_w_g2[l]), row(rwkv_k_k[l]),
                          row(rwkv_k_a[l]), row(rwkv_r_k[l]), B, S)
        o_rw = _rwkv_core(*prep, row(rwkv_lnx_w[l]), row(rwkv_lnx_b[l]), B, S)

        xt = _merge(xt, o_gla, o_rw, p, row(gate_b[l]), w_br, w_o)

        xt = _ffn(xt, row(ffn2_norm[l]), wg2, wu2, wd2, row(final_norm),
                  mode=FINAL if last else RESIDUAL)[0]
    return xt.reshape(B, S, D)
```

```python
import functools
import math

import jax
import jax.numpy as jnp
from jax import lax
from jax.experimental import pallas as pl
from jax.experimental.pallas import tpu as pltpu

F32 = jnp.float32
BF16 = jnp.bfloat16

NORM_EPS = 1e-6
GN_EPS = 64e-5
GLA_TAU = 16.0
CHUNK = 64

GLA_HEADS = 4
GLA_DK = 128
GLA_DV = 256
GLA_QK = GLA_HEADS * GLA_DK
GLA_V = GLA_HEADS * GLA_DV
GLA_LORA = 16

RWKV_HD = 64
RWKV_W = 1024
DECAY_LORA = 96
AAA_LORA = 96
GATE_LORA = 256

LANE = 128
PAIR = 2 * RWKV_HD
N_PAIR = RWKV_W // PAIR

RW_GROUP = 3 * RWKV_W + LANE + LANE + GATE_LORA
GLA_GROUP = RW_GROUP
RW_WD = 3 * RWKV_W
GLA_AD = 2 * GLA_QK + 2 * GLA_V
RW_AD = RW_WD + LANE
RW_GD = RW_AD + LANE

VMEM_LIMIT = 60 * 1024 * 1024
BF16_SUBLANES = 16
F32_SUBLANES = 8

FFN_TM, FFN_TF = 512, 512
PROJ_TM, PROJ_TN = 2048, 1024


def _rows_per_block(rows, steps):
    for rb in range(BF16_SUBLANES, rows + 1, BF16_SUBLANES):
        if rows % rb == 0 and rows // rb <= steps:
            return rb
    raise ValueError(f"cannot split {rows} rows over {steps} steps")


def _cparams(sem):
    return pltpu.CompilerParams(dimension_semantics=sem, vmem_limit_bytes=VMEM_LIMIT)


def _dot(a, b):
    return jnp.dot(a, b, preferred_element_type=F32)


def _dot_nt(a, b):
    return lax.dot_general(a, b, (((1,), (1,)), ((), ())), preferred_element_type=F32)


def _dot_tn(a, b):
    return lax.dot_general(a, b, (((0,), (0,)), ((), ())), preferred_element_type=F32)


def _rmsnorm(x, g):
    return x * lax.rsqrt(jnp.mean(x * x, axis=-1, keepdims=True) + NORM_EPS) * g


def _sigmoid(z):
    return 0.5 + 0.5 * jnp.tanh(0.5 * z)


def _softplus(z):
    return jnp.maximum(z, 0.0) + jnp.log(1.0 + jnp.exp(-jnp.abs(z)))


CAST = "cast"
PACK_T = "pack_t"

D_MODEL = 2048
GLA_IN = 2 * GLA_QK + 2 * GLA_V + GLA_LORA
RWKV_IN = 3 * RWKV_W + DECAY_LORA + AAA_LORA + GATE_LORA
GATE_COL = RW_GROUP + GLA_GROUP
PACKED_COLS = GATE_COL + 2 * D_MODEL

IN_PIECES = (
    (0, GLA_IN, 3 * RWKV_W),
    (RW_WD, GLA_IN + 3 * RWKV_W, DECAY_LORA),
    (RW_AD, GLA_IN + 3 * RWKV_W + DECAY_LORA, AAA_LORA),
    (RW_GD, GLA_IN + 3 * RWKV_W + DECAY_LORA + AAA_LORA, GATE_LORA),
    (RW_GROUP, 0, GLA_IN),
    (GATE_COL, GLA_IN + RWKV_IN, 2 * D_MODEL),
)


def _pack_cols(w):
    zeros = lambda n: jnp.zeros((w.shape[0], n), w.dtype)
    parts, pos = [], 0
    for start, w_start, width in IN_PIECES:
        parts += [zeros(start - pos), w[:, w_start:w_start + width]]
        pos = start + width
    assert pos == PACKED_COLS
    return jnp.concatenate([p for p in parts if p.shape[1]], axis=1)


def _pack_src(k):
    col = k * LANE
    src = jnp.zeros_like(col)
    valid = jnp.zeros_like(col)
    for start, w_start, width in IN_PIECES:
        inside = (col >= start) & (col < start + width)
        src = jnp.where(inside, w_start + col - start, src)
        valid = jnp.where(inside, jnp.minimum(start + width - col, LANE), valid)
    return src, valid


def _aux_plan(aux, nj, steps):
    in_specs, operands, out_specs, out_shapes, meta = [], [], [], [], []
    for kind, w in aux:
        rows, cols = w.shape
        if kind == CAST:
            rb = _rows_per_block(rows, steps)
            n_active = rows // rb
            index = lambda i, j, n=n_active: (jnp.minimum(i * nj + j, n - 1), 0)
            in_specs.append(pl.BlockSpec((rb, cols), index))
            operands.append(w)
            out_specs.append(pl.BlockSpec((rb, cols), index))
            out_shapes.append(jax.ShapeDtypeStruct((rows, cols), BF16))
            meta.append((kind, n_active, 1, 1))
        else:
            n_blocks = PACKED_COLS // LANE
            per_step = next(d for d in range(1, n_blocks + 1) if n_blocks % d == 0 and n_blocks // d <= steps)
            n_active = n_blocks // per_step
            last = n_active - 1
            for q in range(per_step):
                src = lambda i, j, q=q, last=last, per_step=per_step: (
                    pl.multiple_of(_pack_src(jnp.minimum(i * nj + j, last) * per_step + q)[0],
                                   BF16_SUBLANES), 0)
                in_specs.append(pl.BlockSpec((pl.Element(LANE), pl.Element(cols)), src))
                operands.append(w)
            out_specs.append(pl.BlockSpec((per_step * LANE, cols),
                                          lambda i, j, last=last: (jnp.minimum(i * nj + j, last), 0)))
            out_shapes.append(jax.ShapeDtypeStruct((PACKED_COLS, cols), BF16))
            meta.append((kind, n_active, per_step, 1))
    return in_specs, operands, out_specs, out_shapes, tuple(meta)


def _aux_run(meta, in_refs, out_refs):
    step = pl.program_id(0) * pl.num_programs(1) + pl.program_id(1)
    ki = ko = 0
    for kind, n_active, n_in, n_out in meta:
        ins, outs = in_refs[ki:ki + n_in], out_refs[ko:ko + n_out]
        ki += n_in
        ko += n_out
        if kind == CAST:
            outs[0][...] = ins[0][...].astype(BF16)
        else:
            block = jnp.minimum(step, n_active - 1) * n_in
            for q, w_ref in enumerate(ins):
                w = w_ref[...]
                _, valid = _pack_src(block + q)
                row = lax.broadcasted_iota(jnp.int32, w.shape, 0)
                outs[0][q * LANE:(q + 1) * LANE, :] = jnp.where(row < valid, w, 0.0).astype(BF16)


RESIDUAL = "residual"
FINAL = "final"
EMIT_NORM = "emit"


def _ffn_kernel(*refs, mode, aux_meta):
    x_ref, g_ref, wg_ref, wu_ref, wd_ref, fn_ref = refs[:6]
    n_aux = sum(m[2] for m in aux_meta)
    n_main_out = 2 if mode == EMIT_NORM else 1
    o_ref = refs[6 + n_aux]
    h_scr = refs[-1]
    j = pl.program_id(1)

    @pl.when(j == 0)
    def _():
        h_scr[...] = _rmsnorm(x_ref[...], g_ref[...]).astype(BF16)
        o_ref[...] = jnp.zeros_like(o_ref)

    _aux_run(aux_meta, refs[6:6 + n_aux], refs[6 + n_aux + n_main_out:-1])
    h = h_scr[...]
    a = _dot(h, wg_ref[...])
    u = _dot(h, wu_ref[...])
    act = (a * _sigmoid(a) * u).astype(BF16)
    o_ref[...] += _dot(act, wd_ref[...])

    @pl.when(j == pl.num_programs(1) - 1)
    def _():
        y = x_ref[...] + 0.5 * o_ref[...]
        if mode == FINAL:
            y = _rmsnorm(y, fn_ref[...])
        o_ref[...] = y
        if mode == EMIT_NORM:
            refs[7 + n_aux][...] = _rmsnorm(y, fn_ref[...]).astype(BF16)


def _ffn(x, g, wg, wu, wd, fn, *, mode, aux=(), tm=FFN_TM, tf=FFN_TF):
    T, D = x.shape
    FF = wg.shape[1]
    nj = FF // tf
    aux_in, aux_ops, aux_out, aux_shapes, aux_meta = _aux_plan(aux, nj, (T // tm) * nj)
    tok = lambda: pl.BlockSpec((tm, D), lambda i, j: (i, 0))
    main_shapes = [jax.ShapeDtypeStruct((T, D), F32)]
    if mode == EMIT_NORM:
        main_shapes.append(jax.ShapeDtypeStruct((T, D), BF16))
    main_specs = [tok() for _ in main_shapes]
    return pl.pallas_call(
        functools.partial(_ffn_kernel, mode=mode, aux_meta=aux_meta),
        out_shape=main_shapes + aux_shapes,
        grid=(T // tm, nj),
        in_specs=[
            tok(),
            pl.BlockSpec((1, D), lambda i, j: (0, 0)),
            pl.BlockSpec((D, tf), lambda i, j: (0, j)),
            pl.BlockSpec((D, tf), lambda i, j: (0, j)),
            pl.BlockSpec((tf, D), lambda i, j: (j, 0)),
            pl.BlockSpec((1, D), lambda i, j: (0, 0)),
        ] + aux_in,
        out_specs=main_specs + aux_out,
        scratch_shapes=[pltpu.VMEM((tm, D), BF16)],
        compiler_params=_cparams(("arbitrary", "arbitrary")),
        name="ffn_" + mode,
    )(x, g, wg, wu, wd, fn, *aux_ops)


def _proj_kernel(*refs, aux_meta):
    h_ref, w_ref = refs[:2]
    n_aux = sum(m[2] for m in aux_meta)
    _aux_run(aux_meta, refs[2:2 + n_aux], refs[3 + n_aux:])
    refs[2 + n_aux][...] = _dot_nt(h_ref[...], w_ref[...])


def _proj(h, w, *, aux=(), tm=PROJ_TM, tn=PROJ_TN):
    T, D = h.shape
    N = w.shape[0]
    nj = N // tn
    aux_in, aux_ops, aux_out, aux_shapes, aux_meta = _aux_plan(aux, nj, (T // tm) * nj)
    return pl.pallas_call(
        functools.partial(_proj_kernel, aux_meta=aux_meta),
        out_shape=[jax.ShapeDtypeStruct((T, N), F32)] + aux_shapes,
        grid=(T // tm, nj),
        in_specs=[
            pl.BlockSpec((tm, D), lambda i, j: (i, 0), pipeline_mode=pl.Buffered(1)),
            pl.BlockSpec((tn, D), lambda i, j: (j, 0)),
        ] + aux_in,
        out_specs=[pl.BlockSpec((tm, tn), lambda i, j: (i, j))] + aux_out,
        compiler_params=_cparams(("arbitrary", "arbitrary")),
        name="in_proj",
    )(h, w, *aux_ops)


def _chunk_tri(n):
    r = lax.broadcasted_iota(jnp.int32, (n, n), 0)
    c = lax.broadcasted_iota(jnp.int32, (n, n), 1)
    return jnp.where((r >= c) & (r // CHUNK == c // CHUNK), 1.0, 0.0).astype(BF16)


def _gla_kernel(p_ref, wa2_ref, ba_ref, gn_ref, o_ref, st_scr, *, tg):
    @pl.when(pl.program_id(1) == 0)
    def _():
        st_scr[...] = jnp.zeros_like(st_scr)

    gad = p_ref[:, GLA_AD:GLA_AD + LANE].astype(BF16)
    z = _dot(gad, wa2_ref[...]) + ba_ref[...]
    log_alpha = -_softplus(-z) * (1.0 / GLA_TAU)
    tri = _chunk_tri(tg)
    hi = log_alpha.astype(BF16)
    lo = (log_alpha - hi.astype(F32)).astype(BF16)
    cum = _dot(tri, hi) + _dot(tri, lo)
    gn = gn_ref[...]
    scale = GLA_DK ** -0.5

    n_chunks = tg // CHUNK
    rows = [slice(c * CHUNK, (c + 1) * CHUNK) for c in range(n_chunks)]
    keys = [slice(h * GLA_DK, (h + 1) * GLA_DK) for h in range(GLA_HEADS)]
    vals = [slice(h * GLA_DV, (h + 1) * GLA_DV) for h in range(GLA_HEADS)]
    units = [(c, h) for c in range(n_chunks) for h in range(GLA_HEADS)]
    tot = [cum[r, :][CHUNK - 1:CHUNK, :] for r in rows]
    etot = [jnp.exp(t) for t in tot]
    kdec = [(p_ref[rows[c], GLA_QK:2 * GLA_QK] * jnp.exp(tot[c] - cum[rows[c], :])).astype(BF16)
            for c in range(n_chunks)]
    q = [(p_ref[r, 0:GLA_QK] * scale).astype(BF16) for r in rows]
    inc = {}
    for c, h in units:
        v_h = p_ref[rows[c], 2 * GLA_QK + h * GLA_DV:2 * GLA_QK + (h + 1) * GLA_DV].astype(BF16)
        inc[c, h] = _dot_tn(v_h, kdec[c][:, keys[h]])
    st = {}
    for h in range(GLA_HEADS):
        prev = st_scr[h]
        for c in range(n_chunks):
            prev = prev * etot[c][:, keys[h]] + inc[c, h]
            st[c, h] = prev
        st_scr[h] = prev
    out = {u: _dot_nt(q[u[0]][:, keys[u[1]]], st[u].astype(BF16)) for u in units}
    for c, h in units:
        o = out[c, h]
        r_h = p_ref[rows[c], 2 * GLA_QK + GLA_V + h * GLA_DV:2 * GLA_QK + GLA_V + (h + 1) * GLA_DV]
        o = o * lax.rsqrt(jnp.mean(o * o, axis=-1, keepdims=True) + NORM_EPS) * gn
        o = o * (r_h * _sigmoid(r_h))
        o_ref[rows[c], vals[h]] = o.astype(BF16)


def _gla(p, wa2, ba, gn, B, S, *, tg=256):
    T = B * S
    nb = S // tg
    return pl.pallas_call(
        functools.partial(_gla_kernel, tg=tg),
        out_shape=jax.ShapeDtypeStruct((T, GLA_V), BF16),
        grid=(B, nb),
        in_specs=[
            pl.BlockSpec((tg, GLA_GROUP), lambda b, i: (b * nb + i, 1)),
            pl.BlockSpec((LANE, GLA_QK), lambda b, i: (0, 0)),
            pl.BlockSpec((1, GLA_QK), lambda b, i: (0, 0)),
            pl.BlockSpec((1, GLA_DV), lambda b, i: (0, 0)),
        ],
        out_specs=pl.BlockSpec((tg, GLA_V), lambda b, i: (b * nb + i, 0)),
        scratch_shapes=[pltpu.VMEM((GLA_HEADS, GLA_DV, GLA_DK), F32)],
        compiler_params=_cparams(("parallel", "arbitrary")),
        name="gla",
    )(p, wa2, ba, gn)


MXU_DIM = 256


def _head_ones(n):
    r = lax.broadcasted_iota(jnp.int32, (n, n), 0)
    c = lax.broadcasted_iota(jnp.int32, (n, n), 1)
    return jnp.where(r // RWKV_HD == c // RWKV_HD, 1.0, 0.0).astype(BF16)


def _head_sum(x, ones_blk):
    xb = x.astype(BF16)
    parts = [_dot(xb[:, j:j + MXU_DIM], ones_blk) for j in range(0, x.shape[1], MXU_DIM)]
    return jnp.concatenate(parts, axis=1)


def _rwkv_prep_kernel(p_ref, mu_ref, w0_ref, ww2_ref, a0_ref, wa2_ref, wg2_ref, kk_ref, ka_ref, rk_ref,
                      rt_ref, bt_ref, at_ref, kt_ref, v_ref, bonus_ref, g_ref, ptot_ref,
                      carry_scr, *, tm):
    @pl.when(pl.program_id(1) == 0)
    def _():
        carry_scr[...] = jnp.zeros_like(carry_scr)

    p = p_ref[...]
    last = carry_scr[...]
    carry_scr[...] = p[tm - 1:tm, :]
    prev = pltpu.roll(p, 1, axis=0)
    row = lax.broadcasted_iota(jnp.int32, (F32_SUBLANES, p.shape[1]), 0)
    prev = jnp.concatenate([jnp.where(row == 0, last, prev[:F32_SUBLANES]), prev[F32_SUBLANES:]], axis=0)
    p = p + mu_ref[...] * (prev - p)

    r = p[:, 0:RWKV_W]
    k = p[:, RWKV_W:2 * RWKV_W]
    v = p[:, 2 * RWKV_W:3 * RWKV_W]
    wd = p[:, RW_WD:RW_WD + LANE]
    ad = p[:, RW_AD:RW_AD + LANE]
    gd = p[:, RW_GD:RW_GD + GATE_LORA]

    w_raw = w0_ref[...] + _dot(jnp.tanh(wd).astype(BF16), ww2_ref[...])
    log_w = (-math.exp(-0.5)) * _sigmoid(w_raw)
    a = _sigmoid(a0_ref[...] + _dot(ad.astype(BF16), wa2_ref[...]))
    g_ref[...] = _dot(_sigmoid(gd).astype(BF16), wg2_ref[...])

    ones_blk = _head_ones(MXU_DIM)
    kk = k * kk_ref[...]
    kk = kk * lax.rsqrt(jnp.maximum(_head_sum(kk * kk, ones_blk), 1e-24))
    kp = k * (1.0 + (a - 1.0) * ka_ref[...])
    bonus_ref[...] = _head_sum(r * kp * rk_ref[...], ones_blk) * v
    v_ref[...] = v.astype(BF16)

    tri = _chunk_tri(tm)
    hi = log_w.astype(BF16)
    lo = (log_w - hi.astype(F32)).astype(BF16)
    cum = _dot(tri, hi) + _dot(tri, lo)
    nalpha = -(kk * a)
    for ci in range(tm // CHUNK):
        rows = slice(ci * CHUNK, (ci + 1) * CHUNK)
        cum_c = cum[rows, :]
        tot = cum_c[CHUNK - 1:CHUNK, :]
        e_inv = jnp.exp(-cum_c)
        rt_ref[rows, :] = (r[rows, :] * jnp.exp(cum_c)).astype(BF16)
        bt_ref[rows, :] = (kk[rows, :] * jnp.exp(cum_c - log_w[rows, :])).astype(BF16)
        at_ref[rows, :] = (nalpha[rows, :] * e_inv).astype(BF16)
        kt_ref[rows, :] = (kp[rows, :] * e_inv).astype(BF16)
        ptot_ref[ci] = jnp.exp(tot)


def _rwkv_prep(p, mu, w0, ww2, a0, wa2, wg2, k_k, k_a, r_k, B, S, *, tm=256):
    T = B * S
    nb = S // tm
    cpt = tm // CHUNK
    vec = lambda n: pl.BlockSpec((1, n), lambda b, i: (0, 0))
    tok = lambda: pl.BlockSpec((tm, RWKV_W), lambda b, i: (b * nb + i, 0))
    bf = jax.ShapeDtypeStruct((T, RWKV_W), BF16)
    f32 = jax.ShapeDtypeStruct((T, RWKV_W), F32)
    return pl.pallas_call(
        functools.partial(_rwkv_prep_kernel, tm=tm),
        out_shape=[bf, bf, bf, bf, bf, f32, f32,
                   jax.ShapeDtypeStruct((T // CHUNK, 1, RWKV_W), F32)],
        grid=(B, nb),
        in_specs=[
            pl.BlockSpec((tm, RW_GROUP), lambda b, i: (b * nb + i, 0)),
            vec(RW_GROUP), vec(RWKV_W),
            pl.BlockSpec((LANE, RWKV_W), lambda b, i: (0, 0)),
            vec(RWKV_W),
            pl.BlockSpec((LANE, RWKV_W), lambda b, i: (0, 0)),
            pl.BlockSpec((GATE_LORA, RWKV_W), lambda b, i: (0, 0)),
            vec(RWKV_W), vec(RWKV_W), vec(RWKV_W),
        ],
        out_specs=[tok() for _ in range(7)]
        + [pl.BlockSpec((cpt, 1, RWKV_W), lambda b, i: (b * nb + i, 0, 0))],
        scratch_shapes=[pltpu.VMEM((1, RW_GROUP), F32)],
        compiler_params=_cparams(("parallel", "arbitrary")),
        name="rwkv_prep",
    )(p, mu, w0, ww2, a0, wa2, wg2, k_k, k_a, r_k)


def _rwkv_core_kernel(rt_ref, bt_ref, at_ref, kt_ref, v_ref, bonus_ref, g_ref, ptot_ref,
                      lw_ref, lb_ref, o_ref, h_scr):
    @pl.when(pl.program_id(0) == 0)
    def _():
        h_scr[...] = jnp.zeros_like(h_scr)

    assert CHUNK == RWKV_HD
    n_batch = rt_ref.shape[0]
    ti = lax.broadcasted_iota(jnp.int32, (CHUNK, PAIR), 0)
    si = lax.broadcasted_iota(jnp.int32, (CHUNK, PAIR), 1) % RWKV_HD
    strict = ti > si
    lower = ti >= si
    eye = ti == si
    blk8 = (ti // 8) == (si // 8)
    eye_f = jnp.where(eye, 1.0, 0.0)
    ri = lax.broadcasted_iota(jnp.int32, (PAIR, PAIR), 0)
    ci = lax.broadcasted_iota(jnp.int32, (PAIR, PAIR), 1)
    head_blk = (ri // RWKV_HD) == (ci // RWKV_HD)
    ones_pair = jnp.where(head_blk, 1.0, 0.0).astype(BF16)
    ones_2 = jnp.concatenate([ones_pair, ones_pair], axis=0)
    zeros_b = jnp.zeros((PAIR, PAIR), BF16)

    def bd(x):
        return jnp.where(head_blk, jnp.concatenate([x, x], axis=0), jnp.zeros((), x.dtype))

    def head_t(x):
        xt = x.T
        return jnp.concatenate([xt[:RWKV_HD], xt[RWKV_HD:]], axis=1)

    def dot_packed(lhs, rhs_bd):
        out = []
        for j in range(0, len(lhs), 2):
            l2 = jnp.concatenate([lhs[j], lhs[j + 1]], axis=1)
            r2 = jnp.concatenate([jnp.concatenate([rhs_bd[j], zeros_b], axis=1),
                                  jnp.concatenate([zeros_b, rhs_bd[j + 1]], axis=1)], axis=0)
            o = _dot(l2, r2)
            out += [o[:, :PAIR], o[:, PAIR:]]
        return out

    bf = lambda xs: [x.astype(BF16) for x in xs]

    def head_sums(xs):
        hi = [x.astype(BF16) for x in xs]
        lo = [(x - h.astype(F32)).astype(BF16) for x, h in zip(xs, hi)]
        rows = jnp.concatenate([jnp.concatenate([h, l], axis=1) for h, l in zip(hi, lo)], axis=0)
        sums = _dot(rows, ones_2)
        return [sums[j * CHUNK:(j + 1) * CHUNK] for j in range(len(xs))]

    pairs = range(n_batch * N_PAIR)
    where = [(b, slice(j * PAIR, (j + 1) * PAIR)) for b in range(n_batch) for j in range(N_PAIR)]
    ptot = [ptot_ref[b, 0, :, l] for b, l in where]
    bt = [bt_ref[b, :, l] for b, l in where]
    rt = [rt_ref[b, :, l] for b, l in where]
    at = [at_ref[b, :, l] for b, l in where]
    kt = [kt_ref[b, :, l] for b, l in where]
    bx = [bd(x) for x in bt]
    vx = [bd(v_ref[b, :, l]) for b, l in where]
    ak = [jnp.concatenate([bd(at[j]), bd(kt[j])], axis=0) for j in pairs]
    ak2t = [jnp.concatenate([head_t(at[j].astype(F32) * ptot[j]).astype(BF16),
                             head_t(kt[j].astype(F32) * ptot[j]).astype(BF16)], axis=1) for j in pairs]

    g = [_dot_nt(jnp.concatenate([bt[j], rt[j]], axis=0), ak[j]) for j in pairs]
    gb = [x[:CHUNK] for x in g]
    gr = [x[CHUNK:] for x in g]
    a_ab = [jnp.where(strict, x[:, :PAIR], 0.0) for x in gb]
    a_kb = [jnp.where(strict, x[:, PAIR:], 0.0).astype(BF16) for x in gb]
    a_r = [jnp.concatenate([jnp.where(lower, x[:, :PAIR], 0.0),
                            jnp.where(lower, x[:, PAIR:], 0.0)], axis=1).astype(BF16) for x in gr]

    a_d = [jnp.where(blk8, x, 0.0) for x in a_ab]
    a_db = bf(a_d)
    pw = bf(dot_packed(a_db, [bd(x) for x in a_db]))
    s = [eye_f + x for x in a_d]
    sp = [_dot(pw[j], jnp.concatenate([bd(s[j].astype(BF16)), bd(pw[j])], axis=1)) for j in pairs]
    s = [s[j] + sp[j][:, :PAIR] for j in pairs]
    pw = [x[:, PAIR:].astype(BF16) for x in sp]
    ps = dot_packed(pw, [bd(x) for x in bf(s)])
    s = [s[j] + ps[j] for j in pairs]
    for width in (8, 16, 32):
        off = ((ti // (2 * width)) == (si // (2 * width))) & ((ti // width) != (si // width))
        e = [jnp.where(off, x, 0.0).astype(BF16) for x in a_ab]
        sb = bf(s)
        es = bf(dot_packed(e, [bd(x) for x in sb]))
        ses = dot_packed(sb, [bd(x) for x in es])
        s = [s[j] + ses[j] for j in pairs]
    t_inv = bf(s)

    kv = bf(dot_packed(a_kb, vx))
    wu = bf([_dot(t_inv[j], jnp.concatenate([bx[j], bd(kv[j])], axis=1)) for j in pairs])
    z = [jnp.concatenate([jnp.concatenate([bd(wu[j][:, :PAIR]), bd(wu[j][:, PAIR:])], axis=1),
                          jnp.concatenate([zeros_b, vx[j]], axis=1)], axis=0) for j in pairs]
    mcqy = [_dot(jnp.concatenate([ak2t[j], a_r[j]], axis=0), z[j]) for j in pairs]
    mc = [x[:CHUNK] for x in mcqy]
    qy = [x[CHUNK:] for x in mcqy]

    m = [(mc[j][:, :PAIR] + jnp.where(eye, ptot[j], 0.0)).astype(BF16) for j in pairs]
    q = [(qy[j][:, :PAIR] + rt[j].astype(F32)).astype(BF16) for j in pairs]
    hb = [bd(h_scr[j].astype(BF16)) for j in pairs]
    qmh = dot_packed([jnp.concatenate([q[j], m[j]], axis=0) for j in pairs], hb)
    for j in pairs:
        h_scr[j] = qmh[j][CHUNK:] + mc[j][:, PAIR:]
    y = [qmh[j][:CHUNK] + qy[j][:, PAIR:] for j in pairs]

    mean = [x * (1.0 / RWKV_HD) for x in head_sums(y)]
    yc = [y[j] - mean[j] for j in pairs]
    var = [x * (1.0 / RWKV_HD) for x in head_sums([x * x for x in yc])]
    for j in pairs:
        b, l = where[j]
        yn = yc[j] * lax.rsqrt(var[j] + GN_EPS) * lw_ref[:, l] + lb_ref[:, l]
        o_ref[b, :, l] = ((yn + bonus_ref[b, :, l]) * g_ref[b, :, l]).astype(BF16)


def _rwkv_core(rt, bt, at, kt, v, bonus, g, ptot, lnx_w, lnx_b, B, S):
    nc = S // CHUNK
    seq = lambda a: a.reshape(B, S, RWKV_W)
    tok = lambda: pl.BlockSpec((B, CHUNK, RWKV_W), lambda c: (0, c, 0))
    vec = lambda: pl.BlockSpec((1, RWKV_W), lambda c: (0, 0))
    out = pl.pallas_call(
        _rwkv_core_kernel,
        out_shape=jax.ShapeDtypeStruct((B, S, RWKV_W), BF16),
        grid=(nc,),
        in_specs=[tok() for _ in range(7)]
        + [pl.BlockSpec((B, 1, 1, RWKV_W), lambda c: (0, c, 0, 0)), vec(), vec()],
        out_specs=tok(),
        scratch_shapes=[pltpu.VMEM((B * N_PAIR, RWKV_HD, PAIR), F32)],
        compiler_params=_cparams(("arbitrary",)),
        name="rwkv_core",
    )(seq(rt), seq(bt), seq(at), seq(kt), seq(v), seq(bonus), seq(g),
      ptot.reshape(B, nc, 1, RWKV_W), lnx_w, lnx_b)
    return out.reshape(B * S, RWKV_W)


def _merge_kernel(x_ref, og_ref, or_ref, gb_ref, wb1_ref, wb2_ref, wo_ref, *rest):
    o_ref = rest[-1]
    gate_refs = rest[:-1]
    n = len(gate_refs) // 2
    tn = gate_refs[0].shape[1]
    D = x_ref.shape[1]
    y_gla = _dot(og_ref[...], wb1_ref[...])
    y_rw = _dot(or_ref[...], wb2_ref[...])
    merged = []
    for c in range(n):
        cols = slice(c * tn, (c + 1) * tn)
        g_gla = _sigmoid(gate_refs[c][...] + gb_ref[:, c * tn:(c + 1) * tn])
        g_rw = _sigmoid(gate_refs[n + c][...] + gb_ref[:, D + c * tn:D + (c + 1) * tn])
        merged.append((g_gla * y_gla[:, cols] + g_rw * y_rw[:, cols]).astype(BF16))
    o_ref[...] = x_ref[...] + _dot(jnp.concatenate(merged, axis=1), wo_ref[...])


def _merge(x, o_gla, o_rw, p, gate_b, w_branch, wo, *, tm=256, tn=PROJ_TN):
    T, D = x.shape
    const = lambda shape, r: pl.BlockSpec(shape, lambda i: (r, 0), pipeline_mode=pl.Buffered(1))
    assert GLA_V == RWKV_W and GATE_COL % tn == 0 and D % tn == 0
    n_gate_blocks = 2 * D // tn
    gate_specs = [pl.BlockSpec((tm, tn), lambda i, c=GATE_COL // tn + c: (i, c)) for c in range(n_gate_blocks)]
    return pl.pallas_call(
        _merge_kernel,
        out_shape=jax.ShapeDtypeStruct((T, D), F32),
        grid=(T // tm,),
        in_specs=[
            pl.BlockSpec((tm, D), lambda i: (i, 0)),
            pl.BlockSpec((tm, GLA_V), lambda i: (i, 0)),
            pl.BlockSpec((tm, RWKV_W), lambda i: (i, 0)),
            pl.BlockSpec((1, 2 * D), lambda i: (0, 0)),
            const((GLA_V, D), 0), const((RWKV_W, D), 1), const((D, D), 0),
        ] + gate_specs,
        out_specs=pl.BlockSpec((tm, D), lambda i: (i, 0)),
        compiler_params=_cparams(("parallel",)),
        name="merge_out",
    )(x, o_gla, o_rw, gate_b, w_branch, w_branch, wo, *([p] * n_gate_blocks))


def _pad_rows(w, n):
    return jnp.pad(w, ((0, n - w.shape[0]), (0, 0)))


def kernel(x, ffn1_norm, ffn1_wg, ffn1_wu, ffn1_wd, mix_norm, w_in, gla_w_a2, gla_b_a, gla_gn_w, rwkv_mu,
           rwkv_w0, rwkv_w_w2, rwkv_a0, rwkv_w_a2, rwkv_w_g2, rwkv_k_k, rwkv_k_a, rwkv_r_k, rwkv_lnx_w,
           rwkv_lnx_b, gate_b, w_branch, w_out, ffn2_norm, ffn2_wg, ffn2_wu, ffn2_wd, final_norm):
    B, S, D = x.shape
    T = B * S
    depth = ffn1_norm.shape[0]
    assert depth >= 1
    bf = lambda w: w.astype(BF16)
    row = lambda v: v.reshape(1, -1)
    assert D == D_MODEL and w_in.shape[2] == GLA_IN + RWKV_IN + 2 * D_MODEL
    xt = x.reshape(T, D)
    for l in range(depth):
        last = l == depth - 1
        xt, h_mix, w_p = _ffn(
            xt, row(ffn1_norm[l]), bf(ffn1_wg[l]), bf(ffn1_wu[l]), bf(ffn1_wd[l]), row(mix_norm[l]),
            mode=EMIT_NORM, aux=[(PACK_T, w_in[l].T)])

        mu_row = jnp.pad(rwkv_mu[l], (GLA_IN, 2 * D_MODEL))[None, :]
        mu = _pack_cols(mu_row)[:, :RW_GROUP]
        p, w_br, w_o, wg2, wu2, wd2 = _proj(
            h_mix, w_p, aux=[(CAST, w_branch[l]), (CAST, w_out[l]),
                             (CAST, ffn2_wg[l]), (CAST, ffn2_wu[l]), (CAST, ffn2_wd[l])])

        o_gla = _gla(p, bf(_pad_rows(gla_w_a2[l], LANE)), row(gla_b_a[l]), row(gla_gn_w[l]), B, S)

        prep = _rwkv_prep(p, mu, row(rwkv_w0[l]), bf(_pad_rows(rwkv_w_w2[l], LANE)), row(rwkv_a0[l]),
                          bf(_pad_rows(rwkv_w_a2[l], LANE)), bf(rwkv_w_g2[l]), row(rwkv_k_k[l]),
                          row(rwkv_k_a[l]), row(rwkv_r_k[l]), B, S)
        o_rw = _rwkv_core(*prep, row(rwkv_lnx_w[l]), row(rwkv_lnx_b[l]), B, S)

        xt = _merge(xt, o_gla, o_rw, p, row(gate_b[l]), w_br, w_o)

        xt = _ffn(xt, row(ffn2_norm[l]), wg2, wu2, wd2, row(final_norm),
                  mode=FINAL if last else RESIDUAL)[0]
    return xt.reshape(B, S, D)
```

```python
import functools
import math

import jax
import jax.numpy as jnp
from jax import lax
from jax.experimental import pallas as pl
from jax.experimental.pallas import tpu as pltpu

F32 = jnp.float32
BF16 = jnp.bfloat16

NORM_EPS = 1e-6
GN_EPS = 64e-5
GLA_TAU = 16.0
CHUNK = 64

GLA_HEADS = 4
GLA_DK = 128
GLA_DV = 256
GLA_QK = GLA_HEADS * GLA_DK
GLA_V = GLA_HEADS * GLA_DV
GLA_LORA = 16

RWKV_HD = 64
RWKV_W = 1024
DECAY_LORA = 96
AAA_LORA = 96
GATE_LORA = 256

LANE = 128
PAIR = 2 * RWKV_HD
N_PAIR = RWKV_W // PAIR

RW_GROUP = 3 * RWKV_W + LANE + LANE + GATE_LORA
GLA_GROUP = RW_GROUP
RW_WD = 3 * RWKV_W
GLA_AD = 2 * GLA_QK + 2 * GLA_V
RW_AD = RW_WD + LANE
RW_GD = RW_AD + LANE

VMEM_LIMIT = 60 * 1024 * 1024
BF16_SUBLANES = 16
F32_SUBLANES = 8

FFN_TM, FFN_TF = 512, 512
PROJ_TM, PROJ_TN = 2048, 1024


def _rows_per_block(rows, steps):
    for rb in range(BF16_SUBLANES, rows + 1, BF16_SUBLANES):
        if rows % rb == 0 and rows // rb <= steps:
            return rb
    raise ValueError(f"cannot split {rows} rows over {steps} steps")


def _cparams(sem):
    return pltpu.CompilerParams(dimension_semantics=sem, vmem_limit_bytes=VMEM_LIMIT)


def _dot(a, b):
    return jnp.dot(a, b, preferred_element_type=F32)


def _dot_nt(a, b):
    return lax.dot_general(a, b, (((1,), (1,)), ((), ())), preferred_element_type=F32)


def _dot_tn(a, b):
    return lax.dot_general(a, b, (((0,), (0,)), ((), ())), preferred_element_type=F32)


def _rmsnorm(x, g):
    return x * lax.rsqrt(jnp.mean(x * x, axis=-1, keepdims=True) + NORM_EPS) * g


def _sigmoid(z):
    return 0.5 + 0.5 * jnp.tanh(0.5 * z)


def _softplus(z):
    return jnp.maximum(z, 0.0) + jnp.log(1.0 + jnp.exp(-jnp.abs(z)))


CAST = "cast"
PACK_T = "pack_t"

D_MODEL = 2048
GLA_IN = 2 * GLA_QK + 2 * GLA_V + GLA_LORA
RWKV_IN = 3 * RWKV_W + DECAY_LORA + AAA_LORA + GATE_LORA
GATE_COL = RW_GROUP + GLA_GROUP
PACKED_COLS = GATE_COL + 2 * D_MODEL

IN_PIECES = (
    (0, GLA_IN, 3 * RWKV_W),
    (RW_WD, GLA_IN + 3 * RWKV_W, DECAY_LORA),
    (RW_AD, GLA_IN + 3 * RWKV_W + DECAY_LORA, AAA_LORA),
    (RW_GD, GLA_IN + 3 * RWKV_W + DECAY_LORA + AAA_LORA, GATE_LORA),
    (RW_GROUP, 0, GLA_IN),
    (GATE_COL, GLA_IN + RWKV_IN, 2 * D_MODEL),
)


def _pack_cols(w):
    zeros = lambda n: jnp.zeros((w.shape[0], n), w.dtype)
    parts, pos = [], 0
    for start, w_start, width in IN_PIECES:
        parts += [zeros(start - pos), w[:, w_start:w_start + width]]
        pos = start + width
    assert pos == PACKED_COLS
    return jnp.concatenate([p for p in parts if p.shape[1]], axis=1)


def _pack_src(k):
    col = k * LANE
    src = jnp.zeros_like(col)
    valid = jnp.zeros_like(col)
    for start, w_start, width in IN_PIECES:
        inside = (col >= start) & (col < start + width)
        src = jnp.where(inside, w_start + col - start, src)
        valid = jnp.where(inside, jnp.minimum(start + width - col, LANE), valid)
    return src, valid


def _aux_plan(aux, nj, steps):
    in_specs, operands, out_specs, out_shapes, meta = [], [], [], [], []
    for kind, w in aux:
        rows, cols = w.shape
        if kind == CAST:
            rb = _rows_per_block(rows, steps)
            n_active = rows // rb
            index = lambda i, j, n=n_active: (jnp.minimum(i * nj + j, n - 1), 0)
            in_specs.append(pl.BlockSpec((rb, cols), index))
            operands.append(w)
            out_specs.append(pl.BlockSpec((rb, cols), index))
            out_shapes.append(jax.ShapeDtypeStruct((rows, cols), BF16))
            meta.append((kind, n_active, 1, 1))
        else:
            n_blocks = PACKED_COLS // LANE
            per_step = next(d for d in range(1, n_blocks + 1) if n_blocks % d == 0 and n_blocks // d <= steps)
            n_active = n_blocks // per_step
            last = n_active - 1
            for q in range(per_step):
                src = lambda i, j, q=q, last=last, per_step=per_step: (
                    pl.multiple_of(_pack_src(jnp.minimum(i * nj + j, last) * per_step + q)[0],
                                   BF16_SUBLANES), 0)
                in_specs.append(pl.BlockSpec((pl.Element(LANE), pl.Element(cols)), src))
                operands.append(w)
            out_specs.append(pl.BlockSpec((per_step * LANE, cols),
                                          lambda i, j, last=last: (jnp.minimum(i * nj + j, last), 0)))
            out_shapes.append(jax.ShapeDtypeStruct((PACKED_COLS, cols), BF16))
            meta.append((kind, n_active, per_step, 1))
    return in_specs, operands, out_specs, out_shapes, tuple(meta)


def _aux_run(meta, in_refs, out_refs):
    step = pl.program_id(0) * pl.num_programs(1) + pl.program_id(1)
    ki = ko = 0
    for kind, n_active, n_in, n_out in meta:
        ins, outs = in_refs[ki:ki + n_in], out_refs[ko:ko + n_out]
        ki += n_in
        ko += n_out
        if kind == CAST:
            outs[0][...] = ins[0][...].astype(BF16)
        else:
            block = jnp.minimum(step, n_active - 1) * n_in
            for q, w_ref in enumerate(ins):
                w = w_ref[...]
                _, valid = _pack_src(block + q)
                row = lax.broadcasted_iota(jnp.int32, w.shape, 0)
                outs[0][q * LANE:(q + 1) * LANE, :] = jnp.where(row < valid, w, 0.0).astype(BF16)


RESIDUAL = "residual"
FINAL = "final"
EMIT_NORM = "emit"


def _ffn_kernel(*refs, mode, aux_meta):
    x_ref, g_ref, wg_ref, wu_ref, wd_ref, fn_ref = refs[:6]
    n_aux = sum(m[2] for m in aux_meta)
    n_main_out = 2 if mode == EMIT_NORM else 1
    o_ref = refs[6 + n_aux]
    h_scr = refs[-1]
    j = pl.program_id(1)

    @pl.when(j == 0)
    def _():
        h_scr[...] = _rmsnorm(x_ref[...], g_ref[...]).astype(BF16)
        o_ref[...] = jnp.zeros_like(o_ref)

    _aux_run(aux_meta, refs[6:6 + n_aux], refs[6 + n_aux + n_main_out:-1])
    h = h_scr[...]
    a = _dot(h, wg_ref[...])
    u = _dot(h, wu_ref[...])
    act = (a * _sigmoid(a) * u).astype(BF16)
    o_ref[...] += _dot(act, wd_ref[...])

    @pl.when(j == pl.num_programs(1) - 1)
    def _():
        y = x_ref[...] + 0.5 * o_ref[...]
        if mode == FINAL:
            y = _rmsnorm(y, fn_ref[...])
        o_ref[...] = y
        if mode == EMIT_NORM:
            refs[7 + n_aux][...] = _rmsnorm(y, fn_ref[...]).astype(BF16)


def _ffn(x, g, wg, wu, wd, fn, *, mode, aux=(), tm=FFN_TM, tf=FFN_TF):
    T, D = x.shape
    FF = wg.shape[1]
    nj = FF // tf
    aux_in, aux_ops, aux_out, aux_shapes, aux_meta = _aux_plan(aux, nj, (T // tm) * nj)
    tok = lambda: pl.BlockSpec((tm, D), lambda i, j: (i, 0))
    main_shapes = [jax.ShapeDtypeStruct((T, D), F32)]
    if mode == EMIT_NORM:
        main_shapes.append(jax.ShapeDtypeStruct((T, D), BF16))
    main_specs = [tok() for _ in main_shapes]
    return pl.pallas_call(
        functools.partial(_ffn_kernel, mode=mode, aux_meta=aux_meta),
        out_shape=main_shapes + aux_shapes,
        grid=(T // tm, nj),
        in_specs=[
            tok(),
            pl.BlockSpec((1, D), lambda i, j: (0, 0)),
            pl.BlockSpec((D, tf), lambda i, j: (0, j)),
            pl.BlockSpec((D, tf), lambda i, j: (0, j)),
            pl.BlockSpec((tf, D), lambda i, j: (j, 0)),
            pl.BlockSpec((1, D), lambda i, j: (0, 0)),
        ] + aux_in,
        out_specs=main_specs + aux_out,
        scratch_shapes=[pltpu.VMEM((tm, D), BF16)],
        compiler_params=_cparams(("arbitrary", "arbitrary")),
        name="ffn_" + mode,
    )(x, g, wg, wu, wd, fn, *aux_ops)


def _ffn_pipelined(x, g, wg, wu, wd, fn, *, mode, tm=FFN_TM, tf=FFN_TF):
    T, D = x.shape
    FF = wg.shape[1]
    tok = pl.BlockSpec((tm, D), lambda i, j: (i, 0))
    vec = pl.BlockSpec((1, D), lambda i, j: (0, 0))
    steps = pltpu.emit_pipeline(
        functools.partial(_ffn_kernel, mode=mode, aux_meta=()),
        grid=(T // tm, FF // tf),
        in_specs=[tok, vec,
                  pl.BlockSpec((D, tf), lambda i, j: (0, j)),
                  pl.BlockSpec((D, tf), lambda i, j: (0, j)),
                  pl.BlockSpec((tf, D), lambda i, j: (j, 0)),
                  vec],
        out_specs=[tok])

    def body(x_hbm, g_hbm, wg_hbm, wu_hbm, wd_hbm, fn_hbm, o_hbm, h_scr):
        steps(x_hbm, g_hbm, wg_hbm, wu_hbm, wd_hbm, fn_hbm, o_hbm, scratches=(h_scr,))

    hbm = pl.BlockSpec(memory_space=pl.ANY)
    return pl.pallas_call(
        body,
        out_shape=jax.ShapeDtypeStruct((T, D), F32),
        in_specs=[hbm] * 6,
        out_specs=hbm,
        scratch_shapes=[pltpu.VMEM((tm, D), BF16)],
        compiler_params=pltpu.CompilerParams(vmem_limit_bytes=VMEM_LIMIT),
        name="ffn_" + mode + "_streamed",
    )(x, g, wg, wu, wd, fn)


def _proj_kernel(*refs, aux_meta):
    h_ref, w_ref = refs[:2]
    n_aux = sum(m[2] for m in aux_meta)
    _aux_run(aux_meta, refs[2:2 + n_aux], refs[3 + n_aux:])
    refs[2 + n_aux][...] = _dot_nt(h_ref[...], w_ref[...])


def _proj(h, w, *, aux=(), tm=PROJ_TM, tn=PROJ_TN):
    T, D = h.shape
    N = w.shape[0]
    nj = N // tn
    aux_in, aux_ops, aux_out, aux_shapes, aux_meta = _aux_plan(aux, nj, (T // tm) * nj)
    return pl.pallas_call(
        functools.partial(_proj_kernel, aux_meta=aux_meta),
        out_shape=[jax.ShapeDtypeStruct((T, N), F32)] + aux_shapes,
        grid=(T // tm, nj),
        in_specs=[
            pl.BlockSpec((tm, D), lambda i, j: (i, 0), pipeline_mode=pl.Buffered(1)),
            pl.BlockSpec((tn, D), lambda i, j: (j, 0)),
        ] + aux_in,
        out_specs=[pl.BlockSpec((tm, tn), lambda i, j: (i, j))] + aux_out,
        compiler_params=_cparams(("arbitrary", "arbitrary")),
        name="in_proj",
    )(h, w, *aux_ops)


def _chunk_tri(n):
    r = lax.broadcasted_iota(jnp.int32, (n, n), 0)
    c = lax.broadcasted_iota(jnp.int32, (n, n), 1)
    return jnp.where((r >= c) & (r // CHUNK == c // CHUNK), 1.0, 0.0).astype(BF16)


def _gla_kernel(p_ref, wa2_ref, ba_ref, gn_ref, o_ref, st_scr, *, tg):
    @pl.when(pl.program_id(1) == 0)
    def _():
        st_scr[...] = jnp.zeros_like(st_scr)

    gad = p_ref[:, GLA_AD:GLA_AD + LANE].astype(BF16)
    z = _dot(gad, wa2_ref[...]) + ba_ref[...]
    log_alpha = -_softplus(-z) * (1.0 / GLA_TAU)
    tri = _chunk_tri(tg)
    hi = log_alpha.astype(BF16)
    lo = (log_alpha - hi.astype(F32)).astype(BF16)
    cum = _dot(tri, hi) + _dot(tri, lo)
    gn = gn_ref[...]
    scale = GLA_DK ** -0.5

    n_chunks = tg // CHUNK
    rows = [slice(c * CHUNK, (c + 1) * CHUNK) for c in range(n_chunks)]
    keys = [slice(h * GLA_DK, (h + 1) * GLA_DK) for h in range(GLA_HEADS)]
    vals = [slice(h * GLA_DV, (h + 1) * GLA_DV) for h in range(GLA_HEADS)]
    units = [(c, h) for c in range(n_chunks) for h in range(GLA_HEADS)]
    tot = [cum[r, :][CHUNK - 1:CHUNK, :] for r in rows]
    etot = [jnp.exp(t) for t in tot]
    kdec = [(p_ref[rows[c], GLA_QK:2 * GLA_QK] * jnp.exp(tot[c] - cum[rows[c], :])).astype(BF16)
            for c in range(n_chunks)]
    q = [(p_ref[r, 0:GLA_QK] * scale).astype(BF16) for r in rows]
    inc = {}
    for c, h in units:
        v_h = p_ref[rows[c], 2 * GLA_QK + h * GLA_DV:2 * GLA_QK + (h + 1) * GLA_DV].astype(BF16)
        inc[c, h] = _dot_tn(v_h, kdec[c][:, keys[h]])
    st = {}
    for h in range(GLA_HEADS):
        prev = st_scr[h]
        for c in range(n_chunks):
            prev = prev * etot[c][:, keys[h]] + inc[c, h]
            st[c, h] = prev
        st_scr[h] = prev
    out = {u: _dot_nt(q[u[0]][:, keys[u[1]]], st[u].astype(BF16)) for u in units}
    for c, h in units:
        o = out[c, h]
        r_h = p_ref[rows[c], 2 * GLA_QK + GLA_V + h * GLA_DV:2 * GLA_QK + GLA_V + (h + 1) * GLA_DV]
        o = o * lax.rsqrt(jnp.mean(o * o, axis=-1, keepdims=True) + NORM_EPS) * gn
        o = o * (r_h * _sigmoid(r_h))
        o_ref[rows[c], vals[h]] = o.astype(BF16)


def _gla(p, wa2, ba, gn, B, S, *, tg=256):
    T = B * S
    nb = S // tg
    return pl.pallas_call(
        functools.partial(_gla_kernel, tg=tg),
        out_shape=jax.ShapeDtypeStruct((T, GLA_V), BF16),
        grid=(B, nb),
        in_specs=[
            pl.BlockSpec((tg, GLA_GROUP), lambda b, i: (b * nb + i, 1)),
            pl.BlockSpec((LANE, GLA_QK), lambda b, i: (0, 0)),
            pl.BlockSpec((1, GLA_QK), lambda b, i: (0, 0)),
            pl.BlockSpec((1, GLA_DV), lambda b, i: (0, 0)),
        ],
        out_specs=pl.BlockSpec((tg, GLA_V), lambda b, i: (b * nb + i, 0)),
        scratch_shapes=[pltpu.VMEM((GLA_HEADS, GLA_DV, GLA_DK), F32)],
        compiler_params=_cparams(("parallel", "arbitrary")),
        name="gla",
    )(p, wa2, ba, gn)


MXU_DIM = 256


def _head_ones(n):
    r = lax.broadcasted_iota(jnp.int32, (n, n), 0)
    c = lax.broadcasted_iota(jnp.int32, (n, n), 1)
    return jnp.where(r // RWKV_HD == c // RWKV_HD, 1.0, 0.0).astype(BF16)


def _head_sum(x, ones_blk):
    xb = x.astype(BF16)
    parts = [_dot(xb[:, j:j + MXU_DIM], ones_blk) for j in range(0, x.shape[1], MXU_DIM)]
    return jnp.concatenate(parts, axis=1)


def _rwkv_prep_kernel(p_ref, mu_ref, w0_ref, ww2_ref, a0_ref, wa2_ref, wg2_ref, kk_ref, ka_ref, rk_ref,
                      rt_ref, bt_ref, at_ref, kt_ref, v_ref, bonus_ref, g_ref, ptot_ref,
                      carry_scr, *, tm):
    @pl.when(pl.program_id(1) == 0)
    def _():
        carry_scr[...] = jnp.zeros_like(carry_scr)

    p = p_ref[...]
    last = carry_scr[...]
    carry_scr[...] = p[tm - 1:tm, :]
    prev = pltpu.roll(p, 1, axis=0)
    row = lax.broadcasted_iota(jnp.int32, (F32_SUBLANES, p.shape[1]), 0)
    prev = jnp.concatenate([jnp.where(row == 0, last, prev[:F32_SUBLANES]), prev[F32_SUBLANES:]], axis=0)
    p = p + mu_ref[...] * (prev - p)

    r = p[:, 0:RWKV_W]
    k = p[:, RWKV_W:2 * RWKV_W]
    v = p[:, 2 * RWKV_W:3 * RWKV_W]
    wd = p[:, RW_WD:RW_WD + LANE]
    ad = p[:, RW_AD:RW_AD + LANE]
    gd = p[:, RW_GD:RW_GD + GATE_LORA]

    w_raw = w0_ref[...] + _dot(jnp.tanh(wd).astype(BF16), ww2_ref[...])
    log_w = (-math.exp(-0.5)) * _sigmoid(w_raw)
    a = _sigmoid(a0_ref[...] + _dot(ad.astype(BF16), wa2_ref[...]))
    g_ref[...] = _dot(_sigmoid(gd).astype(BF16), wg2_ref[...])

    ones_blk = _head_ones(MXU_DIM)
    kk = k * kk_ref[...]
    kk = kk * lax.rsqrt(jnp.maximum(_head_sum(kk * kk, ones_blk), 1e-24))
    kp = k * (1.0 + (a - 1.0) * ka_ref[...])
    bonus_ref[...] = _head_sum(r * kp * rk_ref[...], ones_blk) * v
    v_ref[...] = v.astype(BF16)

    tri = _chunk_tri(tm)
    hi = log_w.astype(BF16)
    lo = (log_w - hi.astype(F32)).astype(BF16)
    cum = _dot(tri, hi) + _dot(tri, lo)
    nalpha = -(kk * a)
    for ci in range(tm // CHUNK):
        rows = slice(ci * CHUNK, (ci + 1) * CHUNK)
        cum_c = cum[rows, :]
        tot = cum_c[CHUNK - 1:CHUNK, :]
        e_inv = jnp.exp(-cum_c)
        rt_ref[rows, :] = (r[rows, :] * jnp.exp(cum_c)).astype(BF16)
        bt_ref[rows, :] = (kk[rows, :] * jnp.exp(cum_c - log_w[rows, :])).astype(BF16)
        at_ref[rows, :] = (nalpha[rows, :] * e_inv).astype(BF16)
        kt_ref[rows, :] = (kp[rows, :] * e_inv).astype(BF16)
        ptot_ref[ci] = jnp.exp(tot)


def _rwkv_prep(p, mu, w0, ww2, a0, wa2, wg2, k_k, k_a, r_k, B, S, *, tm=256):
    T = B * S
    nb = S // tm
    cpt = tm // CHUNK
    vec = lambda n: pl.BlockSpec((1, n), lambda b, i: (0, 0))
    tok = lambda: pl.BlockSpec((tm, RWKV_W), lambda b, i: (b * nb + i, 0))
    bf = jax.ShapeDtypeStruct((T, RWKV_W), BF16)
    f32 = jax.ShapeDtypeStruct((T, RWKV_W), F32)
    return pl.pallas_call(
        functools.partial(_rwkv_prep_kernel, tm=tm),
        out_shape=[bf, bf, bf, bf, bf, f32, f32,
                   jax.ShapeDtypeStruct((T // CHUNK, 1, RWKV_W), F32)],
        grid=(B, nb),
        in_specs=[
            pl.BlockSpec((tm, RW_GROUP), lambda b, i: (b * nb + i, 0)),
            vec(RW_GROUP), vec(RWKV_W),
            pl.BlockSpec((LANE, RWKV_W), lambda b, i: (0, 0)),
            vec(RWKV_W),
            pl.BlockSpec((LANE, RWKV_W), lambda b, i: (0, 0)),
            pl.BlockSpec((GATE_LORA, RWKV_W), lambda b, i: (0, 0)),
            vec(RWKV_W), vec(RWKV_W), vec(RWKV_W),
        ],
        out_specs=[tok() for _ in range(7)]
        + [pl.BlockSpec((cpt, 1, RWKV_W), lambda b, i: (b * nb + i, 0, 0))],
        scratch_shapes=[pltpu.VMEM((1, RW_GROUP), F32)],
        compiler_params=_cparams(("parallel", "arbitrary")),
        name="rwkv_prep",
    )(p, mu, w0, ww2, a0, wa2, wg2, k_k, k_a, r_k)


def _rwkv_core_kernel(rt_ref, bt_ref, at_ref, kt_ref, v_ref, bonus_ref, g_ref, ptot_ref,
                      lw_ref, lb_ref, o_ref, h_scr):
    @pl.when(pl.program_id(0) == 0)
    def _():
        h_scr[...] = jnp.zeros_like(h_scr)

    assert CHUNK == RWKV_HD
    n_batch = rt_ref.shape[0]
    ti = lax.broadcasted_iota(jnp.int32, (CHUNK, PAIR), 0)
    si = lax.broadcasted_iota(jnp.int32, (CHUNK, PAIR), 1) % RWKV_HD
    strict = ti > si
    lower = ti >= si
    eye = ti == si
    blk8 = (ti // 8) == (si // 8)
    eye_f = jnp.where(eye, 1.0, 0.0)
    ri = lax.broadcasted_iota(jnp.int32, (PAIR, PAIR), 0)
    ci = lax.broadcasted_iota(jnp.int32, (PAIR, PAIR), 1)
    head_blk = (ri // RWKV_HD) == (ci // RWKV_HD)
    ones_pair = jnp.where(head_blk, 1.0, 0.0).astype(BF16)
    ones_2 = jnp.concatenate([ones_pair, ones_pair], axis=0)
    zeros_b = jnp.zeros((PAIR, PAIR), BF16)

    def bd(x):
        return jnp.where(head_blk, jnp.concatenate([x, x], axis=0), jnp.zeros((), x.dtype))

    def head_t(x):
        xt = x.T
        return jnp.concatenate([xt[:RWKV_HD], xt[RWKV_HD:]], axis=1)

    def dot_packed(lhs, rhs_bd):
        out = []
        for j in range(0, len(lhs), 2):
            l2 = jnp.concatenate([lhs[j], lhs[j + 1]], axis=1)
            r2 = jnp.concatenate([jnp.concatenate([rhs_bd[j], zeros_b], axis=1),
                                  jnp.concatenate([zeros_b, rhs_bd[j + 1]], axis=1)], axis=0)
            o = _dot(l2, r2)
            out += [o[:, :PAIR], o[:, PAIR:]]
        return out

    bf = lambda xs: [x.astype(BF16) for x in xs]

    def head_sums(xs):
        hi = [x.astype(BF16) for x in xs]
        lo = [(x - h.astype(F32)).astype(BF16) for x, h in zip(xs, hi)]
        rows = jnp.concatenate([jnp.concatenate([h, l], axis=1) for h, l in zip(hi, lo)], axis=0)
        sums = _dot(rows, ones_2)
        return [sums[j * CHUNK:(j + 1) * CHUNK] for j in range(len(xs))]

    pairs = range(n_batch * N_PAIR)
    where = [(b, slice(j * PAIR, (j + 1) * PAIR)) for b in range(n_batch) for j in range(N_PAIR)]
    ptot = [ptot_ref[b, 0, :, l] for b, l in where]
    bt = [bt_ref[b, :, l] for b, l in where]
    rt = [rt_ref[b, :, l] for b, l in where]
    at = [at_ref[b, :, l] for b, l in where]
    kt = [kt_ref[b, :, l] for b, l in where]
    bx = [bd(x) for x in bt]
    vx = [bd(v_ref[b, :, l]) for b, l in where]
    ak = [jnp.concatenate([bd(at[j]), bd(kt[j])], axis=0) for j in pairs]
    ak2t = [jnp.concatenate([head_t(at[j].astype(F32) * ptot[j]).astype(BF16),
                             head_t(kt[j].astype(F32) * ptot[j]).astype(BF16)], axis=1) for j in pairs]

    g = [_dot_nt(jnp.concatenate([bt[j], rt[j]], axis=0), ak[j]) for j in pairs]
    gb = [x[:CHUNK] for x in g]
    gr = [x[CHUNK:] for x in g]
    a_ab = [jnp.where(strict, x[:, :PAIR], 0.0) for x in gb]
    a_kb = [jnp.where(strict, x[:, PAIR:], 0.0).astype(BF16) for x in gb]
    a_r = [jnp.concatenate([jnp.where(lower, x[:, :PAIR], 0.0),
                            jnp.where(lower, x[:, PAIR:], 0.0)], axis=1).astype(BF16) for x in gr]

    a_d = [jnp.where(blk8, x, 0.0) for x in a_ab]
    a_db = bf(a_d)
    pw = bf(dot_packed(a_db, [bd(x) for x in a_db]))
    s = [eye_f + x for x in a_d]
    sp = [_dot(pw[j], jnp.concatenate([bd(s[j].astype(BF16)), bd(pw[j])], axis=1)) for j in pairs]
    s = [s[j] + sp[j][:, :PAIR] for j in pairs]
    pw = [x[:, PAIR:].astype(BF16) for x in sp]
    ps = dot_packed(pw, [bd(x) for x in bf(s)])
    s = [s[j] + ps[j] for j in pairs]
    for width in (8, 16, 32):
        off = ((ti // (2 * width)) == (si // (2 * width))) & ((ti // width) != (si // width))
        e = [jnp.where(off, x, 0.0).astype(BF16) for x in a_ab]
        sb = bf(s)
        es = bf(dot_packed(e, [bd(x) for x in sb]))
        ses = dot_packed(sb, [bd(x) for x in es])
        s = [s[j] + ses[j] for j in pairs]
    t_inv = bf(s)

    kv = bf(dot_packed(a_kb, vx))
    wu = bf([_dot(t_inv[j], jnp.concatenate([bx[j], bd(kv[j])], axis=1)) for j in pairs])
    z = [jnp.concatenate([jnp.concatenate([bd(wu[j][:, :PAIR]), bd(wu[j][:, PAIR:])], axis=1),
                          jnp.concatenate([zeros_b, vx[j]], axis=1)], axis=0) for j in pairs]
    mcqy = [_dot(jnp.concatenate([ak2t[j], a_r[j]], axis=0), z[j]) for j in pairs]
    mc = [x[:CHUNK] for x in mcqy]
    qy = [x[CHUNK:] for x in mcqy]

    m = [(mc[j][:, :PAIR] + jnp.where(eye, ptot[j], 0.0)).astype(BF16) for j in pairs]
    q = [(qy[j][:, :PAIR] + rt[j].astype(F32)).astype(BF16) for j in pairs]
    hb = [bd(h_scr[j].astype(BF16)) for j in pairs]
    qmh = dot_packed([jnp.concatenate([q[j], m[j]], axis=0) for j in pairs], hb)
    for j in pairs:
        h_scr[j] = qmh[j][CHUNK:] + mc[j][:, PAIR:]
    y = [qmh[j][:CHUNK] + qy[j][:, PAIR:] for j in pairs]

    mean = [x * (1.0 / RWKV_HD) for x in head_sums(y)]
    yc = [y[j] - mean[j] for j in pairs]
    var = [x * (1.0 / RWKV_HD) for x in head_sums([x * x for x in yc])]
    for j in pairs:
        b, l = where[j]
        yn = yc[j] * lax.rsqrt(var[j] + GN_EPS) * lw_ref[:, l] + lb_ref[:, l]
        o_ref[b, :, l] = ((yn + bonus_ref[b, :, l]) * g_ref[b, :, l]).astype(BF16)


def _rwkv_core(rt, bt, at, kt, v, bonus, g, ptot, lnx_w, lnx_b, B, S):
    nc = S // CHUNK
    seq = lambda a: a.reshape(B, S, RWKV_W)
    tok = lambda: pl.BlockSpec((B, CHUNK, RWKV_W), lambda c: (0, c, 0))
    vec = lambda: pl.BlockSpec((1, RWKV_W), lambda c: (0, 0))
    out = pl.pallas_call(
        _rwkv_core_kernel,
        out_shape=jax.ShapeDtypeStruct((B, S, RWKV_W), BF16),
        grid=(nc,),
        in_specs=[tok() for _ in range(7)]
        + [pl.BlockSpec((B, 1, 1, RWKV_W), lambda c: (0, c, 0, 0)), vec(), vec()],
        out_specs=tok(),
        scratch_shapes=[pltpu.VMEM((B * N_PAIR, RWKV_HD, PAIR), F32)],
        compiler_params=_cparams(("arbitrary",)),
        name="rwkv_core",
    )(seq(rt), seq(bt), seq(at), seq(kt), seq(v), seq(bonus), seq(g),
      ptot.reshape(B, nc, 1, RWKV_W), lnx_w, lnx_b)
    return out.reshape(B * S, RWKV_W)


def _merge_kernel(x_ref, og_ref, or_ref, gb_ref, wb1_ref, wb2_ref, wo_ref, *rest):
    o_ref = rest[-1]
    gate_refs = rest[:-1]
    n = len(gate_refs) // 2
    tn = gate_refs[0].shape[1]
    D = x_ref.shape[1]
    y_gla = _dot(og_ref[...], wb1_ref[...])
    y_rw = _dot(or_ref[...], wb2_ref[...])
    merged = []
    for c in range(n):
        cols = slice(c * tn, (c + 1) * tn)
        g_gla = _sigmoid(gate_refs[c][...] + gb_ref[:, c * tn:(c + 1) * tn])
        g_rw = _sigmoid(gate_refs[n + c][...] + gb_ref[:, D + c * tn:D + (c + 1) * tn])
        merged.append((g_gla * y_gla[:, cols] + g_rw * y_rw[:, cols]).astype(BF16))
    o_ref[...] = x_ref[...] + _dot(jnp.concatenate(merged, axis=1), wo_ref[...])


def _merge(x, o_gla, o_rw, p, gate_b, w_branch, wo, *, tm=256, tn=PROJ_TN):
    T, D = x.shape
    const = lambda shape, r: pl.BlockSpec(shape, lambda i: (r, 0), pipeline_mode=pl.Buffered(1))
    assert GLA_V == RWKV_W and GATE_COL % tn == 0 and D % tn == 0
    n_gate_blocks = 2 * D // tn
    gate_specs = [pl.BlockSpec((tm, tn), lambda i, c=GATE_COL // tn + c: (i, c)) for c in range(n_gate_blocks)]
    return pl.pallas_call(
        _merge_kernel,
        out_shape=jax.ShapeDtypeStruct((T, D), F32),
        grid=(T // tm,),
        in_specs=[
            pl.BlockSpec((tm, D), lambda i: (i, 0)),
            pl.BlockSpec((tm, GLA_V), lambda i: (i, 0)),
            pl.BlockSpec((tm, RWKV_W), lambda i: (i, 0)),
            pl.BlockSpec((1, 2 * D), lambda i: (0, 0)),
            const((GLA_V, D), 0), const((RWKV_W, D), 1), const((D, D), 0),
        ] + gate_specs,
        out_specs=pl.BlockSpec((tm, D), lambda i: (i, 0)),
        compiler_params=_cparams(("parallel",)),
        name="merge_out",
    )(x, o_gla, o_rw, gate_b, w_branch, w_branch, wo, *([p] * n_gate_blocks))


def _pad_rows(w, n):
    return jnp.pad(w, ((0, n - w.shape[0]), (0, 0)))


def kernel(x, ffn1_norm, ffn1_wg, ffn1_wu, ffn1_wd, mix_norm, w_in, gla_w_a2, gla_b_a, gla_gn_w, rwkv_mu,
           rwkv_w0, rwkv_w_w2, rwkv_a0, rwkv_w_a2, rwkv_w_g2, rwkv_k_k, rwkv_k_a, rwkv_r_k, rwkv_lnx_w,
           rwkv_lnx_b, gate_b, w_branch, w_out, ffn2_norm, ffn2_wg, ffn2_wu, ffn2_wd, final_norm):
    B, S, D = x.shape
    T = B * S
    depth = ffn1_norm.shape[0]
    assert depth >= 1
    bf = lambda w: w.astype(BF16)
    row = lambda v: v.reshape(1, -1)
    assert D == D_MODEL and w_in.shape[2] == GLA_IN + RWKV_IN + 2 * D_MODEL
    xt = x.reshape(T, D)
    for l in range(depth):
        last = l == depth - 1
        xt, h_mix, w_p = _ffn(
            xt, row(ffn1_norm[l]), bf(ffn1_wg[l]), bf(ffn1_wu[l]), bf(ffn1_wd[l]), row(mix_norm[l]),
            mode=EMIT_NORM, aux=[(PACK_T, w_in[l].T)])

        mu_row = jnp.pad(rwkv_mu[l], (GLA_IN, 2 * D_MODEL))[None, :]
        mu = _pack_cols(mu_row)[:, :RW_GROUP]
        p, w_br, w_o, wg2, wu2, wd2 = _proj(
            h_mix, w_p, aux=[(CAST, w_branch[l]), (CAST, w_out[l]),
                             (CAST, ffn2_wg[l]), (CAST, ffn2_wu[l]), (CAST, ffn2_wd[l])])

        o_gla = _gla(p, bf(_pad_rows(gla_w_a2[l], LANE)), row(gla_b_a[l]), row(gla_gn_w[l]), B, S)

        prep = _rwkv_prep(p, mu, row(rwkv_w0[l]), bf(_pad_rows(rwkv_w_w2[l], LANE)), row(rwkv_a0[l]),
                          bf(_pad_rows(rwkv_w_a2[l], LANE)), bf(rwkv_w_g2[l]), row(rwkv_k_k[l]),
                          row(rwkv_k_a[l]), row(rwkv_r_k[l]), B, S)
        o_rw = _rwkv_core(*prep, row(rwkv_lnx_w[l]), row(rwkv_lnx_b[l]), B, S)

        xt = _merge(xt, o_gla, o_rw, p, row(gate_b[l]), w_br, w_o)

        xt = _ffn_pipelined(xt, row(ffn2_norm[l]), wg2, wu2, wd2, row(final_norm),
                            mode=FINAL if last else RESIDUAL)
    return xt.reshape(B, S, D)
```

```python
import functools
import math

import jax
import jax.numpy as jnp
from jax import lax
from jax.experimental import pallas as pl
from jax.experimental.pallas import tpu as pltpu

F32 = jnp.float32
BF16 = jnp.bfloat16

NORM_EPS = 1e-6
GN_EPS = 64e-5
GLA_TAU = 16.0
CHUNK = 64

GLA_HEADS = 4
GLA_DK = 128
GLA_DV = 256
GLA_QK = GLA_HEADS * GLA_DK
GLA_V = GLA_HEADS * GLA_DV
GLA_LORA = 16

RWKV_HD = 64
RWKV_W = 1024
DECAY_LORA = 96
AAA_LORA = 96
GATE_LORA = 256

LANE = 128
PAIR = 2 * RWKV_HD
N_PAIR = RWKV_W // PAIR

RW_GROUP = 3 * RWKV_W + LANE + LANE + GATE_LORA
GLA_GROUP = RW_GROUP
RW_WD = 3 * RWKV_W
GLA_AD = 2 * GLA_QK + 2 * GLA_V
RW_AD = RW_WD + LANE
RW_GD = RW_AD + LANE

VMEM_LIMIT = 60 * 1024 * 1024
BF16_SUBLANES = 16
F32_SUBLANES = 8

FFN_TM, FFN_TF = 512, 512
PROJ_TM, PROJ_TN = 2048, 1024


def _rows_per_block(rows, steps):
    for rb in range(BF16_SUBLANES, rows + 1, BF16_SUBLANES):
        if rows % rb == 0 and rows // rb <= steps:
            return rb
    raise ValueError(f"cannot split {rows} rows over {steps} steps")


def _cparams(sem):
    return pltpu.CompilerParams(dimension_semantics=sem, vmem_limit_bytes=VMEM_LIMIT)


def _dot(a, b):
    return jnp.dot(a, b, preferred_element_type=F32)


def _dot_nt(a, b):
    return lax.dot_general(a, b, (((1,), (1,)), ((), ())), preferred_element_type=F32)


def _dot_tn(a, b):
    return lax.dot_general(a, b, (((0,), (0,)), ((), ())), preferred_element_type=F32)


def _rmsnorm(x, g):
    return x * lax.rsqrt(jnp.mean(x * x, axis=-1, keepdims=True) + NORM_EPS) * g


def _sigmoid(z):
    return 0.5 + 0.5 * jnp.tanh(0.5 * z)


def _softplus(z):
    return jnp.maximum(z, 0.0) + jnp.log(1.0 + jnp.exp(-jnp.abs(z)))


CAST = "cast"
PACK_T = "pack_t"

D_MODEL = 2048
GLA_IN = 2 * GLA_QK + 2 * GLA_V + GLA_LORA
RWKV_IN = 3 * RWKV_W + DECAY_LORA + AAA_LORA + GATE_LORA
GATE_COL = RW_GROUP + GLA_GROUP
PACKED_COLS = GATE_COL + 2 * D_MODEL

IN_PIECES = (
    (0, GLA_IN, 3 * RWKV_W),
    (RW_WD, GLA_IN + 3 * RWKV_W, DECAY_LORA),
    (RW_AD, GLA_IN + 3 * RWKV_W + DECAY_LORA, AAA_LORA),
    (RW_GD, GLA_IN + 3 * RWKV_W + DECAY_LORA + AAA_LORA, GATE_LORA),
    (RW_GROUP, 0, GLA_IN),
    (GATE_COL, GLA_IN + RWKV_IN, 2 * D_MODEL),
)


def _pack_cols(w):
    zeros = lambda n: jnp.zeros((w.shape[0], n), w.dtype)
    parts, pos = [], 0
    for start, w_start, width in IN_PIECES:
        parts += [zeros(start - pos), w[:, w_start:w_start + width]]
        pos = start + width
    assert pos == PACKED_COLS
    return jnp.concatenate([p for p in parts if p.shape[1]], axis=1)


def _pack_src(k):
    col = k * LANE
    src = jnp.zeros_like(col)
    valid = jnp.zeros_like(col)
    for start, w_start, width in IN_PIECES:
        inside = (col >= start) & (col < start + width)
        src = jnp.where(inside, w_start + col - start, src)
        valid = jnp.where(inside, jnp.minimum(start + width - col, LANE), valid)
    return src, valid


def _aux_plan(aux, nj, steps):
    in_specs, operands, out_specs, out_shapes, meta = [], [], [], [], []
    for kind, w in aux:
        rows, cols = w.shape
        if kind == CAST:
            rb = _rows_per_block(rows, steps)
            n_active = rows // rb
            index = lambda i, j, n=n_active: (jnp.minimum(i * nj + j, n - 1), 0)
            in_specs.append(pl.BlockSpec((rb, cols), index))
            operands.append(w)
            out_specs.append(pl.BlockSpec((rb, cols), index))
            out_shapes.append(jax.ShapeDtypeStruct((rows, cols), BF16))
            meta.append((kind, n_active, 1, 1))
        else:
            n_blocks = PACKED_COLS // LANE
            per_step = next(d for d in range(1, n_blocks + 1) if n_blocks % d == 0 and n_blocks // d <= steps)
            n_active = n_blocks // per_step
            last = n_active - 1
            for q in range(per_step):
                src = lambda i, j, q=q, last=last, per_step=per_step: (
                    pl.multiple_of(_pack_src(jnp.minimum(i * nj + j, last) * per_step + q)[0],
                                   BF16_SUBLANES), 0)
                in_specs.append(pl.BlockSpec((pl.Element(LANE), pl.Element(cols)), src))
                operands.append(w)
            out_specs.append(pl.BlockSpec((per_step * LANE, cols),
                                          lambda i, j, last=last: (jnp.minimum(i * nj + j, last), 0)))
            out_shapes.append(jax.ShapeDtypeStruct((PACKED_COLS, cols), BF16))
            meta.append((kind, n_active, per_step, 1))
    return in_specs, operands, out_specs, out_shapes, tuple(meta)


def _aux_run(meta, in_refs, out_refs):
    step = pl.program_id(0) * pl.num_programs(1) + pl.program_id(1)
    ki = ko = 0
    for kind, n_active, n_in, n_out in meta:
        ins, outs = in_refs[ki:ki + n_in], out_refs[ko:ko + n_out]
        ki += n_in
        ko += n_out
        if kind == CAST:
            outs[0][...] = ins[0][...].astype(BF16)
        else:
            block = jnp.minimum(step, n_active - 1) * n_in
            for q, w_ref in enumerate(ins):
                w = w_ref[...]
                _, valid = _pack_src(block + q)
                row = lax.broadcasted_iota(jnp.int32, w.shape, 0)
                outs[0][q * LANE:(q + 1) * LANE, :] = jnp.where(row < valid, w, 0.0).astype(BF16)


RESIDUAL = "residual"
FINAL = "final"
EMIT_NORM = "emit"


def _ffn_kernel(*refs, mode, aux_meta):
    x_ref, g_ref, wg_ref, wu_ref, wd_ref, fn_ref = refs[:6]
    n_aux = sum(m[2] for m in aux_meta)
    n_main_out = 2 if mode == EMIT_NORM else 1
    o_ref = refs[6 + n_aux]
    h_scr = refs[-1]
    j = pl.program_id(1)

    @pl.when(j == 0)
    def _():
        h_scr[...] = _rmsnorm(x_ref[...], g_ref[...]).astype(BF16)
        o_ref[...] = jnp.zeros_like(o_ref)

    _aux_run(aux_meta, refs[6:6 + n_aux], refs[6 + n_aux + n_main_out:-1])
    h = h_scr[...]
    a = _dot(h, wg_ref[...])
    u = _dot(h, wu_ref[...])
    act = (a * _sigmoid(a) * u).astype(BF16)
    o_ref[...] += _dot(act, wd_ref[...])

    @pl.when(j == pl.num_programs(1) - 1)
    def _():
        y = x_ref[...] + 0.5 * o_ref[...]
        if mode == FINAL:
            y = _rmsnorm(y, fn_ref[...])
        o_ref[...] = y
        if mode == EMIT_NORM:
            refs[7 + n_aux][...] = _rmsnorm(y, fn_ref[...]).astype(BF16)


def _ffn(x, g, wg, wu, wd, fn, *, mode, aux=(), tm=FFN_TM, tf=FFN_TF):
    T, D = x.shape
    FF = wg.shape[1]
    nj = FF // tf
    aux_in, aux_ops, aux_out, aux_shapes, aux_meta = _aux_plan(aux, nj, (T // tm) * nj)
    tok = lambda: pl.BlockSpec((tm, D), lambda i, j: (i, 0))
    main_shapes = [jax.ShapeDtypeStruct((T, D), F32)]
    if mode == EMIT_NORM:
        main_shapes.append(jax.ShapeDtypeStruct((T, D), BF16))
    main_specs = [tok() for _ in main_shapes]
    return pl.pallas_call(
        functools.partial(_ffn_kernel, mode=mode, aux_meta=aux_meta),
        out_shape=main_shapes + aux_shapes,
        grid=(T // tm, nj),
        in_specs=[
            tok(),
            pl.BlockSpec((1, D), lambda i, j: (0, 0)),
            pl.BlockSpec((D, tf), lambda i, j: (0, j)),
            pl.BlockSpec((D, tf), lambda i, j: (0, j)),
            pl.BlockSpec((tf, D), lambda i, j: (j, 0)),
            pl.BlockSpec((1, D), lambda i, j: (0, 0)),
        ] + aux_in,
        out_specs=main_specs + aux_out,
        scratch_shapes=[pltpu.VMEM((tm, D), BF16)],
        compiler_params=_cparams(("arbitrary", "arbitrary")),
        name="ffn_" + mode,
    )(x, g, wg, wu, wd, fn, *aux_ops)


def _ffn_pipelined(x, g, wg, wu, wd, fn, *, mode, tm=FFN_TM, tf=FFN_TF):
    T, D = x.shape
    FF = wg.shape[1]
    tok = pl.BlockSpec((tm, D), lambda i, j: (i, 0))
    vec = pl.BlockSpec((1, D), lambda i, j: (0, 0))
    steps = pltpu.emit_pipeline(
        functools.partial(_ffn_kernel, mode=mode, aux_meta=()),
        grid=(T // tm, FF // tf),
        in_specs=[tok, vec,
                  pl.BlockSpec((D, tf), lambda i, j: (0, j), pipeline_mode=pl.Buffered(3)),
                  pl.BlockSpec((D, tf), lambda i, j: (0, j), pipeline_mode=pl.Buffered(3)),
                  pl.BlockSpec((tf, D), lambda i, j: (j, 0), pipeline_mode=pl.Buffered(3)),
                  vec],
        out_specs=[tok])

    def body(x_hbm, g_hbm, wg_hbm, wu_hbm, wd_hbm, fn_hbm, o_hbm, h_scr):
        steps(x_hbm, g_hbm, wg_hbm, wu_hbm, wd_hbm, fn_hbm, o_hbm, scratches=(h_scr,))

    hbm = pl.BlockSpec(memory_space=pl.ANY)
    return pl.pallas_call(
        body,
        out_shape=jax.ShapeDtypeStruct((T, D), F32),
        in_specs=[hbm] * 6,
        out_specs=hbm,
        scratch_shapes=[pltpu.VMEM((tm, D), BF16)],
        compiler_params=pltpu.CompilerParams(vmem_limit_bytes=VMEM_LIMIT),
        name="ffn_" + mode + "_streamed",
    )(x, g, wg, wu, wd, fn)


def _proj_kernel(*refs, aux_meta):
    h_ref, w_ref = refs[:2]
    n_aux = sum(m[2] for m in aux_meta)
    _aux_run(aux_meta, refs[2:2 + n_aux], refs[3 + n_aux:])
    refs[2 + n_aux][...] = _dot_nt(h_ref[...], w_ref[...])


def _proj(h, w, *, aux=(), tm=PROJ_TM, tn=PROJ_TN):
    T, D = h.shape
    N = w.shape[0]
    nj = N // tn
    aux_in, aux_ops, aux_out, aux_shapes, aux_meta = _aux_plan(aux, nj, (T // tm) * nj)
    return pl.pallas_call(
        functools.partial(_proj_kernel, aux_meta=aux_meta),
        out_shape=[jax.ShapeDtypeStruct((T, N), F32)] + aux_shapes,
        grid=(T // tm, nj),
        in_specs=[
            pl.BlockSpec((tm, D), lambda i, j: (i, 0), pipeline_mode=pl.Buffered(1)),
            pl.BlockSpec((tn, D), lambda i, j: (j, 0)),
        ] + aux_in,
        out_specs=[pl.BlockSpec((tm, tn), lambda i, j: (i, j))] + aux_out,
        compiler_params=_cparams(("arbitrary", "arbitrary")),
        name="in_proj",
    )(h, w, *aux_ops)


def _chunk_tri(n):
    r = lax.broadcasted_iota(jnp.int32, (n, n), 0)
    c = lax.broadcasted_iota(jnp.int32, (n, n), 1)
    return jnp.where((r >= c) & (r // CHUNK == c // CHUNK), 1.0, 0.0).astype(BF16)


def _gla_kernel(p_ref, wa2_ref, ba_ref, gn_ref, o_ref, st_scr, *, tg):
    @pl.when(pl.program_id(1) == 0)
    def _():
        st_scr[...] = jnp.zeros_like(st_scr)

    gad = p_ref[:, GLA_AD:GLA_AD + LANE].astype(BF16)
    z = _dot(gad, wa2_ref[...]) + ba_ref[...]
    log_alpha = -_softplus(-z) * (1.0 / GLA_TAU)
    tri = _chunk_tri(tg)
    hi = log_alpha.astype(BF16)
    lo = (log_alpha - hi.astype(F32)).astype(BF16)
    cum = _dot(tri, hi) + _dot(tri, lo)
    gn = gn_ref[...]
    scale = GLA_DK ** -0.5

    n_chunks = tg // CHUNK
    rows = [slice(c * CHUNK, (c + 1) * CHUNK) for c in range(n_chunks)]
    keys = [slice(h * GLA_DK, (h + 1) * GLA_DK) for h in range(GLA_HEADS)]
    vals = [slice(h * GLA_DV, (h + 1) * GLA_DV) for h in range(GLA_HEADS)]
    units = [(c, h) for c in range(n_chunks) for h in range(GLA_HEADS)]
    tot = [cum[r, :][CHUNK - 1:CHUNK, :] for r in rows]
    etot = [jnp.exp(t) for t in tot]
    kdec = [(p_ref[rows[c], GLA_QK:2 * GLA_QK] * jnp.exp(tot[c] - cum[rows[c], :])).astype(BF16)
            for c in range(n_chunks)]
    q = [(p_ref[r, 0:GLA_QK] * scale).astype(BF16) for r in rows]
    inc = {}
    for c, h in units:
        v_h = p_ref[rows[c], 2 * GLA_QK + h * GLA_DV:2 * GLA_QK + (h + 1) * GLA_DV].astype(BF16)
        inc[c, h] = _dot_tn(v_h, kdec[c][:, keys[h]])
    st = {}
    for h in range(GLA_HEADS):
        prev = st_scr[h]
        for c in range(n_chunks):
            prev = prev * etot[c][:, keys[h]] + inc[c, h]
            st[c, h] = prev
        st_scr[h] = prev
    out = {u: _dot_nt(q[u[0]][:, keys[u[1]]], st[u].astype(BF16)) for u in units}
    for c, h in units:
        o = out[c, h]
        r_h = p_ref[rows[c], 2 * GLA_QK + GLA_V + h * GLA_DV:2 * GLA_QK + GLA_V + (h + 1) * GLA_DV]
        o = o * lax.rsqrt(jnp.mean(o * o, axis=-1, keepdims=True) + NORM_EPS) * gn
        o = o * (r_h * _sigmoid(r_h))
        o_ref[rows[c], vals[h]] = o.astype(BF16)


def _gla(p, wa2, ba, gn, B, S, *, tg=256):
    T = B * S
    nb = S // tg
    return pl.pallas_call(
        functools.partial(_gla_kernel, tg=tg),
        out_shape=jax.ShapeDtypeStruct((T, GLA_V), BF16),
        grid=(B, nb),
        in_specs=[
            pl.BlockSpec((tg, GLA_GROUP), lambda b, i: (b * nb + i, 1)),
            pl.BlockSpec((LANE, GLA_QK), lambda b, i: (0, 0)),
            pl.BlockSpec((1, GLA_QK), lambda b, i: (0, 0)),
            pl.BlockSpec((1, GLA_DV), lambda b, i: (0, 0)),
        ],
        out_specs=pl.BlockSpec((tg, GLA_V), lambda b, i: (b * nb + i, 0)),
        scratch_shapes=[pltpu.VMEM((GLA_HEADS, GLA_DV, GLA_DK), F32)],
        compiler_params=_cparams(("parallel", "arbitrary")),
        name="gla",
    )(p, wa2, ba, gn)


MXU_DIM = 256


def _head_ones(n):
    r = lax.broadcasted_iota(jnp.int32, (n, n), 0)
    c = lax.broadcasted_iota(jnp.int32, (n, n), 1)
    return jnp.where(r // RWKV_HD == c // RWKV_HD, 1.0, 0.0).astype(BF16)


def _head_sum(x, ones_blk):
    xb = x.astype(BF16)
    parts = [_dot(xb[:, j:j + MXU_DIM], ones_blk) for j in range(0, x.shape[1], MXU_DIM)]
    return jnp.concatenate(parts, axis=1)


def _rwkv_prep_kernel(p_ref, mu_ref, w0_ref, ww2_ref, a0_ref, wa2_ref, wg2_ref, kk_ref, ka_ref, rk_ref,
                      rt_ref, bt_ref, at_ref, kt_ref, v_ref, bonus_ref, g_ref, ptot_ref,
                      carry_scr, *, tm):
    @pl.when(pl.program_id(1) == 0)
    def _():
        carry_scr[...] = jnp.zeros_like(carry_scr)

    p = p_ref[...]
    last = carry_scr[...]
    carry_scr[...] = p[tm - 1:tm, :]
    prev = pltpu.roll(p, 1, axis=0)
    row = lax.broadcasted_iota(jnp.int32, (F32_SUBLANES, p.shape[1]), 0)
    prev = jnp.concatenate([jnp.where(row == 0, last, prev[:F32_SUBLANES]), prev[F32_SUBLANES:]], axis=0)
    p = p + mu_ref[...] * (prev - p)

    r = p[:, 0:RWKV_W]
    k = p[:, RWKV_W:2 * RWKV_W]
    v = p[:, 2 * RWKV_W:3 * RWKV_W]
    wd = p[:, RW_WD:RW_WD + LANE]
    ad = p[:, RW_AD:RW_AD + LANE]
    gd = p[:, RW_GD:RW_GD + GATE_LORA]

    w_raw = w0_ref[...] + _dot(jnp.tanh(wd).astype(BF16), ww2_ref[...])
    log_w = (-math.exp(-0.5)) * _sigmoid(w_raw)
    a = _sigmoid(a0_ref[...] + _dot(ad.astype(BF16), wa2_ref[...]))
    g_ref[...] = _dot(_sigmoid(gd).astype(BF16), wg2_ref[...])

    ones_blk = _head_ones(MXU_DIM)
    kk = k * kk_ref[...]
    kk = kk * lax.rsqrt(jnp.maximum(_head_sum(kk * kk, ones_blk), 1e-24))
    kp = k * (1.0 + (a - 1.0) * ka_ref[...])
    bonus_ref[...] = _head_sum(r * kp * rk_ref[...], ones_blk) * v
    v_ref[...] = v.astype(BF16)

    tri = _chunk_tri(tm)
    hi = log_w.astype(BF16)
    lo = (log_w - hi.astype(F32)).astype(BF16)
    cum = _dot(tri, hi) + _dot(tri, lo)
    nalpha = -(kk * a)
    for ci in range(tm // CHUNK):
        rows = slice(ci * CHUNK, (ci + 1) * CHUNK)
        cum_c = cum[rows, :]
        tot = cum_c[CHUNK - 1:CHUNK, :]
        e_inv = jnp.exp(-cum_c)
        rt_ref[rows, :] = (r[rows, :] * jnp.exp(cum_c)).astype(BF16)
        bt_ref[rows, :] = (kk[rows, :] * jnp.exp(cum_c - log_w[rows, :])).astype(BF16)
        at_ref[rows, :] = (nalpha[rows, :] * e_inv).astype(BF16)
        kt_ref[rows, :] = (kp[rows, :] * e_inv).astype(BF16)
        ptot_ref[ci] = jnp.exp(tot)


def _rwkv_prep(p, mu, w0, ww2, a0, wa2, wg2, k_k, k_a, r_k, B, S, *, tm=256):
    T = B * S
    nb = S // tm
    cpt = tm // CHUNK
    vec = lambda n: pl.BlockSpec((1, n), lambda b, i: (0, 0))
    tok = lambda: pl.BlockSpec((tm, RWKV_W), lambda b, i: (b * nb + i, 0))
    bf = jax.ShapeDtypeStruct((T, RWKV_W), BF16)
    f32 = jax.ShapeDtypeStruct((T, RWKV_W), F32)
    return pl.pallas_call(
        functools.partial(_rwkv_prep_kernel, tm=tm),
        out_shape=[bf, bf, bf, bf, bf, f32, f32,
                   jax.ShapeDtypeStruct((T // CHUNK, 1, RWKV_W), F32)],
        grid=(B, nb),
        in_specs=[
            pl.BlockSpec((tm, RW_GROUP), lambda b, i: (b * nb + i, 0)),
            vec(RW_GROUP), vec(RWKV_W),
            pl.BlockSpec((LANE, RWKV_W), lambda b, i: (0, 0)),
            vec(RWKV_W),
            pl.BlockSpec((LANE, RWKV_W), lambda b, i: (0, 0)),
            pl.BlockSpec((GATE_LORA, RWKV_W), lambda b, i: (0, 0)),
            vec(RWKV_W), vec(RWKV_W), vec(RWKV_W),
        ],
        out_specs=[tok() for _ in range(7)]
        + [pl.BlockSpec((cpt, 1, RWKV_W), lambda b, i: (b * nb + i, 0, 0))],
        scratch_shapes=[pltpu.VMEM((1, RW_GROUP), F32)],
        compiler_params=_cparams(("parallel", "arbitrary")),
        name="rwkv_prep",
    )(p, mu, w0, ww2, a0, wa2, wg2, k_k, k_a, r_k)


def _rwkv_core_kernel(rt_ref, bt_ref, at_ref, kt_ref, v_ref, bonus_ref, g_ref, ptot_ref,
                      lw_ref, lb_ref, o_ref, h_scr):
    @pl.when(pl.program_id(0) == 0)
    def _():
        h_scr[...] = jnp.zeros_like(h_scr)

    assert CHUNK == RWKV_HD
    n_batch = rt_ref.shape[0]
    ti = lax.broadcasted_iota(jnp.int32, (CHUNK, PAIR), 0)
    si = lax.broadcasted_iota(jnp.int32, (CHUNK, PAIR), 1) % RWKV_HD
    strict = ti > si
    lower = ti >= si
    eye = ti == si
    blk8 = (ti // 8) == (si // 8)
    eye_f = jnp.where(eye, 1.0, 0.0)
    ri = lax.broadcasted_iota(jnp.int32, (PAIR, PAIR), 0)
    ci = lax.broadcasted_iota(jnp.int32, (PAIR, PAIR), 1)
    head_blk = (ri // RWKV_HD) == (ci // RWKV_HD)
    ones_pair = jnp.where(head_blk, 1.0, 0.0).astype(BF16)
    ones_2 = jnp.concatenate([ones_pair, ones_pair], axis=0)
    zeros_b = jnp.zeros((PAIR, PAIR), BF16)

    def bd(x):
        return jnp.where(head_blk, jnp.concatenate([x, x], axis=0), jnp.zeros((), x.dtype))

    def head_t(x):
        xt = x.T
        return jnp.concatenate([xt[:RWKV_HD], xt[RWKV_HD:]], axis=1)

    def dot_packed(lhs, rhs_bd):
        out = []
        for j in range(0, len(lhs), 2):
            l2 = jnp.concatenate([lhs[j], lhs[j + 1]], axis=1)
            r2 = jnp.concatenate([jnp.concatenate([rhs_bd[j], zeros_b], axis=1),
                                  jnp.concatenate([zeros_b, rhs_bd[j + 1]], axis=1)], axis=0)
            o = _dot(l2, r2)
            out += [o[:, :PAIR], o[:, PAIR:]]
        return out

    bf = lambda xs: [x.astype(BF16) for x in xs]

    def head_sums(xs):
        hi = [x.astype(BF16) for x in xs]
        lo = [(x - h.astype(F32)).astype(BF16) for x, h in zip(xs, hi)]
        rows = jnp.concatenate([jnp.concatenate([h, l], axis=1) for h, l in zip(hi, lo)], axis=0)
        sums = _dot(rows, ones_2)
        return [sums[j * CHUNK:(j + 1) * CHUNK] for j in range(len(xs))]

    pairs = range(n_batch * N_PAIR)
    where = [(b, slice(j * PAIR, (j + 1) * PAIR)) for b in range(n_batch) for j in range(N_PAIR)]
    ptot = [ptot_ref[b, 0, :, l] for b, l in where]
    bt = [bt_ref[b, :, l] for b, l in where]
    rt = [rt_ref[b, :, l] for b, l in where]
    at = [at_ref[b, :, l] for b, l in where]
    kt = [kt_ref[b, :, l] for b, l in where]
    bx = [bd(x) for x in bt]
    vx = [bd(v_ref[b, :, l]) for b, l in where]
    ak = [jnp.concatenate([bd(at[j]), bd(kt[j])], axis=0) for j in pairs]
    ak2t = [jnp.concatenate([head_t(at[j].astype(F32) * ptot[j]).astype(BF16),
                             head_t(kt[j].astype(F32) * ptot[j]).astype(BF16)], axis=1) for j in pairs]

    g = [_dot_nt(jnp.concatenate([bt[j], rt[j]], axis=0), ak[j]) for j in pairs]
    gb = [x[:CHUNK] for x in g]
    gr = [x[CHUNK:] for x in g]
    a_ab = [jnp.where(strict, x[:, :PAIR], 0.0) for x in gb]
    a_kb = [jnp.where(strict, x[:, PAIR:], 0.0).astype(BF16) for x in gb]
    a_r = [jnp.concatenate([jnp.where(lower, x[:, :PAIR], 0.0),
                            jnp.where(lower, x[:, PAIR:], 0.0)], axis=1).astype(BF16) for x in gr]

    a_d = [jnp.where(blk8, x, 0.0) for x in a_ab]
    a_db = bf(a_d)
    pw = bf(dot_packed(a_db, [bd(x) for x in a_db]))
    s = [eye_f + x for x in a_d]
    sp = [_dot(pw[j], jnp.concatenate([bd(s[j].astype(BF16)), bd(pw[j])], axis=1)) for j in pairs]
    s = [s[j] + sp[j][:, :PAIR] for j in pairs]
    pw = [x[:, PAIR:].astype(BF16) for x in sp]
    ps = dot_packed(pw, [bd(x) for x in bf(s)])
    s = [s[j] + ps[j] for j in pairs]
    for width in (8, 16, 32):
        off = ((ti // (2 * width)) == (si // (2 * width))) & ((ti // width) != (si // width))
        e = [jnp.where(off, x, 0.0).astype(BF16) for x in a_ab]
        sb = bf(s)
        es = bf(dot_packed(e, [bd(x) for x in sb]))
        ses = dot_packed(sb, [bd(x) for x in es])
        s = [s[j] + ses[j] for j in pairs]
    t_inv = bf(s)

    kv = bf(dot_packed(a_kb, vx))
    wu = bf([_dot(t_inv[j], jnp.concatenate([bx[j], bd(kv[j])], axis=1)) for j in pairs])
    z = [jnp.concatenate([jnp.concatenate([bd(wu[j][:, :PAIR]), bd(wu[j][:, PAIR:])], axis=1),
                          jnp.concatenate([zeros_b, vx[j]], axis=1)], axis=0) for j in pairs]
    mcqy = [_dot(jnp.concatenate([ak2t[j], a_r[j]], axis=0), z[j]) for j in pairs]
    mc = [x[:CHUNK] for x in mcqy]
    qy = [x[CHUNK:] for x in mcqy]

    m = [(mc[j][:, :PAIR] + jnp.where(eye, ptot[j], 0.0)).astype(BF16) for j in pairs]
    q = [(qy[j][:, :PAIR] + rt[j].astype(F32)).astype(BF16) for j in pairs]
    hb = [bd(h_scr[j].astype(BF16)) for j in pairs]
    qmh = dot_packed([jnp.concatenate([q[j], m[j]], axis=0) for j in pairs], hb)
    for j in pairs:
        h_scr[j] = qmh[j][CHUNK:] + mc[j][:, PAIR:]
    y = [qmh[j][:CHUNK] + qy[j][:, PAIR:] for j in pairs]

    mean = [x * (1.0 / RWKV_HD) for x in head_sums(y)]
    yc = [y[j] - mean[j] for j in pairs]
    var = [x * (1.0 / RWKV_HD) for x in head_sums([x * x for x in yc])]
    for j in pairs:
        b, l = where[j]
        yn = yc[j] * lax.rsqrt(var[j] + GN_EPS) * lw_ref[:, l] + lb_ref[:, l]
        o_ref[b, :, l] = ((yn + bonus_ref[b, :, l]) * g_ref[b, :, l]).astype(BF16)


def _rwkv_core(rt, bt, at, kt, v, bonus, g, ptot, lnx_w, lnx_b, B, S):
    nc = S // CHUNK
    seq = lambda a: a.reshape(B, S, RWKV_W)
    tok = lambda: pl.BlockSpec((B, CHUNK, RWKV_W), lambda c: (0, c, 0))
    vec = lambda: pl.BlockSpec((1, RWKV_W), lambda c: (0, 0))
    out = pl.pallas_call(
        _rwkv_core_kernel,
        out_shape=jax.ShapeDtypeStruct((B, S, RWKV_W), BF16),
        grid=(nc,),
        in_specs=[tok() for _ in range(7)]
        + [pl.BlockSpec((B, 1, 1, RWKV_W), lambda c: (0, c, 0, 0)), vec(), vec()],
        out_specs=tok(),
        scratch_shapes=[pltpu.VMEM((B * N_PAIR, RWKV_HD, PAIR), F32)],
        compiler_params=_cparams(("arbitrary",)),
        name="rwkv_core",
    )(seq(rt), seq(bt), seq(at), seq(kt), seq(v), seq(bonus), seq(g),
      ptot.reshape(B, nc, 1, RWKV_W), lnx_w, lnx_b)
    return out.reshape(B * S, RWKV_W)


def _merge_kernel(x_ref, og_ref, or_ref, gb_ref, wb1_ref, wb2_ref, wo_ref, *rest):
    o_ref = rest[-1]
    gate_refs = rest[:-1]
    n = len(gate_refs) // 2
    tn = gate_refs[0].shape[1]
    D = x_ref.shape[1]
    y_gla = _dot(og_ref[...], wb1_ref[...])
    y_rw = _dot(or_ref[...], wb2_ref[...])
    merged = []
    for c in range(n):
        cols = slice(c * tn, (c + 1) * tn)
        g_gla = _sigmoid(gate_refs[c][...] + gb_ref[:, c * tn:(c + 1) * tn])
        g_rw = _sigmoid(gate_refs[n + c][...] + gb_ref[:, D + c * tn:D + (c + 1) * tn])
        merged.append((g_gla * y_gla[:, cols] + g_rw * y_rw[:, cols]).astype(BF16))
    o_ref[...] = x_ref[...] + _dot(jnp.concatenate(merged, axis=1), wo_ref[...])


def _merge(x, o_gla, o_rw, p, gate_b, w_branch, wo, *, tm=256, tn=PROJ_TN):
    T, D = x.shape
    const = lambda shape, r: pl.BlockSpec(shape, lambda i: (r, 0), pipeline_mode=pl.Buffered(1))
    assert GLA_V == RWKV_W and GATE_COL % tn == 0 and D % tn == 0
    n_gate_blocks = 2 * D // tn
    gate_specs = [pl.BlockSpec((tm, tn), lambda i, c=GATE_COL // tn + c: (i, c)) for c in range(n_gate_blocks)]
    return pl.pallas_call(
        _merge_kernel,
        out_shape=jax.ShapeDtypeStruct((T, D), F32),
        grid=(T // tm,),
        in_specs=[
            pl.BlockSpec((tm, D), lambda i: (i, 0)),
            pl.BlockSpec((tm, GLA_V), lambda i: (i, 0)),
            pl.BlockSpec((tm, RWKV_W), lambda i: (i, 0)),
            pl.BlockSpec((1, 2 * D), lambda i: (0, 0)),
            const((GLA_V, D), 0), const((RWKV_W, D), 1), const((D, D), 0),
        ] + gate_specs,
        out_specs=pl.BlockSpec((tm, D), lambda i: (i, 0)),
        compiler_params=_cparams(("parallel",)),
        name="merge_out",
    )(x, o_gla, o_rw, gate_b, w_branch, w_branch, wo, *([p] * n_gate_blocks))


def _pad_rows(w, n):
    return jnp.pad(w, ((0, n - w.shape[0]), (0, 0)))


def kernel(x, ffn1_norm, ffn1_wg, ffn1_wu, ffn1_wd, mix_norm, w_in, gla_w_a2, gla_b_a, gla_gn_w, rwkv_mu,
           rwkv_w0, rwkv_w_w2, rwkv_a0, rwkv_w_a2, rwkv_w_g2, rwkv_k_k, rwkv_k_a, rwkv_r_k, rwkv_lnx_w,
           rwkv_lnx_b, gate_b, w_branch, w_out, ffn2_norm, ffn2_wg, ffn2_wu, ffn2_wd, final_norm):
    B, S, D = x.shape
    T = B * S
    depth = ffn1_norm.shape[0]
    assert depth >= 1
    bf = lambda w: w.astype(BF16)
    row = lambda v: v.reshape(1, -1)
    assert D == D_MODEL and w_in.shape[2] == GLA_IN + RWKV_IN + 2 * D_MODEL
    xt = x.reshape(T, D)
    for l in range(depth):
        last = l == depth - 1
        xt, h_mix, w_p = _ffn(
            xt, row(ffn1_norm[l]), bf(ffn1_wg[l]), bf(ffn1_wu[l]), bf(ffn1_wd[l]), row(mix_norm[l]),
            mode=EMIT_NORM, aux=[(PACK_T, w_in[l].T)])

        mu_row = jnp.pad(rwkv_mu[l], (GLA_IN, 2 * D_MODEL))[None, :]
        mu = _pack_cols(mu_row)[:, :RW_GROUP]
        p, w_br, w_o, wg2, wu2, wd2 = _proj(
            h_mix, w_p, aux=[(CAST, w_branch[l]), (CAST, w_out[l]),
                             (CAST, ffn2_wg[l]), (CAST, ffn2_wu[l]), (CAST, ffn2_wd[l])])

        o_gla = _gla(p, bf(_pad_rows(gla_w_a2[l], LANE)), row(gla_b_a[l]), row(gla_gn_w[l]), B, S)

        prep = _rwkv_prep(p, mu, row(rwkv_w0[l]), bf(_pad_rows(rwkv_w_w2[l], LANE)), row(rwkv_a0[l]),
                          bf(_pad_rows(rwkv_w_a2[l], LANE)), bf(rwkv_w_g2[l]), row(rwkv_k_k[l]),
                          row(rwkv_k_a[l]), row(rwkv_r_k[l]), B, S)
        o_rw = _rwkv_core(*prep, row(rwkv_lnx_w[l]), row(rwkv_lnx_b[l]), B, S)

        xt = _merge(xt, o_gla, o_rw, p, row(gate_b[l]), w_br, w_o)

        xt = _ffn_pipelined(xt, row(ffn2_norm[l]), wg2, wu2, wd2, row(final_norm),
                            mode=FINAL if last else RESIDUAL)
    return xt.reshape(B, S, D)
```
